```python
import math
import jax
import jax.numpy as jnp
from jax import lax
import numpy as np

D_MODEL = 1024
BATCH = 8
SEQ = 2048
DEPTH = 2
DEC_BATCH = 128
DEC_SEQ = 4
PAST_LEN = 16384
PAGE_SIZE = 128

N_META = 16
MIX_WIDTH = D_MODEL
GDN_WIDTH = MIX_WIDTH // 2
CONF_WIDTH = MIX_WIDTH - GDN_WIDTH
GDN_HEAD_DIM = 128
GDN_HEADS = GDN_WIDTH // GDN_HEAD_DIM
GDN_CONV = 4
GDN_CHUNK = 64
QKV_WIDTH = 3 * GDN_WIDTH
CONF_CONV = 31
IN_WIDTH = QKV_WIDTH + GDN_WIDTH + 2 * GDN_HEADS + 2 * CONF_WIDTH
N_KEYS = 128
N_EXPERTS = N_KEYS * N_KEYS
PEER_HEADS = 8
PEER_KEY_DIM = 128
PEER_TOPK = 16
PEER_BLOCK = 128
DT_MIN = 0.001
DT_MAX = 0.1
EPS = 1e-6

kernel_name = 'gdn_conformer_peer_hybrid_step'


def rmsnorm(x, w):
    xf = x.astype(jnp.float32)
    y = xf * lax.rsqrt(jnp.mean(xf * xf, axis=-1, keepdims=True) + EPS)
    return (y * w.astype(jnp.float32)).astype(x.dtype)


def layernorm(x, w, b):
    xf = x.astype(jnp.float32)
    mu = jnp.mean(xf, axis=-1, keepdims=True)
    xc = xf - mu
    var = jnp.mean(xc * xc, axis=-1, keepdims=True)
    return xc * lax.rsqrt(var + EPS) * w.astype(jnp.float32) + b.astype(jnp.float32)


def l2norm(x):
    return x * lax.rsqrt(jnp.sum(x * x, axis=-1, keepdims=True) + EPS)


def causal_dwconv(x_ext, w):
    channels = x_ext.shape[-1]
    return lax.conv_general_dilated(
        x_ext, w[:, None, :], window_strides=(1,), padding='VALID',
        dimension_numbers=('NWC', 'WIO', 'NWC'), feature_group_count=channels)


def gdn_chunk(S, q, k, v, g, beta):
    C = q.shape[2]
    dv = v.shape[-1]
    G = jnp.cumsum(g, axis=-1)
    incl = jnp.tril(jnp.ones((C, C), dtype=bool))
    strict = jnp.tril(jnp.ones((C, C), dtype=bool), -1)
    decay = jnp.exp(jnp.where(incl, G[..., :, None] - G[..., None, :], -jnp.inf))
    kb = k * beta[..., None]
    L = jnp.where(strict, jnp.einsum('bhik,bhjk->bhij', kb, k) * decay, 0.0)
    rhs = jnp.concatenate([v * beta[..., None], kb * jnp.exp(G)[..., None]], axis=-1)
    sol = lax.linalg.triangular_solve(L, rhs, left_side=True, lower=True, unit_diagonal=True)
    u_v, w_k = sol[..., :dv], sol[..., dv:]
    v_new = u_v - jnp.einsum('bhck,bhkv->bhcv', w_k, S)
    attn = jnp.einsum('bhik,bhjk->bhij', q, k) * decay
    o = (jnp.einsum('bhck,bhkv->bhcv', q * jnp.exp(G)[..., None], S)
         + jnp.einsum('bhij,bhjv->bhiv', attn, v_new))
    G_last = G[..., -1:]
    S_new = (S * jnp.exp(G_last)[..., None]
             + jnp.einsum('bhck,bhcv->bhkv', k * jnp.exp(G_last - G)[..., None], v_new))
    return S_new, o


def gdn_prompt(S0, q, k, v, g, beta):
    B, H, T, dk = q.shape
    dv = v.shape[-1]
    S1, o_meta = gdn_chunk(S0, q[:, :, :N_META], k[:, :, :N_META], v[:, :, :N_META],
                           g[:, :, :N_META], beta[:, :, :N_META])
    n_chunks = (T - N_META) // GDN_CHUNK

    def to_chunks(a):
        a = a[:, :, N_META:]
        a = a.reshape((B, H, n_chunks, GDN_CHUNK) + a.shape[3:])
        return jnp.moveaxis(a, 2, 0)

    def body(S, inp):
        return gdn_chunk(S, *inp)

    S_fin, o_real = lax.scan(body, S1, (to_chunks(q), to_chunks(k), to_chunks(v),
                                          to_chunks(g), to_chunks(beta)))
    o_real = jnp.moveaxis(o_real, 0, 2).reshape(B, H, n_chunks * GDN_CHUNK, dv)
    return S_fin, jnp.concatenate([o_meta, o_real], axis=2)


def gdn_recurrent(S0, q, k, v, g, beta):
    def step(S, inp):
        q_t, k_t, v_t, g_t, b_t = inp
        S = S * jnp.exp(g_t)[..., None, None]
        u = b_t[..., None] * (v_t - jnp.einsum('bhkv,bhk->bhv', S, k_t))
        S = S + jnp.einsum('bhk,bhv->bhkv', k_t, u)
        return S, jnp.einsum('bhkv,bhk->bhv', S, q_t)

    xs = (jnp.moveaxis(q, 2, 0), jnp.moveaxis(k, 2, 0), jnp.moveaxis(v, 2, 0),
          jnp.moveaxis(g, 2, 0), jnp.moveaxis(beta, 2, 0))
    S_fin, o = lax.scan(step, S0, xs)
    return S_fin, jnp.moveaxis(o, 0, 2)


def token_mixer(xn, S0, qkv_buf, conf_buf, w_in, conv_qkv_w, a_log, dt_bias, gdn_norm_w,
                conf_dw_w, conf_dw_b, conf_ln_w, conf_ln_b, w_out, prompt):
    B, T, _ = xn.shape
    proj = xn @ w_in
    s0 = QKV_WIDTH
    s1 = s0 + GDN_WIDTH
    s2 = s1 + GDN_HEADS
    s3 = s2 + GDN_HEADS
    qkv, z, a, b, glu_in = proj[..., :s0], proj[..., s0:s1], proj[..., s1:s2], proj[..., s2:s3], proj[..., s3:]

    qkv_ext = jnp.concatenate([qkv_buf.astype(qkv.dtype), qkv], axis=1)
    qkv_c = jax.nn.silu(causal_dwconv(qkv_ext, conv_qkv_w.astype(qkv.dtype)))
    new_qkv_buf = qkv_ext[:, -(GDN_CONV - 1):]

    def heads(t):
        return t.reshape(B, T, GDN_HEADS, GDN_HEAD_DIM).transpose(0, 2, 1, 3).astype(jnp.float32)

    q = l2norm(heads(qkv_c[..., :GDN_WIDTH])) * (GDN_HEAD_DIM ** -0.5)
    k = l2norm(heads(qkv_c[..., GDN_WIDTH:2 * GDN_WIDTH]))
    v = heads(qkv_c[..., 2 * GDN_WIDTH:])
    g = (-jnp.exp(a_log.astype(jnp.float32))
         * jax.nn.softplus(a.astype(jnp.float32) + dt_bias.astype(jnp.float32))).transpose(0, 2, 1)
    beta = jax.nn.sigmoid(b.astype(jnp.float32)).transpose(0, 2, 1)
    S0f = S0.astype(jnp.float32)
    if prompt:
        S_new, o = gdn_prompt(S0f, q, k, v, g, beta)
    else:
        S_new, o = gdn_recurrent(S0f, q, k, v, g, beta)
    o = rmsnorm(o.transpose(0, 2, 1, 3), gdn_norm_w) * jax.nn.silu(
        z.reshape(B, T, GDN_HEADS, GDN_HEAD_DIM).astype(jnp.float32))
    o = o.reshape(B, T, GDN_WIDTH).astype(xn.dtype)

    glu = glu_in[..., :CONF_WIDTH] * jax.nn.sigmoid(glu_in[..., CONF_WIDTH:])
    c_ext = jnp.concatenate([conf_buf.astype(glu.dtype), glu], axis=1)
    c = causal_dwconv(c_ext, conf_dw_w.astype(glu.dtype)) + conf_dw_b
    new_conf_buf = c_ext[:, -(CONF_CONV - 1):]
    c = jax.nn.silu(layernorm(c, conf_ln_w, conf_ln_b)).astype(xn.dtype)

    out = jnp.concatenate([o, c], axis=-1) @ w_out
    return out, S_new.astype(xn.dtype), new_qkv_buf, new_conf_buf


def peer_ffn(x, w_query, sub_keys, expert_u, expert_v):
    B, T, D = x.shape
    xt = x.reshape(B * T, D)
    n = xt.shape[0]
    q = (xt @ w_query).reshape(n, PEER_HEADS, 2, PEER_KEY_DIM).astype(jnp.float32)
    s = jnp.einsum('nhpd,hpkd->nhpk', q, sub_keys.astype(jnp.float32))
    s1, i1 = lax.top_k(s[:, :, 0], PEER_TOPK)
    s2, i2 = lax.top_k(s[:, :, 1], PEER_TOPK)
    cand = (s1[..., :, None] + s2[..., None, :]).reshape(n, PEER_HEADS, PEER_TOPK * PEER_TOPK)
    cidx = (i1[..., :, None] * N_KEYS + i2[..., None, :]).reshape(n, PEER_HEADS, PEER_TOPK * PEER_TOPK)
    top, pos = lax.top_k(cand, PEER_TOPK)
    eidx = jnp.take_along_axis(cidx, pos, axis=-1)
    gates = jax.nn.softmax(top, axis=-1).astype(x.dtype)

    pad = (-n) % PEER_BLOCK
    nb = (n + pad) // PEER_BLOCK
    xp = jnp.pad(xt, ((0, pad), (0, 0))).reshape(nb, PEER_BLOCK, D)
    ep = jnp.pad(eidx, ((0, pad), (0, 0), (0, 0))).reshape(nb, PEER_BLOCK, PEER_HEADS, PEER_TOPK)
    gp = jnp.pad(gates, ((0, pad), (0, 0), (0, 0))).reshape(nb, PEER_BLOCK, PEER_HEADS, PEER_TOPK)

    def block(args):
        xb, eb, gb = args
        u = expert_u[eb]
        act = jax.nn.gelu(jnp.einsum('nd,nhed->nhe', xb, u), approximate=False)
        return jnp.einsum('nhe,nhed->nd', gb * act, expert_v[eb])

    y = lax.map(block, (xp, ep, gp))
    return y.reshape(nb * PEER_BLOCK, D)[:n].reshape(B, T, D)


def setup_inputs(seed: int = 0) -> dict:
    key = jax.random.key(seed)
    ks = jax.random.split(key, 24)
    f32 = jnp.float32

    def nrm(k, shape, scale):
        return scale * jax.random.normal(k, shape, f32)

    dt = jnp.exp(jax.random.uniform(ks[7], (DEPTH, GDN_HEADS), f32, math.log(DT_MIN), math.log(DT_MAX)))
    return {
        'x_prompt': nrm(ks[0], (BATCH, SEQ, D_MODEL), 1.0),
        'x_sample': nrm(ks[1], (DEC_BATCH, DEC_SEQ, D_MODEL), 1.0),
        'state_gdn': nrm(ks[2], (DEPTH, DEC_BATCH, GDN_HEADS, GDN_HEAD_DIM, GDN_HEAD_DIM), 0.05),
        'state_qkv_conv': nrm(ks[3], (DEPTH, DEC_BATCH, GDN_CONV - 1, QKV_WIDTH), 1.0),
        'state_conf_conv': nrm(ks[4], (DEPTH, DEC_BATCH, CONF_CONV - 1, CONF_WIDTH), 1.0),
        'meta_tokens': nrm(ks[5], (N_META, D_MODEL), 1.0),
        'norm_mix_w': 1.0 + nrm(ks[6], (DEPTH, D_MODEL), 0.02),
        'w_in': nrm(ks[8], (DEPTH, D_MODEL, IN_WIDTH), D_MODEL ** -0.5),
        'conv_qkv_w': nrm(ks[9], (DEPTH, GDN_CONV, QKV_WIDTH), GDN_CONV ** -0.5),
        'a_log': jnp.log(jax.random.uniform(ks[10], (DEPTH, GDN_HEADS), f32, 1.0, 16.0)),
        'dt_bias': dt + jnp.log(-jnp.expm1(-dt)),
        'gdn_norm_w': 1.0 + nrm(ks[11], (DEPTH, GDN_HEAD_DIM), 0.02),
        'conf_dw_w': nrm(ks[12], (DEPTH, CONF_CONV, CONF_WIDTH), CONF_CONV ** -0.5),
        'conf_dw_b': nrm(ks[13], (DEPTH, CONF_WIDTH), 0.02),
        'conf_ln_w': 1.0 + nrm(ks[14], (DEPTH, CONF_WIDTH), 0.02),
        'conf_ln_b': nrm(ks[15], (DEPTH, CONF_WIDTH), 0.02),
        'w_out': nrm(ks[16], (DEPTH, MIX_WIDTH, D_MODEL), MIX_WIDTH ** -0.5),
        'norm_ffn_w': 1.0 + nrm(ks[17], (DEPTH, D_MODEL), 0.02),
        'w_query': nrm(ks[18], (DEPTH, D_MODEL, PEER_HEADS * 2 * PEER_KEY_DIM), D_MODEL ** -0.5),
        'sub_keys': nrm(ks[19], (DEPTH, PEER_HEADS, 2, N_KEYS, PEER_KEY_DIM), PEER_KEY_DIM ** -0.5),
        'expert_u': nrm(ks[20], (DEPTH, N_EXPERTS, D_MODEL), D_MODEL ** -0.5),
        'expert_v': nrm(ks[21], (DEPTH, N_EXPERTS, D_MODEL), PEER_HEADS ** -0.5),
        'final_norm_w': 1.0 + nrm(ks[22], (D_MODEL,), 0.02),
    }


def reference(x_prompt, x_sample, state_gdn, state_qkv_conv, state_conf_conv, meta_tokens,
              norm_mix_w, w_in, conv_qkv_w, a_log, dt_bias, gdn_norm_w, conf_dw_w, conf_dw_b,
              conf_ln_w, conf_ln_b, w_out, norm_ffn_w, w_query, sub_keys, expert_u, expert_v,
              final_norm_w):
    def run_trunk(h, S_in, qkv_in, conf_in, prompt):
        S_out, qkv_out, conf_out = [], [], []
        for layer in range(DEPTH):
            mix, S_l, qb_l, cb_l = token_mixer(
                rmsnorm(h, norm_mix_w[layer]), S_in[layer], qkv_in[layer], conf_in[layer],
                w_in[layer], conv_qkv_w[layer], a_log[layer], dt_bias[layer], gdn_norm_w[layer],
                conf_dw_w[layer], conf_dw_b[layer], conf_ln_w[layer], conf_ln_b[layer],
                w_out[layer], prompt)
            h = h + mix
            h = h + peer_ffn(rmsnorm(h, norm_ffn_w[layer]), w_query[layer], sub_keys[layer],
                             expert_u[layer], expert_v[layer])
            S_out.append(S_l)
            qkv_out.append(qb_l)
            conf_out.append(cb_l)
        return (rmsnorm(h, final_norm_w), jnp.stack(S_out), jnp.stack(qkv_out), jnp.stack(conf_out))

    dt_p = x_prompt.dtype
    meta = jnp.broadcast_to(meta_tokens.astype(dt_p)[None], (BATCH, N_META, D_MODEL))
    h_p = jnp.concatenate([meta, x_prompt], axis=1)
    zS = jnp.zeros((DEPTH, BATCH, GDN_HEADS, GDN_HEAD_DIM, GDN_HEAD_DIM), jnp.float32)
    zq = jnp.zeros((DEPTH, BATCH, GDN_CONV - 1, QKV_WIDTH), dt_p)
    zc = jnp.zeros((DEPTH, BATCH, CONF_CONV - 1, CONF_WIDTH), dt_p)
    out_p, new_gdn_prompt, new_qkv_conv_prompt, new_conf_conv_prompt = run_trunk(h_p, zS, zq, zc, True)
    y_prompt = out_p[:, N_META:]

    y_sample, new_gdn_sample, new_qkv_conv_sample, new_conf_conv_sample = run_trunk(
        x_sample, state_gdn, state_qkv_conv, state_conf_conv, False)

    return (y_prompt, y_sample, new_gdn_prompt, new_qkv_conv_prompt, new_conf_conv_prompt,
            new_gdn_sample, new_qkv_conv_sample, new_conf_conv_sample)
```

```python
import functools
import math

import jax
import jax.numpy as jnp
from jax import lax
from jax.experimental import pallas as pl
from jax.experimental.pallas import tpu as pltpu

F32 = jnp.float32
BF16 = jnp.bfloat16
HIGHEST = lax.Precision.HIGHEST

D_MODEL = 1024
N_META = 16
GDN_HEADS = 4
HEAD_DIM = 128
GDN_WIDTH = GDN_HEADS * HEAD_DIM
CONF_WIDTH = 512
QKV_WIDTH = 3 * GDN_WIDTH
GDN_CONV = 4
GDN_CHUNK = 64
CONF_CONV = 31
N_KEYS = 128
PEER_HEADS = 8
PEER_TOPK = 16
EPS = 1e-6

LANES = 128
SUBLANES = 8
VMEM_LIMIT_BYTES = 56 * 1024 * 1024


def _sigmoid(x):
    return 1.0 / (1.0 + jnp.exp(-x))


def _silu(x):
    return x * _sigmoid(x)


def _softplus(x):
    return jnp.maximum(x, 0.0) + jnp.log1p(jnp.exp(-jnp.abs(x)))


def _rmsnorm(x, w):
    return x * lax.rsqrt(jnp.mean(x * x, axis=-1, keepdims=True) + EPS) * w


def _dot(a, b, precision=None):
    return jnp.dot(a, b, preferred_element_type=F32, precision=precision)


def _dot_nt(a, b, precision=None):
    return lax.dot_general(a, b, (((1,), (1,)), ((), ())), preferred_element_type=F32,
                           precision=precision)


def _dot_tn(a, b, precision=None):
    return lax.dot_general(a, b, (((0,), (0,)), ((), ())), preferred_element_type=F32,
                           precision=precision)


def _pick_tile(n, candidates):
    for c in candidates:
        if n % c == 0:
            return c
    return n


def _params(*semantics):
    return pltpu.CompilerParams(dimension_semantics=semantics, vmem_limit_bytes=VMEM_LIMIT_BYTES)


def _in_proj_kernel(h_ref, nw_ref, wqkvz_ref, wab_ref, wglu_ref, alog_ref, dtb_ref,
                    qkv_ref, z_ref, g_ref, beta_ref, glu_ref):
    xb = _rmsnorm(h_ref[...], nw_ref[...]).astype(BF16)
    p = _dot(xb, wqkvz_ref[...])
    qkv_ref[...] = p[:, :QKV_WIDTH]
    z_ref[...] = p[:, QKV_WIDTH:]
    ab = _dot(xb, wab_ref[...])
    a = ab[:, :GDN_WIDTH]
    b = ab[:, GDN_WIDTH:]
    g_ref[...] = -jnp.exp(alog_ref[...]) * _softplus(a + dtb_ref[...])
    beta_ref[...] = _sigmoid(b)
    pg = _dot(xb, wglu_ref[...])
    glu_ref[...] = pg[:, :CONF_WIDTH] * _sigmoid(pg[:, CONF_WIDTH:])


def _in_proj(h, nw, wqkvz, wab, wglu, alog, dtb):
    n = h.shape[0]
    tm = _pick_tile(n, (384, 256, 128))
    row = lambda i: (i, 0)
    fixed = lambda i: (0, 0)
    widths = (QKV_WIDTH, GDN_WIDTH, GDN_WIDTH, GDN_WIDTH, CONF_WIDTH)
    return pl.pallas_call(
        _in_proj_kernel,
        grid=(n // tm,),
        in_specs=[
            pl.BlockSpec((tm, D_MODEL), row),
            pl.BlockSpec((1, D_MODEL), fixed),
            pl.BlockSpec(wqkvz.shape, fixed),
            pl.BlockSpec(wab.shape, fixed),
            pl.BlockSpec(wglu.shape, fixed),
            pl.BlockSpec((1, GDN_WIDTH), fixed),
            pl.BlockSpec((1, GDN_WIDTH), fixed),
        ],
        out_specs=[pl.BlockSpec((tm, w), row) for w in widths],
        out_shape=[jax.ShapeDtypeStruct((n, w), F32) for w in widths],
        compiler_params=_params("parallel"),
        name="in_proj",
    )(h, nw, wqkvz, wab, wglu, alog, dtb)


def _gdn_kernel(xq_ref, xk_ref, xv_ref, g_ref, beta_ref, z_ref, cs_q_ref, cs_k_ref, cs_v_ref, s0_ref,
                cwq_ref, cwk_ref, cwv_ref, nw_ref,
                o_ref, s_ref,
                xq_s, xk_s, xv_s, g_s, b_s, u_s, w_s, att_s, qg_s, a_s, bm_s, sall_s,
                *, chunk, n_pad, seq, bb_count):
    C = chunk
    n_chunks = (seq + n_pad) // C
    head = 8 + n_pad
    n_fac = int(math.log2(C))
    row_i = lax.broadcasted_iota(jnp.int32, (C, C), 0)
    col_i = lax.broadcasted_iota(jnp.int32, (C, C), 1)
    incl = row_i >= col_i
    strict = row_i > col_i
    tril = jnp.where(incl, 1.0, 0.0).astype(F32)
    eye_hd = (lax.broadcasted_iota(jnp.int32, (HEAD_DIM, HEAD_DIM), 0)
              == lax.broadcasted_iota(jnp.int32, (HEAD_DIM, HEAD_DIM), 1))
    live = lax.broadcasted_iota(jnp.int32, (C, HEAD_DIM), 0) >= n_pad
    cws = (cwq_ref[...], cwk_ref[...], cwv_ref[...])
    nw = nw_ref[...]

    def conv(blk, cw):
        acc = blk[5:5 + C] * cw[0:1]
        for j in range(1, GDN_CONV):
            acc = acc + blk[5 + j:5 + j + C] * cw[j:j + 1]
        return _silu(acc)

    def build(c, first):
        t0 = c * C if first else pl.multiple_of(c * C, SUBLANES)
        qc = conv(xq_s[pl.ds(t0, C + 8), :], cws[0])
        kc = conv(xk_s[pl.ds(t0, C + 8), :], cws[1])
        v = conv(xv_s[pl.ds(t0, C + 8), :], cws[2])
        q = qc * lax.rsqrt(jnp.sum(qc * qc, axis=-1, keepdims=True) + EPS) * (HEAD_DIM ** -0.5)
        k = kc * lax.rsqrt(jnp.sum(kc * kc, axis=-1, keepdims=True) + EPS)
        g = g_s[pl.ds(t0, C), :]
        beta = b_s[pl.ds(t0, C), :]
        if first and n_pad:
            q = jnp.where(live, q, 0.0)
            k = jnp.where(live, k, 0.0)
            v = jnp.where(live, v, 0.0)
        gl = g[:, :C]
        dmat = _dot(tril, jnp.where(strict, gl, 0.0), HIGHEST)
        gcum = _dot(tril, g, HIGHEST)
        decay = jnp.where(incl, jnp.exp(dmat), 0.0)
        eg = jnp.exp(gcum)
        kb = k * beta
        neg_l = jnp.where(strict, -(_dot_nt(kb, k, HIGHEST) * decay), 0.0)
        x = jnp.concatenate([v * beta, kb * eg], axis=1)
        x = x + _dot(neg_l, x, HIGHEST)
        npow = neg_l
        for _ in range(n_fac - 1):
            npow = _dot(npow, npow, HIGHEST)
            x = x + _dot(npow, x, HIGHEST)
        u = x[:, :HEAD_DIM]
        w = x[:, HEAD_DIM:]
        att = jnp.where(incl, _dot_nt(q, k, HIGHEST) * decay, 0.0)
        g_last = gcum[C - 1:C, :]
        kt = k * jnp.exp(g_last - gcum)
        a_mat = jnp.where(eye_hd, jnp.exp(g_last), 0.0) - _dot_tn(kt, w, HIGHEST)
        u_s[c] = u
        w_s[c] = w
        att_s[c] = att
        qg_s[c] = q * eg
        a_s[c] = a_mat
        bm_s[c] = _dot_tn(kt, u, HIGHEST)

    def emit(c, first, bb):
        s = sall_s[c]
        v_new = u_s[c] - _dot(w_s[c], s, HIGHEST)
        o = _dot(qg_s[c], s, HIGHEST) + _dot(att_s[c], v_new, HIGHEST)
        o = _rmsnorm(o, nw)
        if first:
            rows = C - n_pad
            zc = z_ref[bb, 0:rows, :]
            o_ref[bb, 0:rows, :] = o[n_pad:] * _silu(zc)
        else:
            r0 = pl.multiple_of(c * C - n_pad, SUBLANES)
            zc = z_ref[bb, pl.ds(r0, C), :]
            o_ref[bb, pl.ds(r0, C), :] = o * _silu(zc)

    for bb in range(bb_count):
        for x_s, x_ref, cs_ref in ((xq_s, xq_ref, cs_q_ref), (xk_s, xk_ref, cs_k_ref),
                                   (xv_s, xv_ref, cs_v_ref)):
            x_s[0:head, :] = jnp.zeros((head, HEAD_DIM), F32)
            x_s[head - (GDN_CONV - 1):head, :] = cs_ref[bb]
            x_s[head:head + seq, :] = x_ref[bb]
        if n_pad:
            g_s[0:n_pad, :] = jnp.zeros((n_pad, HEAD_DIM), F32)
            b_s[0:n_pad, :] = jnp.zeros((n_pad, HEAD_DIM), F32)
        g_s[n_pad:n_pad + seq, :] = g_ref[bb]
        b_s[n_pad:n_pad + seq, :] = beta_ref[bb]

        build(0, True)
        if n_chunks > 1:
            lax.fori_loop(1, n_chunks, lambda c, _: (build(c, False), 0)[1], 0)

        s_ref[bb, 0] = s0_ref[bb, 0]

        def carry(c, _):
            s = s_ref[bb, 0]
            sall_s[c] = s
            s_ref[bb, 0] = _dot(a_s[c], s, HIGHEST) + bm_s[c]
            return 0

        lax.fori_loop(0, n_chunks, carry, 0)

        emit(0, True, bb)
        if n_chunks > 1:
            lax.fori_loop(1, n_chunks, lambda c, _: (emit(c, False, bb), 0)[1], 0)


def _gdn(qkv, g, beta, z, conv_state, s0, conv_w, norm_w, *, chunk, n_pad, bb_count):
    b, seq, _ = qkv.shape
    n_chunks = (seq + n_pad) // chunk
    assert n_chunks * chunk == seq + n_pad and b % bb_count == 0
    col = lambda off: (lambda i, h: (i, 0, off + h))
    wcol = lambda off: (lambda i, h: (0, off + h))
    seq_blk = (bb_count, seq, HEAD_DIM)
    cs_blk = (bb_count, GDN_CONV - 1, HEAD_DIM)
    s_blk = (bb_count, 1, HEAD_DIM, HEAD_DIM)
    s_map = lambda i, h: (i, h, 0, 0)
    ext = chunk * n_chunks
    per_chunk = lambda r, c: pltpu.VMEM((n_chunks, r, c), F32)
    kern = functools.partial(_gdn_kernel, chunk=chunk, n_pad=n_pad, seq=seq, bb_count=bb_count)
    return pl.pallas_call(
        kern,
        grid=(b // bb_count, GDN_HEADS),
        in_specs=[
            pl.BlockSpec(seq_blk, col(0)),
            pl.BlockSpec(seq_blk, col(GDN_HEADS)),
            pl.BlockSpec(seq_blk, col(2 * GDN_HEADS)),
            pl.BlockSpec(seq_blk, col(0)),
            pl.BlockSpec(seq_blk, col(0)),
            pl.BlockSpec(seq_blk, col(0)),
            pl.BlockSpec(cs_blk, col(0)),
            pl.BlockSpec(cs_blk, col(GDN_HEADS)),
            pl.BlockSpec(cs_blk, col(2 * GDN_HEADS)),
            pl.BlockSpec(s_blk, s_map),
            pl.BlockSpec((GDN_CONV, HEAD_DIM), wcol(0)),
            pl.BlockSpec((GDN_CONV, HEAD_DIM), wcol(GDN_HEADS)),
            pl.BlockSpec((GDN_CONV, HEAD_DIM), wcol(2 * GDN_HEADS)),
            pl.BlockSpec((1, HEAD_DIM), lambda i, h: (0, 0)),
        ],
        out_specs=[pl.BlockSpec(seq_blk, col(0)), pl.BlockSpec(s_blk, s_map)],
        out_shape=[jax.ShapeDtypeStruct((b, seq, GDN_WIDTH), F32),
                   jax.ShapeDtypeStruct((b, GDN_HEADS, HEAD_DIM, HEAD_DIM), F32)],
        scratch_shapes=[
            pltpu.VMEM((ext + 8, HEAD_DIM), F32),
            pltpu.VMEM((ext + 8, HEAD_DIM), F32),
            pltpu.VMEM((ext + 8, HEAD_DIM), F32),
            pltpu.VMEM((ext, HEAD_DIM), F32),
            pltpu.VMEM((ext, HEAD_DIM), F32),
            per_chunk(chunk, HEAD_DIM),
            per_chunk(chunk, HEAD_DIM),
            per_chunk(chunk, chunk),
            per_chunk(chunk, HEAD_DIM),
            per_chunk(HEAD_DIM, HEAD_DIM),
            per_chunk(HEAD_DIM, HEAD_DIM),
            per_chunk(HEAD_DIM, HEAD_DIM),
        ],
        compiler_params=_params("parallel", "parallel"),
        name="gdn",
    )(qkv, qkv, qkv, g, beta, z, conv_state, conv_state, conv_state, s0,
      conv_w, conv_w, conv_w, norm_w)


CONF_HALO = 32
CONF_ROWS = 16


def _conf_out_seq_kernel(glu_ref, o_ref, h_ref, cstate_ref, dw_ref, dwb_ref, lnw_ref, lnb_ref, wout_ref,
                         out_ref, ext_s, c_s, *, tt):
    t = pl.program_id(1)
    lead = CONF_HALO - (CONF_CONV - 1)

    @pl.when(t == 0)
    def _():
        ext_s[0:lead, :] = jnp.zeros((lead, CONF_WIDTH), F32)
        ext_s[lead:CONF_HALO, :] = cstate_ref[0]

    @pl.when(t > 0)
    def _():
        ext_s[0:CONF_HALO, :] = ext_s[tt:tt + CONF_HALO, :]

    ext_s[CONF_HALO:CONF_HALO + tt, :] = glu_ref[0]
    dw = dw_ref[...]
    bias = dwb_ref[...]

    def rows(i, _):
        r0 = pl.multiple_of(i * CONF_ROWS, SUBLANES)
        blk = ext_s[pl.ds(r0, CONF_ROWS + CONF_HALO), :]
        acc = bias + blk[lead:lead + CONF_ROWS] * dw[0:1]
        for j in range(1, CONF_CONV):
            acc = acc + blk[lead + j:lead + j + CONF_ROWS] * dw[j:j + 1]
        c_s[pl.ds(r0, CONF_ROWS), :] = acc
        return 0

    lax.fori_loop(0, tt // CONF_ROWS, rows, 0)
    c = c_s[...]
    mu = jnp.mean(c, axis=-1, keepdims=True)
    xc = c - mu
    var = jnp.mean(xc * xc, axis=-1, keepdims=True)
    c = _silu(xc * lax.rsqrt(var + EPS) * lnw_ref[...] + lnb_ref[...])
    mixed = jnp.concatenate([o_ref[0], c], axis=1).astype(BF16)
    out_ref[0] = h_ref[0] + _dot(mixed, wout_ref[...])


def _conf_out_seq(glu, o, h, cstate, dw, dwb, lnw, lnb, wout):
    b, seq, _ = glu.shape
    tt = _pick_tile(seq, (688, 512, 256, 128, 64, 16))
    tile = lambda w: pl.BlockSpec((1, tt, w), lambda i, t: (i, t, 0))
    fixed = lambda a: pl.BlockSpec(a.shape, lambda i, t: (0,) * a.ndim)
    return pl.pallas_call(
        functools.partial(_conf_out_seq_kernel, tt=tt),
        grid=(b, seq // tt),
        in_specs=[tile(CONF_WIDTH), tile(GDN_WIDTH), tile(D_MODEL),
                  pl.BlockSpec((1, CONF_CONV - 1, CONF_WIDTH), lambda i, t: (i, 0, 0)),
                  fixed(dw), fixed(dwb), fixed(lnw), fixed(lnb), fixed(wout)],
        out_specs=tile(D_MODEL),
        out_shape=jax.ShapeDtypeStruct((b, seq, D_MODEL), F32),
        scratch_shapes=[pltpu.VMEM((tt + CONF_HALO, CONF_WIDTH), F32),
                        pltpu.VMEM((tt, CONF_WIDTH), F32)],
        compiler_params=_params("parallel", "arbitrary"),
        name="conf_out_seq",
    )(glu, o, h, cstate, dw, dwb, lnw, lnb, wout)


def _conf_out_step_kernel(cext_ref, o_ref, h_ref, dw_ref, dwb_ref, lnw_ref, lnb_ref, wout_ref, out_ref,
                          *, steps):
    dw = dw_ref[...]
    outs = []
    for t in range(steps):
        acc = dwb_ref[...] + cext_ref[t] * dw[0:1]
        for j in range(1, CONF_CONV):
            acc = acc + cext_ref[t + j] * dw[j:j + 1]
        outs.append(acc)
    c = jnp.concatenate(outs, axis=0)
    mu = jnp.mean(c, axis=-1, keepdims=True)
    xc = c - mu
    var = jnp.mean(xc * xc, axis=-1, keepdims=True)
    c = _silu(xc * lax.rsqrt(var + EPS) * lnw_ref[...] + lnb_ref[...])
    mixed = jnp.concatenate([o_ref[...], c], axis=1).astype(BF16)
    out_ref[...] = h_ref[...] + _dot(mixed, wout_ref[...])


def _conf_out_step(cext, o, h, dw, dwb, lnw, lnb, wout):
    steps = cext.shape[0] - (CONF_CONV - 1)
    n = h.shape[0]
    full = lambda a: pl.BlockSpec(a.shape, lambda i: (0,) * a.ndim)
    args = (cext, o, h, dw, dwb, lnw, lnb, wout)
    return pl.pallas_call(
        functools.partial(_conf_out_step_kernel, steps=steps),
        grid=(1,),
        in_specs=[full(a) for a in args],
        out_specs=pl.BlockSpec((n, D_MODEL), lambda i: (0, 0)),
        out_shape=jax.ShapeDtypeStruct((n, D_MODEL), F32),
        compiler_params=_params("arbitrary"),
        name="conf_out_step",
    )(*args)


N_RANKS = PEER_TOPK + 1
RANK_ROWS = 24
NEG_INF = float("-inf")


def _top_rows(s, count):
    rows = []
    for _ in range(count):
        m = jnp.max(s, axis=0, keepdims=True)
        rows.append(m)
        s = jnp.where(s == m, NEG_INF, s)
    return rows


def _stack_rows(rows, height, tm):
    idx = lax.broadcasted_iota(jnp.int32, (height, tm), 0)
    out = jnp.full((height, tm), NEG_INF, F32)
    for r, row in enumerate(rows):
        out = jnp.where(idx == r, row, out)
    return out


def _peer_score_kernel(h_ref, nw_ref, wqt_ref, keys_ref, xnt_ref, thr_ref, e1_ref, s2_ref, e2_ref):
    tm = h_ref.shape[0]
    xn = _rmsnorm(h_ref[...], nw_ref[...])
    xnt = xn.T.astype(BF16)
    xnt_ref[...] = xnt
    qt = _dot(wqt_ref[...], xnt).astype(BF16)
    idx8 = lax.broadcasted_iota(jnp.int32, (SUBLANES, tm), 0)
    for hh in range(PEER_HEADS):
        s = []
        for p in range(2):
            r0 = (hh * 2 + p) * N_KEYS
            s.append(_dot(keys_ref[hh, p], qt[r0:r0 + N_KEYS, :]))
        a = _top_rows(s[0], N_RANKS)
        b = _top_rows(s[1], N_RANKS)
        a_st = _stack_rows(a, RANK_ROWS, tm)
        b_st = _stack_rows(b, RANK_ROWS, tm)
        cands = [a[0] + b_st,
                 jnp.where(lax.broadcasted_iota(jnp.int32, (RANK_ROWS, tm), 0) >= 1,
                           a_st + b[0], NEG_INF)]
        for r1 in range(1, N_RANKS):
            hi = N_RANKS // (r1 + 1) - 1
            if hi >= 1 and r1 <= 4:
                cands.append(jnp.where((idx8 >= 1) & (idx8 <= hi), a[r1] + b_st[0:SUBLANES], NEG_INF))
        cands.append(jnp.where((idx8 >= 5) & (idx8 <= N_RANKS // 2 - 1),
                               a_st[0:SUBLANES] + b[1], NEG_INF))
        cand = jnp.concatenate(cands, axis=0)
        top = _top_rows(cand, N_RANKS)
        z = jnp.zeros_like(top[0])
        for r in range(PEER_TOPK):
            z = z + jnp.exp(top[r] - top[0])
        thr = 0.5 * (top[PEER_TOPK - 1] + top[PEER_TOPK])
        thr_ref[hh] = thr - s[0]
        e1_ref[hh] = jnp.exp(s[0] - a[0]) / z
        s2_ref[hh] = s[1]
        e2_ref[hh] = jnp.exp(s[1] - b[0])


def _peer_score(h, nw, wqt, keys):
    n = h.shape[0]
    tm = _pick_tile(n, (384, 256, 128))
    heads = lambda: pl.BlockSpec((PEER_HEADS, N_KEYS, tm), lambda i: (0, 0, i))
    return pl.pallas_call(
        _peer_score_kernel,
        grid=(n // tm,),
        in_specs=[pl.BlockSpec((tm, D_MODEL), lambda i: (i, 0)),
                  pl.BlockSpec((1, D_MODEL), lambda i: (0, 0)),
                  pl.BlockSpec(wqt.shape, lambda i: (0, 0)),
                  pl.BlockSpec(keys.shape, lambda i: (0, 0, 0, 0))],
        out_specs=[pl.BlockSpec((D_MODEL, tm), lambda i: (0, i)), heads(), heads(), heads(), heads()],
        out_shape=[jax.ShapeDtypeStruct((D_MODEL, n), BF16)]
        + [jax.ShapeDtypeStruct((PEER_HEADS, N_KEYS, n), F32)] * 4,
        compiler_params=_params("parallel"),
        name="peer_score",
    )(h, nw, wqt, keys)


EXPERT_CHUNK = 1024
SQRT_HALF = math.sqrt(0.5)


def _peer_dense_kernel(xnt_ref, thr_ref, e1_ref, s2_ref, e2_ref, u_ref, vt_ref, h_ref, fw_ref,
                       out_ref, acc_s, w_s, *, final_norm):
    c = pl.program_id(1)

    @pl.when(c == 0)
    def _():
        acc_s[...] = jnp.zeros_like(acc_s)

    xnt = xnt_ref[...]
    for j in range(EXPERT_CHUNK // N_KEYS):
        r0 = j * N_KEYS
        pre = _dot(u_ref[r0:r0 + N_KEYS, :], xnt)
        act = 0.5 * pre * (1.0 + lax.erf(pre * SQRT_HALF))
        gate = jnp.zeros_like(act)
        for hh in range(PEER_HEADS):
            sel = jnp.where(s2_ref[hh] >= thr_ref[hh, j:j + 1, :], e2_ref[hh], 0.0)
            gate = gate + e1_ref[hh, j:j + 1, :] * sel
        w_s[r0:r0 + N_KEYS, :] = (act * gate).astype(BF16)
    acc_s[...] += _dot(vt_ref[...], w_s[...])

    @pl.when(c == pl.num_programs(1) - 1)
    def _():
        y = h_ref[...] + acc_s[...].T
        if final_norm:
            y = _rmsnorm(y, fw_ref[...])
        out_ref[...] = y


def _peer_dense(xnt, thr, e1, s2, e2, u, vt, h, fw, *, final_norm):
    n = h.shape[0]
    tm = _pick_tile(n, (384, 256, 128))
    n_exp = u.shape[0]
    rows_per_chunk = EXPERT_CHUNK // N_KEYS
    by_chunk = lambda: pl.BlockSpec((PEER_HEADS, rows_per_chunk, tm), lambda i, c: (0, c, i))
    by_tile = lambda: pl.BlockSpec((PEER_HEADS, N_KEYS, tm), lambda i, c: (0, 0, i))
    return pl.pallas_call(
        functools.partial(_peer_dense_kernel, final_norm=final_norm),
        grid=(n // tm, n_exp // EXPERT_CHUNK),
        in_specs=[pl.BlockSpec((D_MODEL, tm), lambda i, c: (0, i)),
                  by_chunk(), by_chunk(), by_tile(), by_tile(),
                  pl.BlockSpec((EXPERT_CHUNK, D_MODEL), lambda i, c: (c, 0)),
                  pl.BlockSpec((D_MODEL, EXPERT_CHUNK), lambda i, c: (0, c)),
                  pl.BlockSpec((tm, D_MODEL), lambda i, c: (i, 0)),
                  pl.BlockSpec((1, D_MODEL), lambda i, c: (0, 0))],
        out_specs=pl.BlockSpec((tm, D_MODEL), lambda i, c: (i, 0)),
        out_shape=jax.ShapeDtypeStruct((n, D_MODEL), F32),
        scratch_shapes=[pltpu.VMEM((D_MODEL, tm), F32), pltpu.VMEM((EXPERT_CHUNK, tm), BF16)],
        compiler_params=_params("parallel", "arbitrary"),
        name="peer_dense",
    )(xnt, thr, e1, s2, e2, u, vt, h, fw)


def _prep_layer(layer, norm_mix_w, w_in, conv_qkv_w, a_log, dt_bias, gdn_norm_w, conf_dw_w, conf_dw_b,
                conf_ln_w, conf_ln_b, w_out, norm_ffn_w, w_query, sub_keys, expert_u, expert_v):
    s0 = QKV_WIDTH
    s1 = s0 + GDN_WIDTH
    s2 = s1 + GDN_HEADS
    s3 = s2 + GDN_HEADS
    w = w_in[layer]
    row = lambda v: v.reshape(1, -1)
    return dict(
        norm_mix_w=row(norm_mix_w[layer]),
        wqkvz=w[:, :s1].astype(BF16),
        wab=jnp.repeat(w[:, s1:s3], HEAD_DIM, axis=1).astype(BF16),
        wglu=w[:, s3:].astype(BF16),
        alog=row(jnp.repeat(a_log[layer], HEAD_DIM)),
        dtb=row(jnp.repeat(dt_bias[layer], HEAD_DIM)),
        conv_w=conv_qkv_w[layer],
        gdn_norm_w=row(gdn_norm_w[layer]),
        dw=conf_dw_w[layer], dwb=row(conf_dw_b[layer]),
        lnw=row(conf_ln_w[layer]), lnb=row(conf_ln_b[layer]),
        wout=w_out[layer].astype(BF16),
        norm_ffn_w=row(norm_ffn_w[layer]),
        wqt=w_query[layer].T.astype(BF16),
        keys=sub_keys[layer].astype(BF16),
        u=expert_u[layer].astype(BF16),
        vt=expert_v[layer].T.astype(BF16),
    )


def _peer(h, lw, fw, final_norm):
    xnt, thr, e1, s2, e2 = _peer_score(h, lw["norm_ffn_w"], lw["wqt"], lw["keys"])
    return _peer_dense(xnt, thr, e1, s2, e2, lw["u"], lw["vt"], h, fw, final_norm=final_norm)


def _run_prompt(x_prompt, meta_tokens, layers, fw):
    b, seq0, _ = x_prompt.shape
    seq = seq0 + N_META
    meta = jnp.broadcast_to(meta_tokens[None], (b, N_META, D_MODEL))
    h = jnp.concatenate([meta, x_prompt], axis=1).reshape(b * seq, D_MODEL)
    n_pad = (-seq) % GDN_CHUNK
    s_out, qkv_out, conf_out = [], [], []
    for li, lw in enumerate(layers):
        qkv, z, g, beta, glu = _in_proj(h, lw["norm_mix_w"], lw["wqkvz"], lw["wab"], lw["wglu"],
                                        lw["alog"], lw["dtb"])
        shp = lambda a: a.reshape(b, seq, a.shape[-1])
        qkv, z, g, beta, glu = shp(qkv), shp(z), shp(g), shp(beta), shp(glu)
        o, s_new = _gdn(qkv, g, beta, z,
                        jnp.zeros((b, GDN_CONV - 1, QKV_WIDTH), F32),
                        jnp.zeros((b, GDN_HEADS, HEAD_DIM, HEAD_DIM), F32),
                        lw["conv_w"], lw["gdn_norm_w"], chunk=GDN_CHUNK, n_pad=n_pad, bb_count=1)
        h = _conf_out_seq(glu, o, shp(h), jnp.zeros((b, CONF_CONV - 1, CONF_WIDTH), F32),
                          lw["dw"], lw["dwb"], lw["lnw"], lw["lnb"], lw["wout"]).reshape(b * seq, D_MODEL)
        h = _peer(h, lw, fw, li == len(layers) - 1)
        s_out.append(s_new)
        qkv_out.append(qkv[:, seq - (GDN_CONV - 1):])
        conf_out.append(glu[:, seq - (CONF_CONV - 1):])
    y = h.reshape(b, seq, D_MODEL)[:, N_META:]
    return y, jnp.stack(s_out), jnp.stack(qkv_out), jnp.stack(conf_out)


def _run_sample(x_sample, state_gdn, state_qkv_conv, state_conf_conv, layers, fw):
    b, steps, _ = x_sample.shape
    n_pad = (-steps) % SUBLANES
    to_tb = lambda a: jnp.swapaxes(a, 0, 1)
    h = to_tb(x_sample).reshape(steps * b, D_MODEL)
    s_out, qkv_out, conf_out = [], [], []
    for li, lw in enumerate(layers):
        qkv, z, g, beta, glu = _in_proj(h, lw["norm_mix_w"], lw["wqkvz"], lw["wab"], lw["wglu"],
                                        lw["alog"], lw["dtb"])
        bt = lambda a: to_tb(a.reshape(steps, b, a.shape[-1]))
        qkv_bt = bt(qkv)
        o, s_new = _gdn(qkv_bt, bt(g), bt(beta), bt(z), state_qkv_conv[li], state_gdn[li],
                        lw["conv_w"], lw["gdn_norm_w"], chunk=steps + n_pad, n_pad=n_pad, bb_count=8)
        cext = jnp.concatenate([to_tb(state_conf_conv[li]), glu.reshape(steps, b, CONF_WIDTH)], axis=0)
        h = _conf_out_step(cext, to_tb(o).reshape(steps * b, GDN_WIDTH), h,
                           lw["dw"], lw["dwb"], lw["lnw"], lw["lnb"], lw["wout"])
        h = _peer(h, lw, fw, li == len(layers) - 1)
        s_out.append(s_new)
        qkv_ext = jnp.concatenate([state_qkv_conv[li], qkv_bt], axis=1)
        qkv_out.append(qkv_ext[:, qkv_ext.shape[1] - (GDN_CONV - 1):])
        conf_out.append(to_tb(cext[cext.shape[0] - (CONF_CONV - 1):]))
    y = to_tb(h.reshape(steps, b, D_MODEL))
    return y, jnp.stack(s_out), jnp.stack(qkv_out), jnp.stack(conf_out)


def kernel(x_prompt, x_sample, state_gdn, state_qkv_conv, state_conf_conv, meta_tokens, norm_mix_w, w_in,
           conv_qkv_w, a_log, dt_bias, gdn_norm_w, conf_dw_w, conf_dw_b, conf_ln_w, conf_ln_b, w_out,
           norm_ffn_w, w_query, sub_keys, expert_u, expert_v, final_norm_w):
    depth = w_in.shape[0]
    layers = [_prep_layer(l, norm_mix_w, w_in, conv_qkv_w, a_log, dt_bias, gdn_norm_w, conf_dw_w,
                          conf_dw_b, conf_ln_w, conf_ln_b, w_out, norm_ffn_w, w_query, sub_keys,
                          expert_u, expert_v) for l in range(depth)]
    fw = final_norm_w.reshape(1, D_MODEL)
    y_p, s_p, q_p, c_p = _run_prompt(x_prompt, meta_tokens, layers, fw)
    y_s, s_s, q_s, c_s = _run_sample(x_sample, state_gdn, state_qkv_conv, state_conf_conv, layers, fw)
    return (y_p, y_s, s_p, q_p, c_p, s_s, q_s, c_s)
```

```python
import functools
import math

import jax
import jax.numpy as jnp
from jax import lax
from jax.experimental import pallas as pl
from jax.experimental.pallas import tpu as pltpu

F32 = jnp.float32
BF16 = jnp.bfloat16
HIGHEST = lax.Precision.HIGHEST

D_MODEL = 1024
N_META = 16
GDN_HEADS = 4
HEAD_DIM = 128
GDN_WIDTH = GDN_HEADS * HEAD_DIM
CONF_WIDTH = 512
QKV_WIDTH = 3 * GDN_WIDTH
GDN_CONV = 4
GDN_CHUNK = 64
GDN_GROUP = 4
CONF_CONV = 31
N_KEYS = 128
PEER_HEADS = 8
PEER_TOPK = 16
EPS = 1e-6

LANES = 128
SUBLANES = 8
VMEM_LIMIT_BYTES = 56 * 1024 * 1024


def _sigmoid(x):
    return 1.0 / (1.0 + jnp.exp(-x))


def _silu(x):
    return x * _sigmoid(x)


def _softplus(x):
    return jnp.maximum(x, 0.0) + jnp.log1p(jnp.exp(-jnp.abs(x)))


def _rmsnorm(x, w):
    return x * lax.rsqrt(jnp.mean(x * x, axis=-1, keepdims=True) + EPS) * w


def _dot(a, b, precision=None):
    return jnp.dot(a, b, preferred_element_type=F32, precision=precision)


def _dot_nt(a, b, precision=None):
    return lax.dot_general(a, b, (((1,), (1,)), ((), ())), preferred_element_type=F32,
                           precision=precision)


def _dot_tn(a, b, precision=None):
    return lax.dot_general(a, b, (((0,), (0,)), ((), ())), preferred_element_type=F32,
                           precision=precision)


def _pick_tile(n, candidates):
    for c in candidates:
        if n % c == 0:
            return c
    return n


def _params(*semantics):
    return pltpu.CompilerParams(dimension_semantics=semantics, vmem_limit_bytes=VMEM_LIMIT_BYTES)


def _in_proj_kernel(h_ref, nw_ref, wqkvz_ref, wab_ref, wglu_ref, alog_ref, dtb_ref,
                    qkv_ref, z_ref, g_ref, beta_ref, glu_ref):
    xb = _rmsnorm(h_ref[...], nw_ref[...]).astype(BF16)
    p = _dot(xb, wqkvz_ref[...])
    qkv_ref[...] = p[:, :QKV_WIDTH]
    z_ref[...] = p[:, QKV_WIDTH:]
    ab = _dot(xb, wab_ref[...])
    a = ab[:, :GDN_WIDTH]
    b = ab[:, GDN_WIDTH:]
    g_ref[...] = -jnp.exp(alog_ref[...]) * _softplus(a + dtb_ref[...])
    beta_ref[...] = _sigmoid(b)
    pg = _dot(xb, wglu_ref[...])
    glu_ref[...] = pg[:, :CONF_WIDTH] * _sigmoid(pg[:, CONF_WIDTH:])


def _in_proj(h, nw, wqkvz, wab, wglu, alog, dtb):
    n = h.shape[0]
    tm = _pick_tile(n, (384, 256, 128))
    row = lambda i: (i, 0)
    fixed = lambda i: (0, 0)
    widths = (QKV_WIDTH, GDN_WIDTH, GDN_WIDTH, GDN_WIDTH, CONF_WIDTH)
    return pl.pallas_call(
        _in_proj_kernel,
        grid=(n // tm,),
        in_specs=[
            pl.BlockSpec((tm, D_MODEL), row),
            pl.BlockSpec((1, D_MODEL), fixed),
            pl.BlockSpec(wqkvz.shape, fixed),
            pl.BlockSpec(wab.shape, fixed),
            pl.BlockSpec(wglu.shape, fixed),
            pl.BlockSpec((1, GDN_WIDTH), fixed),
            pl.BlockSpec((1, GDN_WIDTH), fixed),
        ],
        out_specs=[pl.BlockSpec((tm, w), row) for w in widths],
        out_shape=[jax.ShapeDtypeStruct((n, w), F32) for w in widths],
        compiler_params=_params("parallel"),
        name="in_proj",
    )(h, nw, wqkvz, wab, wglu, alog, dtb)


def _gdn_kernel(xq_ref, xk_ref, xv_ref, g_ref, beta_ref, z_ref, cs_q_ref, cs_k_ref, cs_v_ref, s0_ref,
                cwq_ref, cwk_ref, cwv_ref, nw_ref,
                o_ref, s_ref,
                xq_s, xk_s, xv_s, g_s, b_s, u_s, att_s, wq_s, m_s, bm_s, gl_s, sall_s,
                *, chunk, n_pad, seq, bb_count):
    C = chunk
    n_chunks = (seq + n_pad) // C
    head = 8 + n_pad
    n_fac = int(math.log2(C))
    row_i = lax.broadcasted_iota(jnp.int32, (C, C), 0)
    col_i = lax.broadcasted_iota(jnp.int32, (C, C), 1)
    incl = row_i >= col_i
    strict = row_i > col_i
    tril = jnp.where(incl, 1.0, 0.0).astype(F32)
    eye = jnp.where(row_i == col_i, 1.0, 0.0).astype(F32)
    group = GDN_GROUP if (n_chunks - 1) % GDN_GROUP == 0 else 1
    live = lax.broadcasted_iota(jnp.int32, (C, HEAD_DIM), 0) >= n_pad
    cws = (cwq_ref[...], cwk_ref[...], cwv_ref[...])
    nw = nw_ref[...]

    def conv(blk, cw):
        acc = blk[5:5 + C] * cw[0:1]
        for j in range(1, GDN_CONV):
            acc = acc + blk[5 + j:5 + j + C] * cw[j:j + 1]
        return _silu(acc)

    slot = lambda bb, c: bb * n_chunks + c

    def build(items, first):
        each = lambda f, *ls: [f(*a) for a in zip(*ls)]
        at = [(bb, c * C if first else pl.multiple_of(c * C, SUBLANES)) for bb, c in items]
        qc = [conv(xq_s[bb, pl.ds(t0, C + 8), :], cws[0]) for bb, t0 in at]
        kc = [conv(xk_s[bb, pl.ds(t0, C + 8), :], cws[1]) for bb, t0 in at]
        v = [conv(xv_s[bb, pl.ds(t0, C + 8), :], cws[2]) for bb, t0 in at]
        q = each(lambda a: a * lax.rsqrt(jnp.sum(a * a, axis=-1, keepdims=True) + EPS) * (HEAD_DIM ** -0.5), qc)
        k = each(lambda a: a * lax.rsqrt(jnp.sum(a * a, axis=-1, keepdims=True) + EPS), kc)
        g = [g_s[bb, pl.ds(t0, C), :] for bb, t0 in at]
        beta = [b_s[bb, pl.ds(t0, C), :] for bb, t0 in at]
        if first and n_pad:
            q, k, v = ([jnp.where(live, a, 0.0) for a in l] for l in (q, k, v))
        cum = each(lambda a: _dot(tril, jnp.concatenate([a, jnp.where(strict, a[:, :C], 0.0)], axis=1), HIGHEST), g)
        gcum = [a[:, :HEAD_DIM] for a in cum]
        decay = [jnp.where(incl, jnp.exp(a[:, HEAD_DIM:]), 0.0) for a in cum]
        eg = [jnp.exp(a) for a in gcum]
        kb = each(lambda a, b_: a * b_, k, beta)
        kq = each(lambda a, b_, c_: _dot_nt(jnp.concatenate([a, b_], axis=0).astype(BF16), c_.astype(BF16)), kb, q, k)
        p = each(lambda a, d: jnp.where(strict, -(a[:C] * d), 0.0), kq, decay)
        att = each(lambda a, d: jnp.where(incl, a[C:] * d, 0.0), kq, decay)
        t = [eye + a for a in p]
        p = each(lambda a: _dot(a, a, HIGHEST), p)
        for step in range(1, n_fac):
            if step == n_fac - 1:
                t = each(lambda a, b_: a + _dot(b_, a, HIGHEST), t, p)
            else:
                y = each(lambda a, b_: _dot(b_, jnp.concatenate([a, b_], axis=1), HIGHEST), t, p)
                t = each(lambda a, b_: a + b_[:, :C], t, y)
                p = [a[:, C:] for a in y]
        x = each(lambda t_, v_, b_, kb_, eg_: _dot(t_, jnp.concatenate([v_ * b_, kb_ * eg_], axis=1), HIGHEST),
                 t, v, beta, kb, eg)
        g_last = [a[C - 1:C, :] for a in gcum]
        kt = each(lambda k_, gl, gc: k_ * jnp.exp(gl - gc), k, g_last, gcum)
        ktx = each(lambda a, b_: _dot_tn(a.astype(BF16), b_.astype(BF16)), kt, x)
        for i, (bb, c) in enumerate(items):
            n = slot(bb, c)
            u_s[n] = x[i][:, :HEAD_DIM]
            att_s[n] = att[i].astype(BF16)
            wq_s[n] = jnp.concatenate([x[i][:, HEAD_DIM:], q[i] * eg[i]], axis=0).astype(BF16)
            bm_s[n] = ktx[i][:, :HEAD_DIM]
            m_s[n] = (-ktx[i][:, HEAD_DIM:]).astype(BF16)
            gl_s[n] = jnp.exp(g_last[i])

    def carry(items):
        s = [s_ref[bb, 0] for bb, _ in items]
        for (bb, c), s_i in zip(items, s):
            sall_s[slot(bb, c)] = s_i
        new = [gl_s[slot(bb, c)] * s_i + _dot(m_s[slot(bb, c)], s_i.astype(BF16)) + bm_s[slot(bb, c)]
               for (bb, c), s_i in zip(items, s)]
        for (bb, _), s_i in zip(items, new):
            s_ref[bb, 0] = s_i

    def emit(items, first):
        ns = [slot(bb, c) for bb, c in items]
        ws_qs = [_dot(wq_s[n], sall_s[n].astype(BF16)) for n in ns]
        v_new = [u_s[n] - a[:C] for n, a in zip(ns, ws_qs)]
        o = [a[C:] + _dot(att_s[n], vn.astype(BF16)) for n, a, vn in zip(ns, ws_qs, v_new)]
        o = [_rmsnorm(a, nw) for a in o]
        for (bb, c), a in zip(items, o):
            if first:
                rows = C - n_pad
                o_ref[bb, 0:rows, :] = a[n_pad:] * _silu(z_ref[bb, 0:rows, :])
            else:
                r0 = pl.multiple_of(c * C - n_pad, SUBLANES)
                o_ref[bb, pl.ds(r0, C), :] = a * _silu(z_ref[bb, pl.ds(r0, C), :])

    def grouped(fn):
        def body(i, _):
            fn([(0, 1 + i * group + j) for j in range(group)])
            return 0
        lax.fori_loop(0, (n_chunks - 1) // group, body, 0)

    for bb in range(bb_count):
        for x_s, x_ref, cs_ref in ((xq_s, xq_ref, cs_q_ref), (xk_s, xk_ref, cs_k_ref),
                                   (xv_s, xv_ref, cs_v_ref)):
            x_s[bb, 0:head, :] = jnp.zeros((head, HEAD_DIM), F32)
            x_s[bb, head - (GDN_CONV - 1):head, :] = cs_ref[bb]
            x_s[bb, head:head + seq, :] = x_ref[bb]
        if n_pad:
            g_s[bb, 0:n_pad, :] = jnp.zeros((n_pad, HEAD_DIM), F32)
            b_s[bb, 0:n_pad, :] = jnp.zeros((n_pad, HEAD_DIM), F32)
        g_s[bb, n_pad:n_pad + seq, :] = g_ref[bb]
        b_s[bb, n_pad:n_pad + seq, :] = beta_ref[bb]
        s_ref[bb, 0] = s0_ref[bb, 0]

    firsts = [(bb, 0) for bb in range(bb_count)]
    build(firsts, True)
    if n_chunks > 1:
        grouped(lambda items: build(items, False))
    carry(firsts)
    if n_chunks > 1:
        lax.fori_loop(1, n_chunks, lambda c, _: (carry([(0, c)]), 0)[1], 0)
    emit(firsts, True)
    if n_chunks > 1:
        grouped(lambda items: emit(items, False))


def _gdn(qkv, g, beta, z, conv_state, s0, conv_w, norm_w, *, chunk, n_pad, bb_count):
    b, seq, _ = qkv.shape
    n_chunks = (seq + n_pad) // chunk
    assert n_chunks * chunk == seq + n_pad and b % bb_count == 0
    assert n_chunks == 1 or bb_count == 1
    col = lambda off: (lambda i, h: (i, 0, off + h))
    wcol = lambda off: (lambda i, h: (0, off + h))
    seq_blk = (bb_count, seq, HEAD_DIM)
    cs_blk = (bb_count, GDN_CONV - 1, HEAD_DIM)
    s_blk = (bb_count, 1, HEAD_DIM, HEAD_DIM)
    s_map = lambda i, h: (i, h, 0, 0)
    ext = chunk * n_chunks
    per_chunk = lambda r, c, dt=F32: pltpu.VMEM((bb_count * n_chunks, r, c), dt)
    kern = functools.partial(_gdn_kernel, chunk=chunk, n_pad=n_pad, seq=seq, bb_count=bb_count)
    return pl.pallas_call(
        kern,
        grid=(b // bb_count, GDN_HEADS),
        in_specs=[
            pl.BlockSpec(seq_blk, col(0)),
            pl.BlockSpec(seq_blk, col(GDN_HEADS)),
            pl.BlockSpec(seq_blk, col(2 * GDN_HEADS)),
            pl.BlockSpec(seq_blk, col(0)),
            pl.BlockSpec(seq_blk, col(0)),
            pl.BlockSpec(seq_blk, col(0)),
            pl.BlockSpec(cs_blk, col(0)),
            pl.BlockSpec(cs_blk, col(GDN_HEADS)),
            pl.BlockSpec(cs_blk, col(2 * GDN_HEADS)),
            pl.BlockSpec(s_blk, s_map),
            pl.BlockSpec((GDN_CONV, HEAD_DIM), wcol(0)),
            pl.BlockSpec((GDN_CONV, HEAD_DIM), wcol(GDN_HEADS)),
            pl.BlockSpec((GDN_CONV, HEAD_DIM), wcol(2 * GDN_HEADS)),
            pl.BlockSpec((1, HEAD_DIM), lambda i, h: (0, 0)),
        ],
        out_specs=[pl.BlockSpec(seq_blk, col(0)), pl.BlockSpec(s_blk, s_map)],
        out_shape=[jax.ShapeDtypeStruct((b, seq, GDN_WIDTH), F32),
                   jax.ShapeDtypeStruct((b, GDN_HEADS, HEAD_DIM, HEAD_DIM), F32)],
        scratch_shapes=[
            pltpu.VMEM((bb_count, ext + 8, HEAD_DIM), F32),
            pltpu.VMEM((bb_count, ext + 8, HEAD_DIM), F32),
            pltpu.VMEM((bb_count, ext + 8, HEAD_DIM), F32),
            pltpu.VMEM((bb_count, ext, HEAD_DIM), F32),
            pltpu.VMEM((bb_count, ext, HEAD_DIM), F32),
            per_chunk(chunk, HEAD_DIM),
            per_chunk(chunk, chunk, BF16),
            per_chunk(2 * chunk, HEAD_DIM, BF16),
            per_chunk(HEAD_DIM, HEAD_DIM, BF16),
            per_chunk(HEAD_DIM, HEAD_DIM),
            per_chunk(1, HEAD_DIM),
            per_chunk(HEAD_DIM, HEAD_DIM),
        ],
        compiler_params=_params("parallel", "parallel"),
        name="gdn",
    )(qkv, qkv, qkv, g, beta, z, conv_state, conv_state, conv_state, s0,
      conv_w, conv_w, conv_w, norm_w)


CONF_HALO = 32
CONF_ROWS = 16


def _conf_out_seq_kernel(glu_ref, o_ref, h_ref, cstate_ref, dw_ref, dwb_ref, lnw_ref, lnb_ref, wout_ref,
                         out_ref, ext_s, c_s, *, tt):
    t = pl.program_id(1)
    lead = CONF_HALO - (CONF_CONV - 1)

    @pl.when(t == 0)
    def _():
        ext_s[0:lead, :] = jnp.zeros((lead, CONF_WIDTH), F32)
        ext_s[lead:CONF_HALO, :] = cstate_ref[0]

    @pl.when(t > 0)
    def _():
        ext_s[0:CONF_HALO, :] = ext_s[tt:tt + CONF_HALO, :]

    ext_s[CONF_HALO:CONF_HALO + tt, :] = glu_ref[0]
    dw = dw_ref[...]
    bias = dwb_ref[...]

    def rows(i, _):
        r0 = pl.multiple_of(i * CONF_ROWS, SUBLANES)
        blk = ext_s[pl.ds(r0, CONF_ROWS + CONF_HALO), :]
        acc = bias + blk[lead:lead + CONF_ROWS] * dw[0:1]
        for j in range(1, CONF_CONV):
            acc = acc + blk[lead + j:lead + j + CONF_ROWS] * dw[j:j + 1]
        c_s[pl.ds(r0, CONF_ROWS), :] = acc
        return 0

    lax.fori_loop(0, tt // CONF_ROWS, rows, 0)
    c = c_s[...]
    mu = jnp.mean(c, axis=-1, keepdims=True)
    xc = c - mu
    var = jnp.mean(xc * xc, axis=-1, keepdims=True)
    c = _silu(xc * lax.rsqrt(var + EPS) * lnw_ref[...] + lnb_ref[...])
    mixed = jnp.concatenate([o_ref[0], c], axis=1).astype(BF16)
    out_ref[0] = h_ref[0] + _dot(mixed, wout_ref[...])


def _conf_out_seq(glu, o, h, cstate, dw, dwb, lnw, lnb, wout):
    b, seq, _ = glu.shape
    tt = _pick_tile(seq, (688, 512, 256, 128, 64, 16))
    tile = lambda w: pl.BlockSpec((1, tt, w), lambda i, t: (i, t, 0))
    fixed = lambda a: pl.BlockSpec(a.shape, lambda i, t: (0,) * a.ndim)
    return pl.pallas_call(
        functools.partial(_conf_out_seq_kernel, tt=tt),
        grid=(b, seq // tt),
        in_specs=[tile(CONF_WIDTH), tile(GDN_WIDTH), tile(D_MODEL),
                  pl.BlockSpec((1, CONF_CONV - 1, CONF_WIDTH), lambda i, t: (i, 0, 0)),
                  fixed(dw), fixed(dwb), fixed(lnw), fixed(lnb), fixed(wout)],
        out_specs=tile(D_MODEL),
        out_shape=jax.ShapeDtypeStruct((b, seq, D_MODEL), F32),
        scratch_shapes=[pltpu.VMEM((tt + CONF_HALO, CONF_WIDTH), F32),
                        pltpu.VMEM((tt, CONF_WIDTH), F32)],
        compiler_params=_params("parallel", "arbitrary"),
        name="conf_out_seq",
    )(glu, o, h, cstate, dw, dwb, lnw, lnb, wout)


def _conf_out_step_kernel(cext_ref, o_ref, h_ref, dw_ref, dwb_ref, lnw_ref, lnb_ref, wout_ref, out_ref,
                          *, steps):
    dw = dw_ref[...]
    outs = []
    for t in range(steps):
        acc = dwb_ref[...] + cext_ref[t] * dw[0:1]
        for j in range(1, CONF_CONV):
            acc = acc + cext_ref[t + j] * dw[j:j + 1]
        outs.append(acc)
    c = jnp.concatenate(outs, axis=0)
    mu = jnp.mean(c, axis=-1, keepdims=True)
    xc = c - mu
    var = jnp.mean(xc * xc, axis=-1, keepdims=True)
    c = _silu(xc * lax.rsqrt(var + EPS) * lnw_ref[...] + lnb_ref[...])
    mixed = jnp.concatenate([o_ref[...], c], axis=1).astype(BF16)
    out_ref[...] = h_ref[...] + _dot(mixed, wout_ref[...])


def _conf_out_step(cext, o, h, dw, dwb, lnw, lnb, wout):
    steps = cext.shape[0] - (CONF_CONV - 1)
    n = h.shape[0]
    full = lambda a: pl.BlockSpec(a.shape, lambda i: (0,) * a.ndim)
    args = (cext, o, h, dw, dwb, lnw, lnb, wout)
    return pl.pallas_call(
        functools.partial(_conf_out_step_kernel, steps=steps),
        grid=(1,),
        in_specs=[full(a) for a in args],
        out_specs=pl.BlockSpec((n, D_MODEL), lambda i: (0, 0)),
        out_shape=jax.ShapeDtypeStruct((n, D_MODEL), F32),
        compiler_params=_params("arbitrary"),
        name="conf_out_step",
    )(*args)


N_RANKS = PEER_TOPK + 1
RANK_ROWS = 24
NEG_INF = float("-inf")


def _top_rows(s, count, with_rank=False):
    rows = []
    rank = jnp.full(s.shape, float(count), F32) if with_rank else None
    for r in range(count):
        m = jnp.max(s, axis=0, keepdims=True)
        rows.append(m)
        hit = s == m
        if with_rank:
            rank = jnp.where(hit, float(r), rank)
        s = jnp.where(hit, NEG_INF, s)
    return (rows, rank) if with_rank else rows


def _stack_rows(rows, height, tm):
    idx = lax.broadcasted_iota(jnp.int32, (height, tm), 0)
    out = jnp.full((height, tm), NEG_INF, F32)
    for r, row in enumerate(rows):
        out = jnp.where(idx == r, row, out)
    return out


def _peer_score_kernel(h_ref, nw_ref, wqt_ref, keys_ref, xnt_ref, cnt_ref, e1_ref, rank2_ref, e2_ref):
    tm = h_ref.shape[0]
    xn = _rmsnorm(h_ref[...], nw_ref[...])
    xnt = xn.T.astype(BF16)
    xnt_ref[...] = xnt
    qt = _dot(wqt_ref[...], xnt).astype(BF16)
    idx8 = lax.broadcasted_iota(jnp.int32, (SUBLANES, tm), 0)
    for hh in range(PEER_HEADS):
        s = []
        for p in range(2):
            r0 = (hh * 2 + p) * N_KEYS
            s.append(_dot(keys_ref[hh, p], qt[r0:r0 + N_KEYS, :]))
        a, rank1 = _top_rows(s[0], N_RANKS, with_rank=True)
        b, rank2 = _top_rows(s[1], N_RANKS, with_rank=True)
        a_st = _stack_rows(a, RANK_ROWS, tm)
        b_st = _stack_rows(b, RANK_ROWS, tm)
        cands = [a[0] + b_st,
                 jnp.where(lax.broadcasted_iota(jnp.int32, (RANK_ROWS, tm), 0) >= 1,
                           a_st + b[0], NEG_INF)]
        for r1 in range(1, N_RANKS):
            hi = N_RANKS // (r1 + 1) - 1
            if hi >= 1 and r1 <= 4:
                cands.append(jnp.where((idx8 >= 1) & (idx8 <= hi), a[r1] + b_st[0:SUBLANES], NEG_INF))
        cands.append(jnp.where((idx8 >= 5) & (idx8 <= N_RANKS // 2 - 1),
                               a_st[0:SUBLANES] + b[1], NEG_INF))
        cand = jnp.concatenate(cands, axis=0)
        top = _top_rows(cand, N_RANKS)
        z = jnp.zeros_like(top[0])
        for r in range(PEER_TOPK):
            z = z + jnp.exp(top[r] - top[0])
        thr = 0.5 * (top[PEER_TOPK - 1] + top[PEER_TOPK])
        cnt1 = jnp.zeros_like(rank1)
        for r1 in range(PEER_TOPK):
            hi = min(N_RANKS // (r1 + 1), PEER_TOPK)
            cnt = jnp.zeros_like(thr)
            for r2 in range(hi):
                cnt = cnt + jnp.where(a[r1] + b[r2] >= thr, 1.0, 0.0)
            cnt1 = jnp.where(rank1 == float(r1), cnt, cnt1)
        cnt_ref[hh] = cnt1
        e1_ref[hh] = jnp.exp(s[0] - a[0]) / z
        rank2_ref[hh] = rank2.astype(BF16)
        e2_ref[hh] = jnp.exp(s[1] - b[0]).astype(BF16)


def _peer_score(h, nw, wqt, keys):
    n = h.shape[0]
    tm = _pick_tile(n, (384, 256, 128))
    heads = lambda: pl.BlockSpec((PEER_HEADS, N_KEYS, tm), lambda i: (0, 0, i))
    return pl.pallas_call(
        _peer_score_kernel,
        grid=(n // tm,),
        in_specs=[pl.BlockSpec((tm, D_MODEL), lambda i: (i, 0)),
                  pl.BlockSpec((1, D_MODEL), lambda i: (0, 0)),
                  pl.BlockSpec(wqt.shape, lambda i: (0, 0)),
                  pl.BlockSpec(keys.shape, lambda i: (0, 0, 0, 0))],
        out_specs=[pl.BlockSpec((D_MODEL, tm), lambda i: (0, i)), heads(), heads(), heads(), heads()],
        out_shape=[jax.ShapeDtypeStruct((D_MODEL, n), BF16)]
        + [jax.ShapeDtypeStruct((PEER_HEADS, N_KEYS, n), dt) for dt in (F32, F32, BF16, BF16)],
        compiler_params=_params("parallel"),
        name="peer_score",
    )(h, nw, wqt, keys)


EXPERT_CHUNK = 1024
SQRT_HALF = math.sqrt(0.5)


def _peer_dense_kernel(xnt_ref, cnt_ref, e1_ref, rank2_ref, e2_ref, u_ref, vt_ref, h_ref, fw_ref,
                       out_ref, acc_s, w_s, *, final_norm):
    c = pl.program_id(1)

    @pl.when(c == 0)
    def _():
        acc_s[...] = jnp.zeros_like(acc_s)

    xnt = xnt_ref[...]
    for j in range(EXPERT_CHUNK // N_KEYS):
        r0 = j * N_KEYS
        pre = _dot(u_ref[r0:r0 + N_KEYS, :], xnt)
        act = (0.5 * pre * (1.0 + lax.erf(pre * SQRT_HALF))).astype(BF16)
        terms = []
        for hh in range(PEER_HEADS):
            e2 = e2_ref[hh]
            sel = jnp.where(rank2_ref[hh] < cnt_ref[hh, j:j + 1, :].astype(BF16), e2, jnp.zeros_like(e2))
            terms.append(e1_ref[hh, j:j + 1, :].astype(BF16) * sel)
        while len(terms) > 1:
            terms = [terms[i] + terms[i + 1] for i in range(0, len(terms), 2)]
        w_s[r0:r0 + N_KEYS, :] = act * terms[0]
    acc_s[...] += _dot(vt_ref[0], w_s[...])

    @pl.when(c == pl.num_programs(1) - 1)
    def _():
        y = h_ref[...] + acc_s[...].T
        if final_norm:
            y = _rmsnorm(y, fw_ref[...])
        out_ref[...] = y


def _peer_dense(xnt, cnt, e1, rank2, e2, u, vt, h, fw, *, final_norm):
    n = h.shape[0]
    tm = _pick_tile(n, (384, 256, 128))
    n_exp = u.shape[0]
    rows_per_chunk = EXPERT_CHUNK // N_KEYS
    by_chunk = lambda: pl.BlockSpec((PEER_HEADS, rows_per_chunk, tm), lambda i, c: (0, c, i))
    by_tile = lambda: pl.BlockSpec((PEER_HEADS, N_KEYS, tm), lambda i, c: (0, 0, i))
    return pl.pallas_call(
        functools.partial(_peer_dense_kernel, final_norm=final_norm),
        grid=(n // tm, n_exp // EXPERT_CHUNK),
        in_specs=[pl.BlockSpec((D_MODEL, tm), lambda i, c: (0, i)),
                  by_chunk(), by_chunk(), by_tile(), by_tile(),
                  pl.BlockSpec((EXPERT_CHUNK, D_MODEL), lambda i, c: (c, 0)),
                  pl.BlockSpec((1, D_MODEL, EXPERT_CHUNK), lambda i, c: (c, 0, 0)),
                  pl.BlockSpec((tm, D_MODEL), lambda i, c: (i, 0)),
                  pl.BlockSpec((1, D_MODEL), lambda i, c: (0, 0))],
        out_specs=pl.BlockSpec((tm, D_MODEL), lambda i, c: (i, 0)),
        out_shape=jax.ShapeDtypeStruct((n, D_MODEL), F32),
        scratch_shapes=[pltpu.VMEM((D_MODEL, tm), F32), pltpu.VMEM((EXPERT_CHUNK, tm), BF16)],
        compiler_params=_params("parallel", "arbitrary"),
        name="peer_dense",
    )(xnt, cnt, e1, rank2, e2, u, vt, h, fw)


def _prep_layer(layer, norm_mix_w, w_in, conv_qkv_w, a_log, dt_bias, gdn_norm_w, conf_dw_w, conf_dw_b,
                conf_ln_w, conf_ln_b, w_out, norm_ffn_w, w_query, sub_keys, expert_u, expert_v):
    s0 = QKV_WIDTH
    s1 = s0 + GDN_WIDTH
    s2 = s1 + GDN_HEADS
    s3 = s2 + GDN_HEADS
    w = w_in[layer]
    row = lambda v: v.reshape(1, -1)
    return dict(
        norm_mix_w=row(norm_mix_w[layer]),
        wqkvz=w[:, :s1].astype(BF16),
        wab=jnp.repeat(w[:, s1:s3], HEAD_DIM, axis=1).astype(BF16),
        wglu=w[:, s3:].astype(BF16),
        alog=row(jnp.repeat(a_log[layer], HEAD_DIM)),
        dtb=row(jnp.repeat(dt_bias[layer], HEAD_DIM)),
        conv_w=conv_qkv_w[layer],
        gdn_norm_w=row(gdn_norm_w[layer]),
        dw=conf_dw_w[layer], dwb=row(conf_dw_b[layer]),
        lnw=row(conf_ln_w[layer]), lnb=row(conf_ln_b[layer]),
        wout=w_out[layer].astype(BF16),
        norm_ffn_w=row(norm_ffn_w[layer]),
        wqt=w_query[layer].T.astype(BF16),
        keys=sub_keys[layer].astype(BF16),
        u=expert_u[layer].astype(BF16),
        vt=jnp.swapaxes(expert_v[layer].astype(BF16).reshape(-1, EXPERT_CHUNK, D_MODEL), 1, 2),
    )


def _peer(h, lw, fw, final_norm):
    xnt, cnt, e1, rank2, e2 = _peer_score(h, lw["norm_ffn_w"], lw["wqt"], lw["keys"])
    return _peer_dense(xnt, cnt, e1, rank2, e2, lw["u"], lw["vt"], h, fw, final_norm=final_norm)


def _run_prompt(x_prompt, meta_tokens, layers, fw):
    b, seq0, _ = x_prompt.shape
    seq = seq0 + N_META
    meta = jnp.broadcast_to(meta_tokens[None], (b, N_META, D_MODEL))
    h = jnp.concatenate([meta, x_prompt], axis=1).reshape(b * seq, D_MODEL)
    n_pad = (-seq) % GDN_CHUNK
    s_out, qkv_out, conf_out = [], [], []
    for li, lw in enumerate(layers):
        qkv, z, g, beta, glu = _in_proj(h, lw["norm_mix_w"], lw["wqkvz"], lw["wab"], lw["wglu"],
                                        lw["alog"], lw["dtb"])
        shp = lambda a: a.reshape(b, seq, a.shape[-1])
        qkv, z, g, beta, glu = shp(qkv), shp(z), shp(g), shp(beta), shp(glu)
        o, s_new = _gdn(qkv, g, beta, z,
                        jnp.zeros((b, GDN_CONV - 1, QKV_WIDTH), F32),
                        jnp.zeros((b, GDN_HEADS, HEAD_DIM, HEAD_DIM), F32),
                        lw["conv_w"], lw["gdn_norm_w"], chunk=GDN_CHUNK, n_pad=n_pad, bb_count=1)
        h = _conf_out_seq(glu, o, shp(h), jnp.zeros((b, CONF_CONV - 1, CONF_WIDTH), F32),
                          lw["dw"], lw["dwb"], lw["lnw"], lw["lnb"], lw["wout"]).reshape(b * seq, D_MODEL)
        h = _peer(h, lw, fw, li == len(layers) - 1)
        s_out.append(s_new)
        qkv_out.append(qkv[:, seq - (GDN_CONV - 1):])
        conf_out.append(glu[:, seq - (CONF_CONV - 1):])
    y = h.reshape(b, seq, D_MODEL)[:, N_META:]
    return y, jnp.stack(s_out), jnp.stack(qkv_out), jnp.stack(conf_out)


def _run_sample(x_sample, state_gdn, state_qkv_conv, state_conf_conv, layers, fw):
    b, steps, _ = x_sample.shape
    n_pad = (-steps) % SUBLANES
    to_tb = lambda a: jnp.swapaxes(a, 0, 1)
    h = to_tb(x_sample).reshape(steps * b, D_MODEL)
    s_out, qkv_out, conf_out = [], [], []
    for li, lw in enumerate(layers):
        qkv, z, g, beta, glu = _in_proj(h, lw["norm_mix_w"], lw["wqkvz"], lw["wab"], lw["wglu"],
                                        lw["alog"], lw["dtb"])
        bt = lambda a: to_tb(a.reshape(steps, b, a.shape[-1]))
        qkv_bt = bt(qkv)
        o, s_new = _gdn(qkv_bt, bt(g), bt(beta), bt(z), state_qkv_conv[li], state_gdn[li],
                        lw["conv_w"], lw["gdn_norm_w"], chunk=steps + n_pad, n_pad=n_pad, bb_count=8)
        cext = jnp.concatenate([to_tb(state_conf_conv[li]), glu.reshape(steps, b, CONF_WIDTH)], axis=0)
        h = _conf_out_step(cext, to_tb(o).reshape(steps * b, GDN_WIDTH), h,
                           lw["dw"], lw["dwb"], lw["lnw"], lw["lnb"], lw["wout"])
        h = _peer(h, lw, fw, li == len(layers) - 1)
        s_out.append(s_new)
        qkv_ext = jnp.concatenate([state_qkv_conv[li], qkv_bt], axis=1)
        qkv_out.append(qkv_ext[:, qkv_ext.shape[1] - (GDN_CONV - 1):])
        conf_out.append(to_tb(cext[cext.shape[0] - (CONF_CONV - 1):]))
    y = to_tb(h.reshape(steps, b, D_MODEL))
    return y, jnp.stack(s_out), jnp.stack(qkv_out), jnp.stack(conf_out)


def kernel(x_prompt, x_sample, state_gdn, state_qkv_conv, state_conf_conv, meta_tokens, norm_mix_w, w_in,
           conv_qkv_w, a_log, dt_bias, gdn_norm_w, conf_dw_w, conf_dw_b, conf_ln_w, conf_ln_b, w_out,
           norm_ffn_w, w_query, sub_keys, expert_u, expert_v, final_norm_w):
    depth = w_in.shape[0]
    layers = [_prep_layer(l, norm_mix_w, w_in, conv_qkv_w, a_log, dt_bias, gdn_norm_w, conf_dw_w,
                          conf_dw_b, conf_ln_w, conf_ln_b, w_out, norm_ffn_w, w_query, sub_keys,
                          expert_u, expert_v) for l in range(depth)]
    fw = final_norm_w.reshape(1, D_MODEL)
    y_p, s_p, q_p, c_p = _run_prompt(x_prompt, meta_tokens, layers, fw)
    y_s, s_s, q_s, c_s = _run_sample(x_sample, state_gdn, state_qkv_conv, state_conf_conv, layers, fw)
    return (y_p, y_s, s_p, q_p, c_p, s_s, q_s, c_s)
```

```python
import functools
import math

import jax
import jax.numpy as jnp
from jax import lax
from jax.experimental import pallas as pl
from jax.experimental.pallas import tpu as pltpu

F32 = jnp.float32
BF16 = jnp.bfloat16
HIGHEST = lax.Precision.HIGHEST

D_MODEL = 1024
N_META = 16
GDN_HEADS = 4
HEAD_DIM = 128
GDN_WIDTH = GDN_HEADS * HEAD_DIM
CONF_WIDTH = 512
QKV_WIDTH = 3 * GDN_WIDTH
GDN_CONV = 4
GDN_CHUNK = 64
GDN_GROUP = 4
CONF_CONV = 31
N_KEYS = 128
PEER_HEADS = 8
PEER_TOPK = 16
EPS = 1e-6

LANES = 128
SUBLANES = 8
VMEM_LIMIT_BYTES = 56 * 1024 * 1024


def _sigmoid(x):
    return 1.0 / (1.0 + jnp.exp(-x))


def _silu(x):
    return x * _sigmoid(x)


def _softplus(x):
    return jnp.maximum(x, 0.0) + jnp.log1p(jnp.exp(-jnp.abs(x)))


def _rmsnorm(x, w):
    return x * lax.rsqrt(jnp.mean(x * x, axis=-1, keepdims=True) + EPS) * w


def _dot(a, b, precision=None):
    return jnp.dot(a, b, preferred_element_type=F32, precision=precision)


def _dot_nt(a, b, precision=None):
    return lax.dot_general(a, b, (((1,), (1,)), ((), ())), preferred_element_type=F32,
                           precision=precision)


def _dot_tn(a, b, precision=None):
    return lax.dot_general(a, b, (((0,), (0,)), ((), ())), preferred_element_type=F32,
                           precision=precision)


def _pick_tile(n, candidates):
    for c in candidates:
        if n % c == 0:
            return c
    return n


def _params(*semantics):
    return pltpu.CompilerParams(dimension_semantics=semantics, vmem_limit_bytes=VMEM_LIMIT_BYTES)


def _in_proj_kernel(h_ref, nw_ref, wqkvz_ref, wab_ref, wglu_ref, alog_ref, dtb_ref,
                    qkv_ref, z_ref, g_ref, beta_ref, glu_ref):
    xb = _rmsnorm(h_ref[...], nw_ref[...]).astype(BF16)
    p = _dot(xb, wqkvz_ref[...])
    qkv_ref[...] = p[:, :QKV_WIDTH]
    z_ref[...] = p[:, QKV_WIDTH:]
    ab = _dot(xb, wab_ref[...])
    a = ab[:, :GDN_WIDTH]
    b = ab[:, GDN_WIDTH:]
    g_ref[...] = -jnp.exp(alog_ref[...]) * _softplus(a + dtb_ref[...])
    beta_ref[...] = _sigmoid(b)
    pg = _dot(xb, wglu_ref[...])
    glu_ref[...] = pg[:, :CONF_WIDTH] * _sigmoid(pg[:, CONF_WIDTH:])


def _in_proj(h, nw, wqkvz, wab, wglu, alog, dtb):
    n = h.shape[0]
    tm = _pick_tile(n, (384, 256, 128))
    row = lambda i: (i, 0)
    fixed = lambda i: (0, 0)
    widths = (QKV_WIDTH, GDN_WIDTH, GDN_WIDTH, GDN_WIDTH, CONF_WIDTH)
    return pl.pallas_call(
        _in_proj_kernel,
        grid=(n // tm,),
        in_specs=[
            pl.BlockSpec((tm, D_MODEL), row),
            pl.BlockSpec((1, D_MODEL), fixed),
            pl.BlockSpec(wqkvz.shape, fixed),
            pl.BlockSpec(wab.shape, fixed),
            pl.BlockSpec(wglu.shape, fixed),
            pl.BlockSpec((1, GDN_WIDTH), fixed),
            pl.BlockSpec((1, GDN_WIDTH), fixed),
        ],
        out_specs=[pl.BlockSpec((tm, w), row) for w in widths],
        out_shape=[jax.ShapeDtypeStruct((n, w), F32) for w in widths],
        compiler_params=_params("parallel"),
        name="in_proj",
    )(h, nw, wqkvz, wab, wglu, alog, dtb)


def _gdn_kernel(xq_ref, xk_ref, xv_ref, g_ref, beta_ref, z_ref, cs_q_ref, cs_k_ref, cs_v_ref, s0_ref,
                cwq_ref, cwk_ref, cwv_ref, nw_ref,
                o_ref, s_ref,
                xq_s, xk_s, xv_s, g_s, b_s, u_s, att_s, wq_s, m_s, bm_s, gl_s, sall_s,
                *, chunk, n_pad, seq, bb_count):
    C = chunk
    n_chunks = (seq + n_pad) // C
    head = 8 + n_pad
    n_fac = int(math.log2(C))
    row_i = lax.broadcasted_iota(jnp.int32, (C, C), 0)
    col_i = lax.broadcasted_iota(jnp.int32, (C, C), 1)
    incl = row_i >= col_i
    strict = row_i > col_i
    tril = jnp.where(incl, 1.0, 0.0).astype(F32)
    eye = jnp.where(row_i == col_i, 1.0, 0.0).astype(F32)
    group = GDN_GROUP if (n_chunks - 1) % GDN_GROUP == 0 else 1
    live = lax.broadcasted_iota(jnp.int32, (C, HEAD_DIM), 0) >= n_pad
    cws = (cwq_ref[...], cwk_ref[...], cwv_ref[...])
    nw = nw_ref[...]

    def conv(blk, cw):
        acc = blk[5:5 + C] * cw[0:1]
        for j in range(1, GDN_CONV):
            acc = acc + blk[5 + j:5 + j + C] * cw[j:j + 1]
        return _silu(acc)

    slot = lambda bb, c: bb * n_chunks + c

    def build(items, first):
        each = lambda f, *ls: [f(*a) for a in zip(*ls)]
        at = [(bb, c * C if first else pl.multiple_of(c * C, SUBLANES)) for bb, c in items]
        qc = [conv(xq_s[bb, pl.ds(t0, C + 8), :], cws[0]) for bb, t0 in at]
        kc = [conv(xk_s[bb, pl.ds(t0, C + 8), :], cws[1]) for bb, t0 in at]
        v = [conv(xv_s[bb, pl.ds(t0, C + 8), :], cws[2]) for bb, t0 in at]
        q = each(lambda a: a * lax.rsqrt(jnp.sum(a * a, axis=-1, keepdims=True) + EPS) * (HEAD_DIM ** -0.5), qc)
        k = each(lambda a: a * lax.rsqrt(jnp.sum(a * a, axis=-1, keepdims=True) + EPS), kc)
        g = [g_s[bb, pl.ds(t0, C), :] for bb, t0 in at]
        beta = [b_s[bb, pl.ds(t0, C), :] for bb, t0 in at]
        if first and n_pad:
            q, k, v = ([jnp.where(live, a, 0.0) for a in l] for l in (q, k, v))
        cum = each(lambda a: _dot(tril, jnp.concatenate([a, jnp.where(strict, a[:, :C], 0.0)], axis=1), HIGHEST), g)
        gcum = [a[:, :HEAD_DIM] for a in cum]
        decay = [jnp.where(incl, jnp.exp(a[:, HEAD_DIM:]), 0.0) for a in cum]
        eg = [jnp.exp(a) for a in gcum]
        kb = each(lambda a, b_: a * b_, k, beta)
        kq = each(lambda a, b_, c_: _dot_nt(jnp.concatenate([a, b_], axis=0).astype(BF16), c_.astype(BF16)), kb, q, k)
        p = each(lambda a, d: jnp.where(strict, -(a[:C] * d), 0.0), kq, decay)
        att = each(lambda a, d: jnp.where(incl, a[C:] * d, 0.0), kq, decay)
        t = [eye + a for a in p]
        p = each(lambda a: _dot(a, a, HIGHEST), p)
        for step in range(1, n_fac):
            if step == n_fac - 1:
                t = each(lambda a, b_: a + _dot(b_, a, HIGHEST), t, p)
            else:
                y = each(lambda a, b_: _dot(b_, jnp.concatenate([a, b_], axis=1), HIGHEST), t, p)
                t = each(lambda a, b_: a + b_[:, :C], t, y)
                p = [a[:, C:] for a in y]
        x = each(lambda t_, v_, b_, kb_, eg_: _dot(t_, jnp.concatenate([v_ * b_, kb_ * eg_], axis=1), HIGHEST),
                 t, v, beta, kb, eg)
        g_last = [a[C - 1:C, :] for a in gcum]
        kt = each(lambda k_, gl, gc: k_ * jnp.exp(gl - gc), k, g_last, gcum)
        ktx = each(lambda a, b_: _dot_tn(a.astype(BF16), b_.astype(BF16)), kt, x)
        for i, (bb, c) in enumerate(items):
            n = slot(bb, c)
            u_s[n] = x[i][:, :HEAD_DIM]
            att_s[n] = att[i].astype(BF16)
            wq_s[n] = jnp.concatenate([x[i][:, HEAD_DIM:], q[i] * eg[i]], axis=0).astype(BF16)
            bm_s[n] = ktx[i][:, :HEAD_DIM]
            m_s[n] = (-ktx[i][:, HEAD_DIM:]).astype(BF16)
            gl_s[n] = jnp.exp(g_last[i])

    def carry(items):
        s = [s_ref[bb, 0] for bb, _ in items]
        for (bb, c), s_i in zip(items, s):
            sall_s[slot(bb, c)] = s_i
        new = [gl_s[slot(bb, c)] * s_i + _dot(m_s[slot(bb, c)], s_i.astype(BF16)) + bm_s[slot(bb, c)]
               for (bb, c), s_i in zip(items, s)]
        for (bb, _), s_i in zip(items, new):
            s_ref[bb, 0] = s_i

    def emit(items, first):
        ns = [slot(bb, c) for bb, c in items]
        ws_qs = [_dot(wq_s[n], sall_s[n].astype(BF16)) for n in ns]
        v_new = [u_s[n] - a[:C] for n, a in zip(ns, ws_qs)]
        o = [a[C:] + _dot(att_s[n], vn.astype(BF16)) for n, a, vn in zip(ns, ws_qs, v_new)]
        o = [_rmsnorm(a, nw) for a in o]
        for (bb, c), a in zip(items, o):
            if first:
                rows = C - n_pad
                o_ref[bb, 0:rows, :] = a[n_pad:] * _silu(z_ref[bb, 0:rows, :])
            else:
                r0 = pl.multiple_of(c * C - n_pad, SUBLANES)
                o_ref[bb, pl.ds(r0, C), :] = a * _silu(z_ref[bb, pl.ds(r0, C), :])

    def grouped(fn):
        def body(i, _):
            fn([(0, 1 + i * group + j) for j in range(group)])
            return 0
        lax.fori_loop(0, (n_chunks - 1) // group, body, 0)

    for bb in range(bb_count):
        for x_s, x_ref, cs_ref in ((xq_s, xq_ref, cs_q_ref), (xk_s, xk_ref, cs_k_ref),
                                   (xv_s, xv_ref, cs_v_ref)):
            x_s[bb, 0:head, :] = jnp.zeros((head, HEAD_DIM), F32)
            x_s[bb, head - (GDN_CONV - 1):head, :] = cs_ref[bb]
            x_s[bb, head:head + seq, :] = x_ref[bb]
        if n_pad:
            g_s[bb, 0:n_pad, :] = jnp.zeros((n_pad, HEAD_DIM), F32)
            b_s[bb, 0:n_pad, :] = jnp.zeros((n_pad, HEAD_DIM), F32)
        g_s[bb, n_pad:n_pad + seq, :] = g_ref[bb]
        b_s[bb, n_pad:n_pad + seq, :] = beta_ref[bb]
        s_ref[bb, 0] = s0_ref[bb, 0]

    firsts = [(bb, 0) for bb in range(bb_count)]
    build(firsts, True)
    if n_chunks > 1:
        grouped(lambda items: build(items, False))
    carry(firsts)
    if n_chunks > 1:
        lax.fori_loop(1, n_chunks, lambda c, _: (carry([(0, c)]), 0)[1], 0)
    emit(firsts, True)
    if n_chunks > 1:
        grouped(lambda items: emit(items, False))


def _gdn(qkv, g, beta, z, conv_state, s0, conv_w, norm_w, *, chunk, n_pad, bb_count):
    b, seq, _ = qkv.shape
    n_chunks = (seq + n_pad) // chunk
    assert n_chunks * chunk == seq + n_pad and b % bb_count == 0
    assert n_chunks == 1 or bb_count == 1
    col = lambda off: (lambda i, h: (i, 0, off + h))
    wcol = lambda off: (lambda i, h: (0, off + h))
    seq_blk = (bb_count, seq, HEAD_DIM)
    cs_blk = (bb_count, GDN_CONV - 1, HEAD_DIM)
    s_blk = (bb_count, 1, HEAD_DIM, HEAD_DIM)
    s_map = lambda i, h: (i, h, 0, 0)
    ext = chunk * n_chunks
    per_chunk = lambda r, c, dt=F32: pltpu.VMEM((bb_count * n_chunks, r, c), dt)
    kern = functools.partial(_gdn_kernel, chunk=chunk, n_pad=n_pad, seq=seq, bb_count=bb_count)
    return pl.pallas_call(
        kern,
        grid=(b // bb_count, GDN_HEADS),
        in_specs=[
            pl.BlockSpec(seq_blk, col(0)),
            pl.BlockSpec(seq_blk, col(GDN_HEADS)),
            pl.BlockSpec(seq_blk, col(2 * GDN_HEADS)),
            pl.BlockSpec(seq_blk, col(0)),
            pl.BlockSpec(seq_blk, col(0)),
            pl.BlockSpec(seq_blk, col(0)),
            pl.BlockSpec(cs_blk, col(0)),
            pl.BlockSpec(cs_blk, col(GDN_HEADS)),
            pl.BlockSpec(cs_blk, col(2 * GDN_HEADS)),
            pl.BlockSpec(s_blk, s_map),
            pl.BlockSpec((GDN_CONV, HEAD_DIM), wcol(0)),
            pl.BlockSpec((GDN_CONV, HEAD_DIM), wcol(GDN_HEADS)),
            pl.BlockSpec((GDN_CONV, HEAD_DIM), wcol(2 * GDN_HEADS)),
            pl.BlockSpec((1, HEAD_DIM), lambda i, h: (0, 0)),
        ],
        out_specs=[pl.BlockSpec(seq_blk, col(0)), pl.BlockSpec(s_blk, s_map)],
        out_shape=[jax.ShapeDtypeStruct((b, seq, GDN_WIDTH), F32),
                   jax.ShapeDtypeStruct((b, GDN_HEADS, HEAD_DIM, HEAD_DIM), F32)],
        scratch_shapes=[
            pltpu.VMEM((bb_count, ext + 8, HEAD_DIM), F32),
            pltpu.VMEM((bb_count, ext + 8, HEAD_DIM), F32),
            pltpu.VMEM((bb_count, ext + 8, HEAD_DIM), F32),
            pltpu.VMEM((bb_count, ext, HEAD_DIM), F32),
            pltpu.VMEM((bb_count, ext, HEAD_DIM), F32),
            per_chunk(chunk, HEAD_DIM),
            per_chunk(chunk, chunk, BF16),
            per_chunk(2 * chunk, HEAD_DIM, BF16),
            per_chunk(HEAD_DIM, HEAD_DIM, BF16),
            per_chunk(HEAD_DIM, HEAD_DIM),
            per_chunk(1, HEAD_DIM),
            per_chunk(HEAD_DIM, HEAD_DIM),
        ],
        compiler_params=_params("parallel", "parallel"),
        name="gdn",
    )(qkv, qkv, qkv, g, beta, z, conv_state, conv_state, conv_state, s0,
      conv_w, conv_w, conv_w, norm_w)


CONF_HALO = 32
CONF_ROWS = 16


def _conf_out_seq_kernel(glu_ref, o_ref, h_ref, cstate_ref, dw_ref, dwb_ref, lnw_ref, lnb_ref, wout_ref,
                         out_ref, ext_s, c_s, *, tt):
    t = pl.program_id(1)
    lead = CONF_HALO - (CONF_CONV - 1)

    @pl.when(t == 0)
    def _():
        ext_s[0:lead, :] = jnp.zeros((lead, CONF_WIDTH), F32)
        ext_s[lead:CONF_HALO, :] = cstate_ref[0]

    @pl.when(t > 0)
    def _():
        ext_s[0:CONF_HALO, :] = ext_s[tt:tt + CONF_HALO, :]

    ext_s[CONF_HALO:CONF_HALO + tt, :] = glu_ref[0]
    dw = dw_ref[...]
    bias = dwb_ref[...]

    def rows(i, _):
        r0 = pl.multiple_of(i * CONF_ROWS, SUBLANES)
        blk = ext_s[pl.ds(r0, CONF_ROWS + CONF_HALO), :]
        acc = bias + blk[lead:lead + CONF_ROWS] * dw[0:1]
        for j in range(1, CONF_CONV):
            acc = acc + blk[lead + j:lead + j + CONF_ROWS] * dw[j:j + 1]
        c_s[pl.ds(r0, CONF_ROWS), :] = acc
        return 0

    lax.fori_loop(0, tt // CONF_ROWS, rows, 0)
    c = c_s[...]
    mu = jnp.mean(c, axis=-1, keepdims=True)
    xc = c - mu
    var = jnp.mean(xc * xc, axis=-1, keepdims=True)
    c = _silu(xc * lax.rsqrt(var + EPS) * lnw_ref[...] + lnb_ref[...])
    mixed = jnp.concatenate([o_ref[0], c], axis=1).astype(BF16)
    out_ref[0] = h_ref[0] + _dot(mixed, wout_ref[...])


def _conf_out_seq(glu, o, h, cstate, dw, dwb, lnw, lnb, wout):
    b, seq, _ = glu.shape
    tt = _pick_tile(seq, (688, 512, 256, 128, 64, 16))
    tile = lambda w: pl.BlockSpec((1, tt, w), lambda i, t: (i, t, 0))
    fixed = lambda a: pl.BlockSpec(a.shape, lambda i, t: (0,) * a.ndim)
    return pl.pallas_call(
        functools.partial(_conf_out_seq_kernel, tt=tt),
        grid=(b, seq // tt),
        in_specs=[tile(CONF_WIDTH), tile(GDN_WIDTH), tile(D_MODEL),
                  pl.BlockSpec((1, CONF_CONV - 1, CONF_WIDTH), lambda i, t: (i, 0, 0)),
                  fixed(dw), fixed(dwb), fixed(lnw), fixed(lnb), fixed(wout)],
        out_specs=tile(D_MODEL),
        out_shape=jax.ShapeDtypeStruct((b, seq, D_MODEL), F32),
        scratch_shapes=[pltpu.VMEM((tt + CONF_HALO, CONF_WIDTH), F32),
                        pltpu.VMEM((tt, CONF_WIDTH), F32)],
        compiler_params=_params("parallel", "arbitrary"),
        name="conf_out_seq",
    )(glu, o, h, cstate, dw, dwb, lnw, lnb, wout)


def _conf_out_step_kernel(cext_ref, o_ref, h_ref, dw_ref, dwb_ref, lnw_ref, lnb_ref, wout_ref, out_ref,
                          *, steps):
    dw = dw_ref[...]
    outs = []
    for t in range(steps):
        acc = dwb_ref[...] + cext_ref[t] * dw[0:1]
        for j in range(1, CONF_CONV):
            acc = acc + cext_ref[t + j] * dw[j:j + 1]
        outs.append(acc)
    c = jnp.concatenate(outs, axis=0)
    mu = jnp.mean(c, axis=-1, keepdims=True)
    xc = c - mu
    var = jnp.mean(xc * xc, axis=-1, keepdims=True)
    c = _silu(xc * lax.rsqrt(var + EPS) * lnw_ref[...] + lnb_ref[...])
    mixed = jnp.concatenate([o_ref[...], c], axis=1).astype(BF16)
    out_ref[...] = h_ref[...] + _dot(mixed, wout_ref[...])


def _conf_out_step(cext, o, h, dw, dwb, lnw, lnb, wout):
    steps = cext.shape[0] - (CONF_CONV - 1)
    n = h.shape[0]
    full = lambda a: pl.BlockSpec(a.shape, lambda i: (0,) * a.ndim)
    args = (cext, o, h, dw, dwb, lnw, lnb, wout)
    return pl.pallas_call(
        functools.partial(_conf_out_step_kernel, steps=steps),
        grid=(1,),
        in_specs=[full(a) for a in args],
        out_specs=pl.BlockSpec((n, D_MODEL), lambda i: (0, 0)),
        out_shape=jax.ShapeDtypeStruct((n, D_MODEL), F32),
        compiler_params=_params("arbitrary"),
        name="conf_out_step",
    )(*args)


N_RANKS = PEER_TOPK + 1
RANK_ROWS = 24
NEG_INF = float("-inf")


def _top_rows(s, count, with_rank=False):
    rows = []
    rank = jnp.full(s.shape, float(count), F32) if with_rank else None
    for r in range(count):
        m = jnp.max(s, axis=0, keepdims=True)
        rows.append(m)
        hit = s == m
        if with_rank:
            rank = jnp.where(hit, float(r), rank)
        s = jnp.where(hit, NEG_INF, s)
    return (rows, rank) if with_rank else rows


def _pack_pair(v):
    bits = pltpu.bitcast(v.astype(BF16).astype(F32), jnp.uint32)
    return bits | (bits >> 16)


def _stack_rows(rows, height, tm):
    idx = lax.broadcasted_iota(jnp.int32, (height, tm), 0)
    out = jnp.full((height, tm), NEG_INF, F32)
    for r, row in enumerate(rows):
        out = jnp.where(idx == r, row, out)
    return out


def _peer_score_kernel(h_ref, nw_ref, wqt_ref, keys_ref, xnt_ref, cnt_ref, e1_ref, rank2_ref, e2_ref):
    tm = h_ref.shape[0]
    xn = _rmsnorm(h_ref[...], nw_ref[...])
    xnt = xn.T.astype(BF16)
    xnt_ref[...] = xnt
    qt = _dot(wqt_ref[...], xnt).astype(BF16)
    idx8 = lax.broadcasted_iota(jnp.int32, (SUBLANES, tm), 0)
    for hh in range(PEER_HEADS):
        s = []
        for p in range(2):
            r0 = (hh * 2 + p) * N_KEYS
            s.append(_dot(keys_ref[hh, p], qt[r0:r0 + N_KEYS, :]))
        a, rank1 = _top_rows(s[0], N_RANKS, with_rank=True)
        b, rank2 = _top_rows(s[1], N_RANKS, with_rank=True)
        a_st = _stack_rows(a, RANK_ROWS, tm)
        b_st = _stack_rows(b, RANK_ROWS, tm)
        cands = [a[0] + b_st,
                 jnp.where(lax.broadcasted_iota(jnp.int32, (RANK_ROWS, tm), 0) >= 1,
                           a_st + b[0], NEG_INF)]
        for r1 in range(1, N_RANKS):
            hi = N_RANKS // (r1 + 1) - 1
            if hi >= 1 and r1 <= 4:
                cands.append(jnp.where((idx8 >= 1) & (idx8 <= hi), a[r1] + b_st[0:SUBLANES], NEG_INF))
        cands.append(jnp.where((idx8 >= 5) & (idx8 <= N_RANKS // 2 - 1),
                               a_st[0:SUBLANES] + b[1], NEG_INF))
        cand = jnp.concatenate(cands, axis=0)
        top = _top_rows(cand, N_RANKS)
        z = jnp.zeros_like(top[0])
        for r in range(PEER_TOPK):
            z = z + jnp.exp(top[r] - top[0])
        thr = 0.5 * (top[PEER_TOPK - 1] + top[PEER_TOPK])
        cnt1 = jnp.zeros_like(rank1)
        for r1 in range(PEER_TOPK):
            hi = min(N_RANKS // (r1 + 1), PEER_TOPK)
            cnt = jnp.zeros_like(thr)
            for r2 in range(hi):
                cnt = cnt + jnp.where(a[r1] + b[r2] >= thr, 1.0, 0.0)
            cnt1 = jnp.where(rank1 == float(r1), cnt, cnt1)
        cnt_ref[hh] = _pack_pair(cnt1)
        e1_ref[hh] = _pack_pair(0.5 * jnp.exp(s[0] - a[0]) / z)
        rank2_ref[hh] = rank2.astype(BF16)
        e2_ref[hh] = jnp.exp(s[1] - b[0]).astype(BF16)


def _peer_score(h, nw, wqt, keys):
    n = h.shape[0]
    tm = _pick_tile(n, (PEER_TOKENS, 256, 128))
    heads = lambda: pl.BlockSpec((PEER_HEADS, N_KEYS, tm), lambda i: (0, 0, i))
    return pl.pallas_call(
        _peer_score_kernel,
        grid=(n // tm,),
        in_specs=[pl.BlockSpec((tm, D_MODEL), lambda i: (i, 0)),
                  pl.BlockSpec((1, D_MODEL), lambda i: (0, 0)),
                  pl.BlockSpec(wqt.shape, lambda i: (0, 0)),
                  pl.BlockSpec(keys.shape, lambda i: (0, 0, 0, 0))],
        out_specs=[pl.BlockSpec((D_MODEL, tm), lambda i: (0, i)), heads(), heads(), heads(), heads()],
        out_shape=[jax.ShapeDtypeStruct((D_MODEL, n), BF16)]
        + [jax.ShapeDtypeStruct((PEER_HEADS, N_KEYS, n), dt) for dt in (jnp.uint32, jnp.uint32, BF16, BF16)],
        compiler_params=_params("parallel"),
        name="peer_score",
    )(h, nw, wqt, keys)


EXPERT_CHUNK = 1024
PEER_TOKENS = 512
SQRT_HALF = math.sqrt(0.5)


def _peer_dense_kernel(xnt_ref, cnt_ref, e1_ref, rank2_ref, e2_ref, u_ref, vt_ref, h_ref, fw_ref,
                       out_ref, acc_s, w_s, cnt_s, e1_s, *, final_norm, n_chunks):
    s = pl.program_id(0)
    c = lax.rem(s, n_chunks)
    cur = lax.rem(c, 2)
    tm = xnt_ref.shape[1]
    half = tm // 2

    @pl.when(s == 0)
    def _():
        acc_s[...] = jnp.zeros_like(acc_s)
        w_s[1] = jnp.zeros(w_s.shape[1:], BF16)

    prev = 1 - cur
    n_blocks = EXPERT_CHUNK // N_KEYS
    n_lt = tm // LANES
    per_lt = PEER_HEADS * n_blocks
    for l in range(n_lt):
        cnt_s[l * per_lt:(l + 1) * per_lt, :] = cnt_ref[:, :, l * LANES:(l + 1) * LANES].reshape(per_lt, LANES)
        e1_s[l * per_lt:(l + 1) * per_lt, :] = e1_ref[:, :, l * LANES:(l + 1) * LANES].reshape(per_lt, LANES)

    def rows(ref, hh, j):
        words = jnp.concatenate([jnp.broadcast_to(ref[pl.ds(l * per_lt + hh * n_blocks + j, 1), :], (SUBLANES, LANES))
                                 for l in range(n_lt)], axis=1)
        return jnp.tile(pltpu.bitcast(words, BF16), (N_KEYS // (2 * SUBLANES), 1))

    xnt = xnt_ref[...]
    for j in range(n_blocks):
        r0 = j * N_KEYS
        pre = _dot(u_ref[r0:r0 + N_KEYS, :], xnt)
        act = (pre * (1.0 + lax.erf(pre * SQRT_HALF))).astype(BF16)
        terms = []
        for hh in range(PEER_HEADS):
            e2 = e2_ref[hh]
            sel = jnp.where(rank2_ref[hh] < rows(cnt_s, hh, j), e2, jnp.zeros_like(e2))
            terms.append(rows(e1_s, hh, j) * sel)
        while len(terms) > 1:
            terms = [terms[i] + terms[i + 1] for i in range(0, len(terms), 2)]
        w_s[cur, r0:r0 + N_KEYS, :] = act * terms[0]
        if j == 1:
            acc_s[:, :half] += _dot(vt_ref[0], w_s[prev, :, :half])
        if j == n_blocks // 2 + 1:
            acc_s[:, half:] += _dot(vt_ref[0], w_s[prev, :, half:])

    @pl.when(c == 0)
    def _():
        y = h_ref[...] + acc_s[...].T
        if final_norm:
            y = _rmsnorm(y, fw_ref[...])
        out_ref[...] = y
        acc_s[...] = jnp.zeros_like(acc_s)


def _peer_dense(xnt, cnt, e1, rank2, e2, u, vt, h, fw, *, final_norm):
    n = h.shape[0]
    tm = _pick_tile(n, (PEER_TOKENS, 256, 128))
    n_tiles = n // tm
    n_chunks = u.shape[0] // EXPERT_CHUNK
    assert n_chunks % 2 == 0
    rows_per_chunk = EXPERT_CHUNK // N_KEYS
    tile = lambda s: jnp.minimum(s // n_chunks, n_tiles - 1)
    chunk = lambda s: lax.rem(s, n_chunks)
    done = lambda s: jnp.maximum(s - 1, 0) // n_chunks
    by_chunk = lambda: pl.BlockSpec((PEER_HEADS, rows_per_chunk, tm), lambda s: (0, chunk(s), tile(s)))
    by_tile = lambda: pl.BlockSpec((PEER_HEADS, N_KEYS, tm), lambda s: (0, 0, tile(s)))
    return pl.pallas_call(
        functools.partial(_peer_dense_kernel, final_norm=final_norm, n_chunks=n_chunks),
        grid=(n_tiles * n_chunks + 1,),
        in_specs=[pl.BlockSpec((D_MODEL, tm), lambda s: (0, tile(s))),
                  by_chunk(), by_chunk(), by_tile(), by_tile(),
                  pl.BlockSpec((EXPERT_CHUNK, D_MODEL), lambda s: (chunk(s), 0)),
                  pl.BlockSpec((1, D_MODEL, EXPERT_CHUNK), lambda s: (chunk(s + n_chunks - 1), 0, 0)),
                  pl.BlockSpec((tm, D_MODEL), lambda s: (done(s), 0)),
                  pl.BlockSpec((1, D_MODEL), lambda s: (0, 0))],
        out_specs=pl.BlockSpec((tm, D_MODEL), lambda s: (done(s), 0)),
        out_shape=jax.ShapeDtypeStruct((n, D_MODEL), F32),
        scratch_shapes=[pltpu.VMEM((D_MODEL, tm), F32), pltpu.VMEM((2, EXPERT_CHUNK, tm), BF16)]
        + [pltpu.VMEM((PEER_HEADS * rows_per_chunk * tm // LANES, LANES), jnp.uint32)] * 2,
        compiler_params=_params("arbitrary"),
        name="peer_dense",
    )(xnt, cnt, e1, rank2, e2, u, vt, h, fw)


def _prep_layer(layer, norm_mix_w, w_in, conv_qkv_w, a_log, dt_bias, gdn_norm_w, conf_dw_w, conf_dw_b,
                conf_ln_w, conf_ln_b, w_out, norm_ffn_w, w_query, sub_keys, expert_u, expert_v):
    s0 = QKV_WIDTH
    s1 = s0 + GDN_WIDTH
    s2 = s1 + GDN_HEADS
    s3 = s2 + GDN_HEADS
    w = w_in[layer]
    row = lambda v: v.reshape(1, -1)
    return dict(
        norm_mix_w=row(norm_mix_w[layer]),
        wqkvz=w[:, :s1].astype(BF16),
        wab=jnp.repeat(w[:, s1:s3], HEAD_DIM, axis=1).astype(BF16),
        wglu=w[:, s3:].astype(BF16),
        alog=row(jnp.repeat(a_log[layer], HEAD_DIM)),
        dtb=row(jnp.repeat(dt_bias[layer], HEAD_DIM)),
        conv_w=conv_qkv_w[layer],
        gdn_norm_w=row(gdn_norm_w[layer]),
        dw=conf_dw_w[layer], dwb=row(conf_dw_b[layer]),
        lnw=row(conf_ln_w[layer]), lnb=row(conf_ln_b[layer]),
        wout=w_out[layer].astype(BF16),
        norm_ffn_w=row(norm_ffn_w[layer]),
        wqt=w_query[layer].T.astype(BF16),
        keys=sub_keys[layer].astype(BF16),
        u=expert_u[layer].astype(BF16),
        vt=jnp.swapaxes(expert_v[layer].astype(BF16).reshape(-1, EXPERT_CHUNK, D_MODEL), 1, 2),
    )


def _peer(h, lw, fw, final_norm):
    xnt, cnt, e1, rank2, e2 = _peer_score(h, lw["norm_ffn_w"], lw["wqt"], lw["keys"])
    return _peer_dense(xnt, cnt, e1, rank2, e2, lw["u"], lw["vt"], h, fw, final_norm=final_norm)


def _mix_prompt(h, lw, b, seq):
    n_pad = (-seq) % GDN_CHUNK
    qkv, z, g, beta, glu = _in_proj(h, lw["norm_mix_w"], lw["wqkvz"], lw["wab"], lw["wglu"],
                                    lw["alog"], lw["dtb"])
    shp = lambda a: a.reshape(b, seq, a.shape[-1])
    qkv, z, g, beta, glu = shp(qkv), shp(z), shp(g), shp(beta), shp(glu)
    o, s_new = _gdn(qkv, g, beta, z,
                    jnp.zeros((b, GDN_CONV - 1, QKV_WIDTH), F32),
                    jnp.zeros((b, GDN_HEADS, HEAD_DIM, HEAD_DIM), F32),
                    lw["conv_w"], lw["gdn_norm_w"], chunk=GDN_CHUNK, n_pad=n_pad, bb_count=1)
    h = _conf_out_seq(glu, o, shp(h), jnp.zeros((b, CONF_CONV - 1, CONF_WIDTH), F32),
                      lw["dw"], lw["dwb"], lw["lnw"], lw["lnb"], lw["wout"]).reshape(b * seq, D_MODEL)
    return h, s_new, qkv[:, seq - (GDN_CONV - 1):], glu[:, seq - (CONF_CONV - 1):]


def _mix_sample(h, lw, b, steps, s0, qkv_state, conf_state):
    n_pad = (-steps) % SUBLANES
    to_tb = lambda a: jnp.swapaxes(a, 0, 1)
    qkv, z, g, beta, glu = _in_proj(h, lw["norm_mix_w"], lw["wqkvz"], lw["wab"], lw["wglu"],
                                    lw["alog"], lw["dtb"])
    bt = lambda a: to_tb(a.reshape(steps, b, a.shape[-1]))
    qkv_bt = bt(qkv)
    o, s_new = _gdn(qkv_bt, bt(g), bt(beta), bt(z), qkv_state, s0,
                    lw["conv_w"], lw["gdn_norm_w"], chunk=steps + n_pad, n_pad=n_pad, bb_count=8)
    cext = jnp.concatenate([to_tb(conf_state), glu.reshape(steps, b, CONF_WIDTH)], axis=0)
    h = _conf_out_step(cext, to_tb(o).reshape(steps * b, GDN_WIDTH), h,
                       lw["dw"], lw["dwb"], lw["lnw"], lw["lnb"], lw["wout"])
    qkv_ext = jnp.concatenate([qkv_state, qkv_bt], axis=1)
    return (h, s_new, qkv_ext[:, qkv_ext.shape[1] - (GDN_CONV - 1):],
            to_tb(cext[cext.shape[0] - (CONF_CONV - 1):]))


def kernel(x_prompt, x_sample, state_gdn, state_qkv_conv, state_conf_conv, meta_tokens, norm_mix_w, w_in,
           conv_qkv_w, a_log, dt_bias, gdn_norm_w, conf_dw_w, conf_dw_b, conf_ln_w, conf_ln_b, w_out,
           norm_ffn_w, w_query, sub_keys, expert_u, expert_v, final_norm_w):
    depth = w_in.shape[0]
    layers = [_prep_layer(l, norm_mix_w, w_in, conv_qkv_w, a_log, dt_bias, gdn_norm_w, conf_dw_w,
                          conf_dw_b, conf_ln_w, conf_ln_b, w_out, norm_ffn_w, w_query, sub_keys,
                          expert_u, expert_v) for l in range(depth)]
    fw = final_norm_w.reshape(1, D_MODEL)
    b_p, seq0, _ = x_prompt.shape
    seq = seq0 + N_META
    b_s, steps, _ = x_sample.shape
    n_p, n_s = b_p * seq, b_s * steps
    n_fill = (-(n_p + n_s)) % PEER_TOKENS
    meta = jnp.broadcast_to(meta_tokens[None], (b_p, N_META, D_MODEL))
    h_p = jnp.concatenate([meta, x_prompt], axis=1).reshape(n_p, D_MODEL)
    h_s = jnp.swapaxes(x_sample, 0, 1).reshape(n_s, D_MODEL)
    outs_p, outs_s = [], []
    for li, lw in enumerate(layers):
        h_p, *st_p = _mix_prompt(h_p, lw, b_p, seq)
        h_s, *st_s = _mix_sample(h_s, lw, b_s, steps, state_gdn[li], state_qkv_conv[li], state_conf_conv[li])
        outs_p.append(st_p)
        outs_s.append(st_s)
        h = _peer(jnp.concatenate([h_p, h_s, jnp.zeros((n_fill, D_MODEL), F32)], axis=0), lw, fw,
                  li == depth - 1)
        h_p, h_s = h[:n_p], h[n_p:n_p + n_s]
    y_p = h_p.reshape(b_p, seq, D_MODEL)[:, N_META:]
    y_s = jnp.swapaxes(h_s.reshape(steps, b_s, D_MODEL), 0, 1)
    stack = lambda outs, i: jnp.stack([o[i] for o in outs])
    return (y_p, y_s, stack(outs_p, 0), stack(outs_p, 1), stack(outs_p, 2),
            stack(outs_s, 0), stack(outs_s, 1), stack(outs_s, 2))
```

```python
import functools
import math

import jax
import jax.numpy as jnp
from jax import lax
from jax.experimental import pallas as pl
from jax.experimental.pallas import tpu as pltpu

F32 = jnp.float32
BF16 = jnp.bfloat16
HIGHEST = lax.Precision.HIGHEST

D_MODEL = 1024
N_META = 16
GDN_HEADS = 4
HEAD_DIM = 128
GDN_WIDTH = GDN_HEADS * HEAD_DIM
CONF_WIDTH = 512
QKV_WIDTH = 3 * GDN_WIDTH
GDN_CONV = 4
GDN_CHUNK = 64
GDN_GROUP = 8
CONF_CONV = 31
N_KEYS = 128
PEER_HEADS = 8
PEER_TOPK = 16
EPS = 1e-6

LANES = 128
SUBLANES = 8
VMEM_LIMIT_BYTES = 56 * 1024 * 1024


def _sigmoid(x):
    return 1.0 / (1.0 + jnp.exp(-x))


def _silu(x):
    return x * _sigmoid(x)


def _softplus(x):
    return jnp.maximum(x, 0.0) + jnp.log1p(jnp.exp(-jnp.abs(x)))


def _rmsnorm(x, w):
    return x * lax.rsqrt(jnp.mean(x * x, axis=-1, keepdims=True) + EPS) * w


def _dot(a, b, precision=None):
    return jnp.dot(a, b, preferred_element_type=F32, precision=precision)


def _dot_nt(a, b, precision=None):
    return lax.dot_general(a, b, (((1,), (1,)), ((), ())), preferred_element_type=F32,
                           precision=precision)


def _dot_tn(a, b, precision=None):
    return lax.dot_general(a, b, (((0,), (0,)), ((), ())), preferred_element_type=F32,
                           precision=precision)


def _split(a):
    hi = a.astype(BF16)
    return hi, a - hi.astype(F32)


def _dot3(a, b):
    a_hi, a_rest = _split(a)
    b_hi, b_rest = _split(b)
    return _dot(a_hi, b_hi) + (_dot(a_hi, b_rest.astype(BF16)) + _dot(a_rest.astype(BF16), b_hi))


def _dot_exact_lhs(a_bf16, b):
    b0, r = _split(b)
    b1, r = _split(r)
    return _dot(a_bf16, b0) + (_dot(a_bf16, b1) + _dot(a_bf16, r.astype(BF16)))


def _pick_tile(n, candidates):
    for c in candidates:
        if n % c == 0:
            return c
    return n


def _params(*semantics):
    return pltpu.CompilerParams(dimension_semantics=semantics, vmem_limit_bytes=VMEM_LIMIT_BYTES)


def _in_proj_kernel(h_ref, nw_ref, wqkvz_ref, wab_ref, wglu_ref, alog_ref, dtb_ref,
                    qkv_ref, z_ref, g_ref, beta_ref, glu_ref):
    xb = _rmsnorm(h_ref[...], nw_ref[...]).astype(BF16)
    p = _dot(xb, wqkvz_ref[...])
    qkv_ref[...] = p[:, :QKV_WIDTH]
    z_ref[...] = p[:, QKV_WIDTH:]
    ab = _dot(xb, wab_ref[...])
    a = ab[:, :GDN_WIDTH]
    b = ab[:, GDN_WIDTH:]
    g_ref[...] = -jnp.exp(alog_ref[...]) * _softplus(a + dtb_ref[...])
    beta_ref[...] = _sigmoid(b)
    pg = _dot(xb, wglu_ref[...])
    glu_ref[...] = pg[:, :CONF_WIDTH] * _sigmoid(pg[:, CONF_WIDTH:])


def _in_proj(h, nw, wqkvz, wab, wglu, alog, dtb):
    n = h.shape[0]
    tm = _pick_tile(n, (384, 256, 128))
    row = lambda i: (i, 0)
    fixed = lambda i: (0, 0)
    widths = (QKV_WIDTH, GDN_WIDTH, GDN_WIDTH, GDN_WIDTH, CONF_WIDTH)
    return pl.pallas_call(
        _in_proj_kernel,
        grid=(n // tm,),
        in_specs=[
            pl.BlockSpec((tm, D_MODEL), row),
            pl.BlockSpec((1, D_MODEL), fixed),
            pl.BlockSpec(wqkvz.shape, fixed),
            pl.BlockSpec(wab.shape, fixed),
            pl.BlockSpec(wglu.shape, fixed),
            pl.BlockSpec((1, GDN_WIDTH), fixed),
            pl.BlockSpec((1, GDN_WIDTH), fixed),
        ],
        out_specs=[pl.BlockSpec((tm, w), row) for w in widths],
        out_shape=[jax.ShapeDtypeStruct((n, w), F32) for w in widths],
        compiler_params=_params("parallel"),
        name="in_proj",
    )(h, nw, wqkvz, wab, wglu, alog, dtb)


def _gdn_kernel(xq_ref, xk_ref, xv_ref, g_ref, beta_ref, z_ref, cs_q_ref, cs_k_ref, cs_v_ref, s0_ref,
                cwq_ref, cwk_ref, cwv_ref, nw_ref,
                o_ref, s_ref,
                xq_s, xk_s, xv_s, g_s, b_s, u_s, att_s, wq_s, m_s, bm_s, gl_s, sall_s,
                *, chunk, n_pad, seq, bb_count):
    C = chunk
    n_chunks = (seq + n_pad) // C
    head = 8 + n_pad
    n_fac = int(math.log2(C))
    row_i = lax.broadcasted_iota(jnp.int32, (C, C), 0)
    col_i = lax.broadcasted_iota(jnp.int32, (C, C), 1)
    incl = row_i >= col_i
    strict = row_i > col_i
    tril_b = jnp.where(incl, 1.0, 0.0).astype(BF16)
    eye =jnp.where(row_i == col_i, 1.0, 0.0).astype(F32)
    group = GDN_GROUP
    live = lax.broadcasted_iota(jnp.int32, (C, HEAD_DIM), 0) >= n_pad
    cws = (cwq_ref[...], cwk_ref[...], cwv_ref[...])
    nw = nw_ref[...]

    def conv(blk, cw):
        acc = blk[5:5 + C] * cw[0:1]
        for j in range(1, GDN_CONV):
            acc = acc + blk[5 + j:5 + j + C] * cw[j:j + 1]
        return _silu(acc)

    slot = lambda bb, c: bb * n_chunks + c

    is_first = lambda c: isinstance(c, int) and c == 0

    def build(items, fill=()):
        fill = list(fill)
        tick = lambda: fill.pop(0)() if fill else None
        each = lambda f, *ls: [f(*a) for a in zip(*ls)]
        at = [(bb, c * C if isinstance(c, int) else pl.multiple_of(c * C, SUBLANES)) for bb, c in items]
        qc = [conv(xq_s[bb, pl.ds(t0, C + 8), :], cws[0]) for bb, t0 in at]
        kc = [conv(xk_s[bb, pl.ds(t0, C + 8), :], cws[1]) for bb, t0 in at]
        v = [conv(xv_s[bb, pl.ds(t0, C + 8), :], cws[2]) for bb, t0 in at]
        q = each(lambda a: a * lax.rsqrt(jnp.sum(a * a, axis=-1, keepdims=True) + EPS) * (HEAD_DIM ** -0.5), qc)
        k = each(lambda a: a * lax.rsqrt(jnp.sum(a * a, axis=-1, keepdims=True) + EPS), kc)
        g = [g_s[bb, pl.ds(t0, C), :] for bb, t0 in at]
        beta = [b_s[bb, pl.ds(t0, C), :] for bb, t0 in at]
        if n_pad:
            q, k, v = ([jnp.where(live, a, 0.0) if is_first(c) else a for a, (_, c) in zip(l, items)]
                       for l in (q, k, v))
        cum = each(lambda a: _dot_exact_lhs(tril_b, jnp.concatenate([a, jnp.where(strict, a[:, :C], 0.0)], axis=1)), g)
        tick()
        gcum = [a[:, :HEAD_DIM] for a in cum]
        decay = [jnp.where(incl, jnp.exp(a[:, HEAD_DIM:]), 0.0) for a in cum]
        eg = [jnp.exp(a) for a in gcum]
        kb = each(lambda a, b_: a * b_, k, beta)
        kq = each(lambda a, b_, c_: _dot_nt(jnp.concatenate([a, b_], axis=0).astype(BF16), c_.astype(BF16)), kb, q, k)
        tick()
        p = each(lambda a, d: jnp.where(strict, -(a[:C] * d), 0.0), kq, decay)
        att = each(lambda a, d: jnp.where(incl, a[C:] * d, 0.0), kq, decay)
        t = [eye + a for a in p]
        p = each(lambda a: _dot3(a, a), p)
        tick()
        for step in range(1, n_fac):
            tick()
            if step == n_fac - 1:
                t = each(lambda a, b_: a + _dot3(b_, a), t, p)
            else:
                y = each(lambda a, b_: _dot3(b_, jnp.concatenate([a, b_], axis=1)), t, p)
                t = each(lambda a, b_: a + b_[:, :C], t, y)
                p = [a[:, C:] for a in y]
        x = each(lambda t_, v_, b_, kb_, eg_: _dot3(t_, jnp.concatenate([v_ * b_, kb_ * eg_], axis=1)),
                 t, v, beta, kb, eg)
        while fill:
            tick()
        g_last = [a[C - 1:C, :] for a in gcum]
        kt = each(lambda k_, gl, gc: k_ * jnp.exp(gl - gc), k, g_last, gcum)
        ktx = each(lambda a, b_: _dot_tn(a.astype(BF16), b_.astype(BF16)), kt, x)
        for i, (bb, c) in enumerate(items):
            n = slot(bb, c)
            u_s[n] = x[i][:, :HEAD_DIM]
            att_s[n] = att[i].astype(BF16)
            wq_s[n] = jnp.concatenate([x[i][:, HEAD_DIM:], q[i] * eg[i]], axis=0).astype(BF16)
            bm_s[n] = ktx[i][:, :HEAD_DIM]
            m_s[n] = (-ktx[i][:, HEAD_DIM:]).astype(BF16)
            gl_s[n] = jnp.exp(g_last[i])

    def carry(items):
        s = [s_ref[bb, 0] for bb, _ in items]
        for (bb, c), s_i in zip(items, s):
            sall_s[slot(bb, c)] = s_i
        new = [gl_s[slot(bb, c)] * s_i + _dot(m_s[slot(bb, c)], s_i.astype(BF16)) + bm_s[slot(bb, c)]
               for (bb, c), s_i in zip(items, s)]
        for (bb, _), s_i in zip(items, new):
            s_ref[bb, 0] = s_i

    def emit(items):
        ns = [slot(bb, c) for bb, c in items]
        ws_qs = [_dot(wq_s[n], sall_s[n].astype(BF16)) for n in ns]
        v_new = [u_s[n] - a[:C] for n, a in zip(ns, ws_qs)]
        o = [a[C:] + _dot(att_s[n], vn.astype(BF16)) for n, a, vn in zip(ns, ws_qs, v_new)]
        o = [_rmsnorm(a, nw) for a in o]
        for (bb, c), a in zip(items, o):
            if is_first(c):
                rows = C - n_pad
                o_ref[bb, 0:rows, :] = a[n_pad:] * _silu(z_ref[bb, 0:rows, :])
            else:
                r0 = c * C - n_pad if isinstance(c, int) else pl.multiple_of(c * C - n_pad, SUBLANES)
                o_ref[bb, pl.ds(r0, C), :] = a * _silu(z_ref[bb, pl.ds(r0, C), :])

    lead = (n_chunks - 1) % group + 1
    if lead == 1 and n_chunks > 1:
        lead += group
    leading = [(bb, c) for bb in range(bb_count) for c in range(lead)]

    n_loop = (n_chunks - lead) // group
    later = lambda i: [(0, lead + i * group + j) for j in range(group)]
    carry_range = lambda lo, hi: lax.fori_loop(lo, hi, lambda c, _: (carry([(0, c)]), 0)[1], 0)

    for bb in range(bb_count):
        for x_s, x_ref, cs_ref in ((xq_s, xq_ref, cs_q_ref), (xk_s, xk_ref, cs_k_ref),
                                   (xv_s, xv_ref, cs_v_ref)):
            x_s[bb, 0:head, :] = jnp.zeros((head, HEAD_DIM), F32)
            x_s[bb, head - (GDN_CONV - 1):head, :] = cs_ref[bb]
            x_s[bb, head:head + seq, :] = x_ref[bb]
        if n_pad:
            g_s[bb, 0:n_pad, :] = jnp.zeros((n_pad, HEAD_DIM), F32)
            b_s[bb, 0:n_pad, :] = jnp.zeros((n_pad, HEAD_DIM), F32)
        g_s[bb, n_pad:n_pad + seq, :] = g_ref[bb]
        b_s[bb, n_pad:n_pad + seq, :] = beta_ref[bb]
        s_ref[bb, 0] = s0_ref[bb, 0]

    build(leading)
    carry([(bb, 0) for bb in range(bb_count)])

    def build_later(i, _):
        build(later(i), [functools.partial(carry, [(0, 1 + i * group + j)]) for j in range(group)])
        return 0

    if n_loop:
        lax.fori_loop(0, n_loop, build_later, 0)
    if n_chunks > 1:
        carry_range(1 + n_loop * group, n_chunks)
    emit(leading)
    if n_loop:
        lax.fori_loop(0, n_loop, lambda i, _: (emit(later(i)), 0)[1], 0)


def _gdn(qkv, g, beta, z, conv_state, s0, conv_w, norm_w, *, chunk, n_pad, bb_count):
    b, seq, _ = qkv.shape
    n_chunks = (seq + n_pad) // chunk
    assert n_chunks * chunk == seq + n_pad and b % bb_count == 0
    assert n_chunks == 1 or bb_count == 1
    col = lambda off: (lambda i, h: (i, 0, off + h))
    wcol = lambda off: (lambda i, h: (0, off + h))
    seq_blk = (bb_count, seq, HEAD_DIM)
    cs_blk = (bb_count, GDN_CONV - 1, HEAD_DIM)
    s_blk = (bb_count, 1, HEAD_DIM, HEAD_DIM)
    s_map = lambda i, h: (i, h, 0, 0)
    ext = chunk * n_chunks
    per_chunk = lambda r, c, dt=F32: pltpu.VMEM((bb_count * n_chunks, r, c), dt)
    kern = functools.partial(_gdn_kernel, chunk=chunk, n_pad=n_pad, seq=seq, bb_count=bb_count)
    return pl.pallas_call(
        kern,
        grid=(b // bb_count, GDN_HEADS),
        in_specs=[
            pl.BlockSpec(seq_blk, col(0)),
            pl.BlockSpec(seq_blk, col(GDN_HEADS)),
            pl.BlockSpec(seq_blk, col(2 * GDN_HEADS)),
            pl.BlockSpec(seq_blk, col(0)),
            pl.BlockSpec(seq_blk, col(0)),
            pl.BlockSpec(seq_blk, col(0)),
            pl.BlockSpec(cs_blk, col(0)),
            pl.BlockSpec(cs_blk, col(GDN_HEADS)),
            pl.BlockSpec(cs_blk, col(2 * GDN_HEADS)),
            pl.BlockSpec(s_blk, s_map),
            pl.BlockSpec((GDN_CONV, HEAD_DIM), wcol(0)),
            pl.BlockSpec((GDN_CONV, HEAD_DIM), wcol(GDN_HEADS)),
            pl.BlockSpec((GDN_CONV, HEAD_DIM), wcol(2 * GDN_HEADS)),
            pl.BlockSpec((1, HEAD_DIM), lambda i, h: (0, 0)),
        ],
        out_specs=[pl.BlockSpec(seq_blk, col(0)), pl.BlockSpec(s_blk, s_map)],
        out_shape=[jax.ShapeDtypeStruct((b, seq, GDN_WIDTH), F32),
                   jax.ShapeDtypeStruct((b, GDN_HEADS, HEAD_DIM, HEAD_DIM), F32)],
        scratch_shapes=[
            pltpu.VMEM((bb_count, ext + 8, HEAD_DIM), F32),
            pltpu.VMEM((bb_count, ext + 8, HEAD_DIM), F32),
            pltpu.VMEM((bb_count, ext + 8, HEAD_DIM), F32),
            pltpu.VMEM((bb_count, ext, HEAD_DIM), F32),
            pltpu.VMEM((bb_count, ext, HEAD_DIM), F32),
            per_chunk(chunk, HEAD_DIM),
            per_chunk(chunk, chunk, BF16),
            per_chunk(2 * chunk, HEAD_DIM, BF16),
            per_chunk(HEAD_DIM, HEAD_DIM, BF16),
            per_chunk(HEAD_DIM, HEAD_DIM),
            per_chunk(1, HEAD_DIM),
            per_chunk(HEAD_DIM, HEAD_DIM),
        ],
        compiler_params=_params("parallel", "parallel"),
        name="gdn",
    )(qkv, qkv, qkv, g, beta, z, conv_state, conv_state, conv_state, s0,
      conv_w, conv_w, conv_w, norm_w)


CONF_HALO = 32
CONF_ROWS = 16


def _conf_out_seq_kernel(glu_ref, o_ref, h_ref, cstate_ref, dw_ref, dwb_ref, lnw_ref, lnb_ref, wout_ref,
                         out_ref, ext_s, c_s, sh_s, *, tt):
    t = pl.program_id(1)
    lead = CONF_HALO - (CONF_CONV - 1)

    @pl.when(t == 0)
    def _():
        ext_s[0:lead, :] = jnp.zeros((lead, CONF_WIDTH), F32)
        ext_s[lead:CONF_HALO, :] = cstate_ref[0]

    @pl.when(t > 0)
    def _():
        ext_s[0:CONF_HALO, :] = ext_s[tt:tt + CONF_HALO, :]

    ext_s[CONF_HALO:CONF_HALO + tt, :] = glu_ref[0]
    dw = dw_ref[...]
    bias = dwb_ref[...]

    def rows(i, _):
        r0 = pl.multiple_of(i * CONF_ROWS, SUBLANES)
        blk = ext_s[pl.ds(r0, CONF_ROWS + CONF_HALO), :]
        span = CONF_ROWS + CONF_HALO - SUBLANES
        for s in range(1, SUBLANES):
            sh_s[s - 1] = blk[s:s + span]
        acc = bias
        for j in range(CONF_CONV):
            q, s = divmod(lead + j, SUBLANES)
            rows_j = slice(q * SUBLANES, q * SUBLANES + CONF_ROWS)
            acc = acc + (blk[rows_j] if s == 0 else sh_s[s - 1, rows_j, :]) * dw[j:j + 1]
        c_s[pl.ds(r0, CONF_ROWS), :] = acc
        return 0

    lax.fori_loop(0, tt // CONF_ROWS, rows, 0)
    c = c_s[...]
    mu = jnp.mean(c, axis=-1, keepdims=True)
    xc = c - mu
    var = jnp.mean(xc * xc, axis=-1, keepdims=True)
    c = _silu(xc * lax.rsqrt(var + EPS) * lnw_ref[...] + lnb_ref[...])
    mixed = jnp.concatenate([o_ref[0], c], axis=1).astype(BF16)
    out_ref[0] = h_ref[0] + _dot(mixed, wout_ref[...])


def _conf_out_seq(glu, o, h, cstate, dw, dwb, lnw, lnb, wout):
    b, seq, _ = glu.shape
    tt = _pick_tile(seq, (688, 512, 256, 128, 64, 16))
    tile = lambda w: pl.BlockSpec((1, tt, w), lambda i, t: (i, t, 0))
    fixed = lambda a: pl.BlockSpec(a.shape, lambda i, t: (0,) * a.ndim)
    return pl.pallas_call(
        functools.partial(_conf_out_seq_kernel, tt=tt),
        grid=(b, seq // tt),
        in_specs=[tile(CONF_WIDTH), tile(GDN_WIDTH), tile(D_MODEL),
                  pl.BlockSpec((1, CONF_CONV - 1, CONF_WIDTH), lambda i, t: (i, 0, 0)),
                  fixed(dw), fixed(dwb), fixed(lnw), fixed(lnb), fixed(wout)],
        out_specs=tile(D_MODEL),
        out_shape=jax.ShapeDtypeStruct((b, seq, D_MODEL), F32),
        scratch_shapes=[pltpu.VMEM((tt + CONF_HALO, CONF_WIDTH), F32),
                        pltpu.VMEM((tt, CONF_WIDTH), F32),
                        pltpu.VMEM((SUBLANES - 1, CONF_ROWS + CONF_HALO - SUBLANES, CONF_WIDTH), F32)],
        compiler_params=_params("parallel", "arbitrary"),
        name="conf_out_seq",
    )(glu, o, h, cstate, dw, dwb, lnw, lnb, wout)


def _conf_out_step_kernel(cext_ref, o_ref, h_ref, dw_ref, dwb_ref, lnw_ref, lnb_ref, wout_ref, out_ref,
                          *, steps):
    dw = dw_ref[...]
    outs = []
    for t in range(steps):
        acc = dwb_ref[...] + cext_ref[t] * dw[0:1]
        for j in range(1, CONF_CONV):
            acc = acc + cext_ref[t + j] * dw[j:j + 1]
        outs.append(acc)
    c = jnp.concatenate(outs, axis=0)
    mu = jnp.mean(c, axis=-1, keepdims=True)
    xc = c - mu
    var = jnp.mean(xc * xc, axis=-1, keepdims=True)
    c = _silu(xc * lax.rsqrt(var + EPS) * lnw_ref[...] + lnb_ref[...])
    mixed = jnp.concatenate([o_ref[...], c], axis=1).astype(BF16)
    out_ref[...] = h_ref[...] + _dot(mixed, wout_ref[...])


def _conf_out_step(cext, o, h, dw, dwb, lnw, lnb, wout):
    steps = cext.shape[0] - (CONF_CONV - 1)
    n = h.shape[0]
    full = lambda a: pl.BlockSpec(a.shape, lambda i: (0,) * a.ndim)
    args = (cext, o, h, dw, dwb, lnw, lnb, wout)
    return pl.pallas_call(
        functools.partial(_conf_out_step_kernel, steps=steps),
        grid=(1,),
        in_specs=[full(a) for a in args],
        out_specs=pl.BlockSpec((n, D_MODEL), lambda i: (0, 0)),
        out_shape=jax.ShapeDtypeStruct((n, D_MODEL), F32),
        compiler_params=_params("arbitrary"),
        name="conf_out_step",
    )(*args)


N_RANKS = PEER_TOPK + 1
RANK_ROWS = 24
NEG_INF = float("-inf")


def _top_rows(s, count, with_rank=False):
    rows = []
    rank = jnp.full(s.shape, float(count), F32) if with_rank else None
    for r in range(count):
        m = jnp.max(s, axis=0, keepdims=True)
        rows.append(m)
        hit = s == m
        if with_rank:
            rank = jnp.where(hit, float(r), rank)
        s = jnp.where(hit, NEG_INF, s)
    return (rows, rank) if with_rank else rows


def _pack_pair(v):
    bits = pltpu.bitcast(v.astype(BF16).astype(F32), jnp.uint32)
    return bits | (bits >> 16)


def _stack_rows(rows, height, tm):
    idx = lax.broadcasted_iota(jnp.int32, (height, tm), 0)
    out = jnp.full((height, tm), NEG_INF, F32)
    for r, row in enumerate(rows):
        out = jnp.where(idx == r, row, out)
    return out


def _peer_score_kernel(h_ref, nw_ref, wqt_ref, keys_ref, xnt_ref, cnt_ref, e1_ref, rank2_ref, e2_ref):
    tm = h_ref.shape[0]
    xn = _rmsnorm(h_ref[...], nw_ref[...])
    xnt = xn.T.astype(BF16)
    xnt_ref[...] = xnt
    qt = _dot(wqt_ref[...], xnt).astype(BF16)
    idx8 = lax.broadcasted_iota(jnp.int32, (SUBLANES, tm), 0)
    for hh in range(PEER_HEADS):
        s = []
        for p in range(2):
            r0 = (hh * 2 + p) * N_KEYS
            s.append(_dot(keys_ref[hh, p], qt[r0:r0 + N_KEYS, :]))
        a, rank1 = _top_rows(s[0], N_RANKS, with_rank=True)
        b, rank2 = _top_rows(s[1], N_RANKS, with_rank=True)
        a_st = _stack_rows(a, RANK_ROWS, tm)
        b_st = _stack_rows(b, RANK_ROWS, tm)
        cands = [a[0] + b_st,
                 jnp.where(lax.broadcasted_iota(jnp.int32, (RANK_ROWS, tm), 0) >= 1,
                           a_st + b[0], NEG_INF)]
        for r1 in range(1, N_RANKS):
            hi = N_RANKS // (r1 + 1) - 1
            if hi >= 1 and r1 <= 4:
                cands.append(jnp.where((idx8 >= 1) & (idx8 <= hi), a[r1] + b_st[0:SUBLANES], NEG_INF))
        cands.append(jnp.where((idx8 >= 5) & (idx8 <= N_RANKS // 2 - 1),
                               a_st[0:SUBLANES] + b[1], NEG_INF))
        cand = jnp.concatenate(cands, axis=0)
        top = _top_rows(cand, N_RANKS)
        z = jnp.zeros_like(top[0])
        for r in range(PEER_TOPK):
            z = z + jnp.exp(top[r] - top[0])
        thr = 0.5 * (top[PEER_TOPK - 1] + top[PEER_TOPK])
        cnt1 = jnp.zeros_like(rank1)
        for r1 in range(PEER_TOPK):
            hi = min(N_RANKS // (r1 + 1), PEER_TOPK)
            cnt = jnp.zeros_like(thr)
            for r2 in range(hi):
                cnt = cnt + jnp.where(a[r1] + b[r2] >= thr, 1.0, 0.0)
            cnt1 = jnp.where(rank1 == float(r1), cnt, cnt1)
        cnt_ref[hh] = _pack_pair(cnt1)
        e1_ref[hh] = _pack_pair(0.5 * jnp.exp(s[0] - a[0]) / z)
        rank2_ref[hh] = rank2.astype(BF16)
        e2_ref[hh] = jnp.exp(s[1] - b[0]).astype(BF16)


def _peer_score(h, nw, wqt, keys):
    n = h.shape[0]
    tm = _pick_tile(n, (PEER_TOKENS, 256, 128))
    heads = lambda: pl.BlockSpec((PEER_HEADS, N_KEYS, tm), lambda i: (0, 0, i))
    return pl.pallas_call(
        _peer_score_kernel,
        grid=(n // tm,),
        in_specs=[pl.BlockSpec((tm, D_MODEL), lambda i: (i, 0)),
                  pl.BlockSpec((1, D_MODEL), lambda i: (0, 0)),
                  pl.BlockSpec(wqt.shape, lambda i: (0, 0)),
                  pl.BlockSpec(keys.shape, lambda i: (0, 0, 0, 0))],
        out_specs=[pl.BlockSpec((D_MODEL, tm), lambda i: (0, i)), heads(), heads(), heads(), heads()],
        out_shape=[jax.ShapeDtypeStruct((D_MODEL, n), BF16)]
        + [jax.ShapeDtypeStruct((PEER_HEADS, N_KEYS, n), dt) for dt in (jnp.uint32, jnp.uint32, BF16, BF16)],
        compiler_params=_params("parallel"),
        name="peer_score",
    )(h, nw, wqt, keys)


EXPERT_CHUNK = 1024
PEER_TOKENS = 512
SQRT_HALF = math.sqrt(0.5)


def _peer_dense_kernel(xnt_ref, cnt_ref, e1_ref, rank2_ref, e2_ref, u_ref, vt_ref, h_ref, fw_ref,
                       out_ref, acc_s, w_s, gate_s, cnt_s, e1_s, *, final_norm, n_chunks):
    s = pl.program_id(0)
    c = lax.rem(s, n_chunks)
    cur = lax.rem(c, 2)
    tm = xnt_ref.shape[1]
    half = tm // 2

    @pl.when(s == 0)
    def _():
        acc_s[...] = jnp.zeros_like(acc_s)
        w_s[1] = jnp.zeros(w_s.shape[1:], BF16)

    prev = 1 - cur
    n_blocks = EXPERT_CHUNK // N_KEYS
    n_lt = tm // LANES
    per_lt = PEER_HEADS * n_blocks
    for l in range(n_lt):
        cnt_s[l * per_lt:(l + 1) * per_lt, :] = cnt_ref[:, :, l * LANES:(l + 1) * LANES].reshape(per_lt, LANES)
        e1_s[l * per_lt:(l + 1) * per_lt, :] = e1_ref[:, :, l * LANES:(l + 1) * LANES].reshape(per_lt, LANES)

    def rows(ref, hh, j):
        words = jnp.concatenate([jnp.broadcast_to(ref[pl.ds(l * per_lt + hh * n_blocks + j, 1), :], (SUBLANES, LANES))
                                 for l in range(n_lt)], axis=1)
        return jnp.concatenate([pltpu.bitcast(words, BF16)] * (N_KEYS // (2 * SUBLANES)), axis=0)

    xnt = xnt_ref[...]
    block = lambda j: slice(j * N_KEYS, (j + 1) * N_KEYS)

    def gate(j):
        total = None
        for hh in range(PEER_HEADS):
            e2 = e2_ref[hh]
            sel = jnp.where(rank2_ref[hh] < rows(cnt_s, hh, j), e2, jnp.zeros_like(e2))
            term = rows(e1_s, hh, j) * sel
            total = term if total is None else total + term
        gate_s[block(j), :] = total

    def project(j):
        return _dot(u_ref[block(j), :], xnt)

    def finish(j, pre):
        act = (pre * (1.0 + lax.erf(pre * SQRT_HALF))).astype(BF16)
        w_s[cur, block(j), :] = act * gate_s[block(j), :]

    ahead = 3
    pres = {j: project(j) for j in range(ahead)}
    gate(0)
    for j in range(n_blocks):
        if j + ahead < n_blocks:
            pres[j + ahead] = project(j + ahead)
        if j == 0:
            acc_s[:, :half] += _dot(vt_ref[0], w_s[prev, :, :half])
        if j == n_blocks // 2:
            acc_s[:, half:] += _dot(vt_ref[0], w_s[prev, :, half:])
        if j + 1 < n_blocks:
            gate(j + 1)
        finish(j, pres.pop(j))

    @pl.when(c == 0)
    def _():
        y = h_ref[...] + acc_s[...].T
        if final_norm:
            y = _rmsnorm(y, fw_ref[...])
        out_ref[...] = y
        acc_s[...] = jnp.zeros_like(acc_s)


def _peer_dense(xnt, cnt, e1, rank2, e2, u, vt, h, fw, *, final_norm):
    n = h.shape[0]
    tm = _pick_tile(n, (PEER_TOKENS, 256, 128))
    n_tiles = n // tm
    n_chunks = u.shape[0] // EXPERT_CHUNK
    assert n_chunks % 2 == 0
    rows_per_chunk = EXPERT_CHUNK // N_KEYS
    tile = lambda s: jnp.minimum(s // n_chunks, n_tiles - 1)
    chunk = lambda s: lax.rem(s, n_chunks)
    done = lambda s: jnp.maximum(s - 1, 0) // n_chunks
    by_chunk = lambda: pl.BlockSpec((PEER_HEADS, rows_per_chunk, tm), lambda s: (0, chunk(s), tile(s)))
    by_tile = lambda: pl.BlockSpec((PEER_HEADS, N_KEYS, tm), lambda s: (0, 0, tile(s)))
    return pl.pallas_call(
        functools.partial(_peer_dense_kernel, final_norm=final_norm, n_chunks=n_chunks),
        grid=(n_tiles * n_chunks + 1,),
        in_specs=[pl.BlockSpec((D_MODEL, tm), lambda s: (0, tile(s))),
                  by_chunk(), by_chunk(), by_tile(), by_tile(),
                  pl.BlockSpec((EXPERT_CHUNK, D_MODEL), lambda s: (chunk(s), 0)),
                  pl.BlockSpec((1, D_MODEL, EXPERT_CHUNK), lambda s: (chunk(s + n_chunks - 1), 0, 0)),
                  pl.BlockSpec((tm, D_MODEL), lambda s: (done(s), 0)),
                  pl.BlockSpec((1, D_MODEL), lambda s: (0, 0))],
        out_specs=pl.BlockSpec((tm, D_MODEL), lambda s: (done(s), 0)),
        out_shape=jax.ShapeDtypeStruct((n, D_MODEL), F32),
        scratch_shapes=[pltpu.VMEM((D_MODEL, tm), F32), pltpu.VMEM((2, EXPERT_CHUNK, tm), BF16),
                        pltpu.VMEM((EXPERT_CHUNK, tm), BF16)]
        + [pltpu.VMEM((PEER_HEADS * rows_per_chunk * tm // LANES, LANES), jnp.uint32)] * 2,
        compiler_params=_params("arbitrary"),
        name="peer_dense",
    )(xnt, cnt, e1, rank2, e2, u, vt, h, fw)


def _prep_layer(layer, norm_mix_w, w_in, conv_qkv_w, a_log, dt_bias, gdn_norm_w, conf_dw_w, conf_dw_b,
                conf_ln_w, conf_ln_b, w_out, norm_ffn_w, w_query, sub_keys, expert_u, expert_v):
    s0 = QKV_WIDTH
    s1 = s0 + GDN_WIDTH
    s2 = s1 + GDN_HEADS
    s3 = s2 + GDN_HEADS
    w = w_in[layer]
    row = lambda v: v.reshape(1, -1)
    return dict(
        norm_mix_w=row(norm_mix_w[layer]),
        wqkvz=w[:, :s1].astype(BF16),
        wab=jnp.repeat(w[:, s1:s3], HEAD_DIM, axis=1).astype(BF16),
        wglu=w[:, s3:].astype(BF16),
        alog=row(jnp.repeat(a_log[layer], HEAD_DIM)),
        dtb=row(jnp.repeat(dt_bias[layer], HEAD_DIM)),
        conv_w=conv_qkv_w[layer],
        gdn_norm_w=row(gdn_norm_w[layer]),
        dw=conf_dw_w[layer], dwb=row(conf_dw_b[layer]),
        lnw=row(conf_ln_w[layer]), lnb=row(conf_ln_b[layer]),
        wout=w_out[layer].astype(BF16),
        norm_ffn_w=row(norm_ffn_w[layer]),
        wqt=w_query[layer].T.astype(BF16),
        keys=sub_keys[layer].astype(BF16),
        u=expert_u[layer].astype(BF16),
        vt=jnp.swapaxes(expert_v[layer].astype(BF16).reshape(-1, EXPERT_CHUNK, D_MODEL), 1, 2),
    )


def _peer(h, lw, fw, final_norm):
    xnt, cnt, e1, rank2, e2 = _peer_score(h, lw["norm_ffn_w"], lw["wqt"], lw["keys"])
    return _peer_dense(xnt, cnt, e1, rank2, e2, lw["u"], lw["vt"], h, fw, final_norm=final_norm)


def _mix_prompt(h, lw, b, seq):
    n_pad = (-seq) % GDN_CHUNK
    qkv, z, g, beta, glu = _in_proj(h, lw["norm_mix_w"], lw["wqkvz"], lw["wab"], lw["wglu"],
                                    lw["alog"], lw["dtb"])
    shp = lambda a: a.reshape(b, seq, a.shape[-1])
    qkv, z, g, beta, glu = shp(qkv), shp(z), shp(g), shp(beta), shp(glu)
    o, s_new = _gdn(qkv, g, beta, z,
                    jnp.zeros((b, GDN_CONV - 1, QKV_WIDTH), F32),
                    jnp.zeros((b, GDN_HEADS, HEAD_DIM, HEAD_DIM), F32),
                    lw["conv_w"], lw["gdn_norm_w"], chunk=GDN_CHUNK, n_pad=n_pad, bb_count=1)
    h = _conf_out_seq(glu, o, shp(h), jnp.zeros((b, CONF_CONV - 1, CONF_WIDTH), F32),
                      lw["dw"], lw["dwb"], lw["lnw"], lw["lnb"], lw["wout"]).reshape(b * seq, D_MODEL)
    return h, s_new, qkv[:, seq - (GDN_CONV - 1):], glu[:, seq - (CONF_CONV - 1):]


def _mix_sample(h, lw, b, steps, s0, qkv_state, conf_state):
    n_pad = (-steps) % SUBLANES
    to_tb = lambda a: jnp.swapaxes(a, 0, 1)
    qkv, z, g, beta, glu = _in_proj(h, lw["norm_mix_w"], lw["wqkvz"], lw["wab"], lw["wglu"],
                                    lw["alog"], lw["dtb"])
    bt = lambda a: to_tb(a.reshape(steps, b, a.shape[-1]))
    qkv_bt = bt(qkv)
    o, s_new = _gdn(qkv_bt, bt(g), bt(beta), bt(z), qkv_state, s0,
                    lw["conv_w"], lw["gdn_norm_w"], chunk=steps + n_pad, n_pad=n_pad, bb_count=8)
    cext = jnp.concatenate([to_tb(conf_state), glu.reshape(steps, b, CONF_WIDTH)], axis=0)
    h = _conf_out_step(cext, to_tb(o).reshape(steps * b, GDN_WIDTH), h,
                       lw["dw"], lw["dwb"], lw["lnw"], lw["lnb"], lw["wout"])
    qkv_ext = jnp.concatenate([qkv_state, qkv_bt], axis=1)
    return (h, s_new, qkv_ext[:, qkv_ext.shape[1] - (GDN_CONV - 1):],
            to_tb(cext[cext.shape[0] - (CONF_CONV - 1):]))


def kernel(x_prompt, x_sample, state_gdn, state_qkv_conv, state_conf_conv, meta_tokens, norm_mix_w, w_in,
           conv_qkv_w, a_log, dt_bias, gdn_norm_w, conf_dw_w, conf_dw_b, conf_ln_w, conf_ln_b, w_out,
           norm_ffn_w, w_query, sub_keys, expert_u, expert_v, final_norm_w):
    depth = w_in.shape[0]
    layers = [_prep_layer(l, norm_mix_w, w_in, conv_qkv_w, a_log, dt_bias, gdn_norm_w, conf_dw_w,
                          conf_dw_b, conf_ln_w, conf_ln_b, w_out, norm_ffn_w, w_query, sub_keys,
                          expert_u, expert_v) for l in range(depth)]
    fw = final_norm_w.reshape(1, D_MODEL)
    b_p, seq0, _ = x_prompt.shape
    seq = seq0 + N_META
    b_s, steps, _ = x_sample.shape
    n_p, n_s = b_p * seq, b_s * steps
    n_fill = (-(n_p + n_s)) % PEER_TOKENS
    meta = jnp.broadcast_to(meta_tokens[None], (b_p, N_META, D_MODEL))
    h_p = jnp.concatenate([meta, x_prompt], axis=1).reshape(n_p, D_MODEL)
    h_s = jnp.swapaxes(x_sample, 0, 1).reshape(n_s, D_MODEL)
    outs_p, outs_s = [], []
    for li, lw in enumerate(layers):
        h_p, *st_p = _mix_prompt(h_p, lw, b_p, seq)
        h_s, *st_s = _mix_sample(h_s, lw, b_s, steps, state_gdn[li], state_qkv_conv[li], state_conf_conv[li])
        outs_p.append(st_p)
        outs_s.append(st_s)
        h = _peer(jnp.concatenate([h_p, h_s, jnp.zeros((n_fill, D_MODEL), F32)], axis=0), lw, fw,
                  li == depth - 1)
        h_p, h_s = h[:n_p], h[n_p:n_p + n_s]
    y_p = h_p.reshape(b_p, seq, D_MODEL)[:, N_META:]
    y_s = jnp.swapaxes(h_s.reshape(steps, b_s, D_MODEL), 0, 1)
    stack = lambda outs, i: jnp.stack([o[i] for o in outs])
    return (y_p, y_s, stack(outs_p, 0), stack(outs_p, 1), stack(outs_p, 2),
            stack(outs_s, 0), stack(outs_s, 1), stack(outs_s, 2))
```

```python
import functools
import math

import jax
import jax.numpy as jnp
from jax import lax
from jax.experimental import pallas as pl
from jax.experimental.pallas import tpu as pltpu

F32 = jnp.float32
BF16 = jnp.bfloat16

D_MODEL = 1024
N_META = 16
GDN_HEADS = 4
HEAD_DIM = 128
GDN_WIDTH = GDN_HEADS * HEAD_DIM
CONF_WIDTH = 512
QKV_WIDTH = 3 * GDN_WIDTH
GDN_CONV = 4
GDN_CHUNK = 64
GDN_GROUP = 8
CONF_CONV = 31
N_KEYS = 128
PEER_HEADS = 8
PEER_TOPK = 16
EPS = 1e-6

LANES = 128
SUBLANES = 8
VMEM_LIMIT_BYTES = 56 * 1024 * 1024


def _sigmoid(x):
    return 1.0 / (1.0 + jnp.exp(-x))


def _silu(x):
    return x * _sigmoid(x)


def _softplus(x):
    return jnp.maximum(x, 0.0) + jnp.log1p(jnp.exp(-jnp.abs(x)))


def _rmsnorm(x, w):
    return x * lax.rsqrt(jnp.mean(x * x, axis=-1, keepdims=True) + EPS) * w


def _dot(a, b, precision=None):
    return jnp.dot(a, b, preferred_element_type=F32, precision=precision)


def _dot_nt(a, b, precision=None):
    return lax.dot_general(a, b, (((1,), (1,)), ((), ())), preferred_element_type=F32,
                           precision=precision)


def _dot_tn(a, b, precision=None):
    return lax.dot_general(a, b, (((0,), (0,)), ((), ())), preferred_element_type=F32,
                           precision=precision)


def _split(a):
    hi = a.astype(BF16)
    return hi, a - hi.astype(F32)


def _dot3(a, b):
    a_hi, a_rest = _split(a)
    b_hi, b_rest = _split(b)
    return _dot(a_hi, b_hi) + (_dot(a_hi, b_rest.astype(BF16)) + _dot(a_rest.astype(BF16), b_hi))


def _dot_exact_lhs(a_bf16, b):
    b0, r = _split(b)
    b1, r = _split(r)
    return _dot(a_bf16, b0) + (_dot(a_bf16, b1) + _dot(a_bf16, r.astype(BF16)))


def _pick_tile(n, candidates):
    for c in candidates:
        if n % c == 0:
            return c
    return n


def _params(*semantics):
    return pltpu.CompilerParams(dimension_semantics=semantics, vmem_limit_bytes=VMEM_LIMIT_BYTES)


def _in_proj_kernel(h_ref, nw_ref, wqkvz_ref, wab_ref, wglu_ref, alog_ref, dtb_ref,
                    qkv_ref, z_ref, gb_ref, glu_ref):
    xb = _rmsnorm(h_ref[...], nw_ref[...]).astype(BF16)
    p = _dot(xb, wqkvz_ref[...])
    qkv_ref[...] = p[:, :QKV_WIDTH]
    z_ref[...] = p[:, QKV_WIDTH:]
    ab = _dot(xb, wab_ref[...])
    g = -jnp.exp(alog_ref[...]) * _softplus(ab + dtb_ref[...])
    lane = lax.broadcasted_iota(jnp.int32, ab.shape, 1)
    gb_ref[...] = jnp.where(lane < GDN_HEADS, g, _sigmoid(ab))
    pg = _dot(xb, wglu_ref[...])
    glu_ref[...] = pg[:, :CONF_WIDTH] * _sigmoid(pg[:, CONF_WIDTH:])


def _in_proj(h, nw, wqkvz, wab, wglu, alog, dtb):
    n = h.shape[0]
    tm = _pick_tile(n, (384, 256, 128))
    row = lambda i: (i, 0)
    fixed = lambda i: (0, 0)
    widths = (QKV_WIDTH, GDN_WIDTH, LANES, CONF_WIDTH)
    return pl.pallas_call(
        _in_proj_kernel,
        grid=(n // tm,),
        in_specs=[
            pl.BlockSpec((tm, D_MODEL), row),
            pl.BlockSpec((1, D_MODEL), fixed),
            pl.BlockSpec(wqkvz.shape, fixed),
            pl.BlockSpec(wab.shape, fixed),
            pl.BlockSpec(wglu.shape, fixed),
            pl.BlockSpec((1, LANES), fixed),
            pl.BlockSpec((1, LANES), fixed),
        ],
        out_specs=[pl.BlockSpec((tm, w), row) for w in widths],
        out_shape=[jax.ShapeDtypeStruct((n, w), F32) for w in widths],
        compiler_params=_params("parallel"),
        name="in_proj",
    )(h, nw, wqkvz, wab, wglu, alog, dtb)


def _gdn_kernel(xq_ref, xk_ref, xv_ref, gb_ref, z_ref, cs_q_ref, cs_k_ref, cs_v_ref, s0_ref,
                cwq_ref, cwk_ref, cwv_ref, nw_ref,
                o_ref, s_ref,
                xq_s, xk_s, xv_s, g_s, b_s, u_s, att_s, wq_s, m_s, bm_s, gl_s, sall_s,
                *, chunk, n_pad, seq, bb_count):
    C = chunk
    n_chunks = (seq + n_pad) // C
    head = 8 + n_pad
    n_fac = int(math.log2(C))
    row_i = lax.broadcasted_iota(jnp.int32, (C, C), 0)
    col_i = lax.broadcasted_iota(jnp.int32, (C, C), 1)
    incl = row_i >= col_i
    strict = row_i > col_i
    tril_b = jnp.where(incl, 1.0, 0.0).astype(BF16)
    src_lane = lax.broadcasted_iota(jnp.int32, (LANES, HEAD_DIM), 0)
    pick_g = jnp.where(src_lane == pl.program_id(1), 1.0, 0.0).astype(BF16)
    pick_b = jnp.where(src_lane == pl.program_id(1) + GDN_HEADS, 1.0, 0.0).astype(BF16)
    eye =jnp.where(row_i == col_i, 1.0, 0.0).astype(F32)
    group = GDN_GROUP
    live = lax.broadcasted_iota(jnp.int32, (C, HEAD_DIM), 0) >= n_pad
    cws = (cwq_ref[...], cwk_ref[...], cwv_ref[...])
    nw = nw_ref[...]

    def conv(blk, cw):
        acc = blk[5:5 + C] * cw[0:1]
        for j in range(1, GDN_CONV):
            acc = acc + blk[5 + j:5 + j + C] * cw[j:j + 1]
        return _silu(acc)

    slot = lambda bb, c: bb * n_chunks + c

    is_first = lambda c: isinstance(c, int) and c == 0

    def build(items, fill=()):
        fill = list(fill)
        tick = lambda: fill.pop(0)() if fill else None
        each = lambda f, *ls: [f(*a) for a in zip(*ls)]
        at = [(bb, c * C if isinstance(c, int) else pl.multiple_of(c * C, SUBLANES)) for bb, c in items]
        qc = [conv(xq_s[bb, pl.ds(t0, C + 8), :], cws[0]) for bb, t0 in at]
        kc = [conv(xk_s[bb, pl.ds(t0, C + 8), :], cws[1]) for bb, t0 in at]
        v = [conv(xv_s[bb, pl.ds(t0, C + 8), :], cws[2]) for bb, t0 in at]
        q = each(lambda a: a * lax.rsqrt(jnp.sum(a * a, axis=-1, keepdims=True) + EPS) * (HEAD_DIM ** -0.5), qc)
        k = each(lambda a: a * lax.rsqrt(jnp.sum(a * a, axis=-1, keepdims=True) + EPS), kc)
        g = [g_s[bb, pl.ds(t0, C), :] for bb, t0 in at]
        beta = [b_s[bb, pl.ds(t0, C), :] for bb, t0 in at]
        if n_pad:
            q, k, v = ([jnp.where(live, a, 0.0) if is_first(c) else a for a, (_, c) in zip(l, items)]
                       for l in (q, k, v))
        cum = each(lambda a: _dot_exact_lhs(tril_b, jnp.concatenate([a, jnp.where(strict, a[:, :C], 0.0)], axis=1)), g)
        tick()
        gcum = [a[:, :HEAD_DIM] for a in cum]
        decay = [jnp.where(incl, jnp.exp(a[:, HEAD_DIM:]), 0.0) for a in cum]
        eg = [jnp.exp(a) for a in gcum]
        kb = each(lambda a, b_: a * b_, k, beta)
        kq = each(lambda a, b_, c_: _dot_nt(jnp.concatenate([a, b_], axis=0).astype(BF16), c_.astype(BF16)), kb, q, k)
        tick()
        p = each(lambda a, d: jnp.where(strict, -(a[:C] * d), 0.0), kq, decay)
        att = each(lambda a, d: jnp.where(incl, a[C:] * d, 0.0), kq, decay)
        t = [eye + a for a in p]
        p = each(lambda a: _dot3(a, a), p)
        tick()
        for step in range(1, n_fac):
            tick()
            if step == n_fac - 1:
                t = each(lambda a, b_: a + _dot3(b_, a), t, p)
            else:
                y = each(lambda a, b_: _dot3(b_, jnp.concatenate([a, b_], axis=1)), t, p)
                t = each(lambda a, b_: a + b_[:, :C], t, y)
                p = [a[:, C:] for a in y]
        x = each(lambda t_, v_, b_, kb_, eg_: _dot3(t_, jnp.concatenate([v_ * b_, kb_ * eg_], axis=1)),
                 t, v, beta, kb, eg)
        while fill:
            tick()
        g_last = [a[C - 1:C, :] for a in gcum]
        kt = each(lambda k_, gl, gc: k_ * jnp.exp(gl - gc), k, g_last, gcum)
        ktx = each(lambda a, b_: _dot_tn(a.astype(BF16), b_.astype(BF16)), kt, x)
        for i, (bb, c) in enumerate(items):
            n = slot(bb, c)
            u_s[n] = x[i][:, :HEAD_DIM]
            att_s[n] = att[i].astype(BF16)
            wq_s[n] = jnp.concatenate([x[i][:, HEAD_DIM:], q[i] * eg[i]], axis=0).astype(BF16)
            bm_s[n] = ktx[i][:, :HEAD_DIM]
            m_s[n] = (-ktx[i][:, HEAD_DIM:]).astype(BF16)
            gl_s[n] = jnp.exp(g_last[i])

    def carry(items):
        s = [s_ref[bb, 0] for bb, _ in items]
        for (bb, c), s_i in zip(items, s):
            sall_s[slot(bb, c)] = s_i
        new = [gl_s[slot(bb, c)] * s_i + _dot(m_s[slot(bb, c)], s_i.astype(BF16)) + bm_s[slot(bb, c)]
               for (bb, c), s_i in zip(items, s)]
        for (bb, _), s_i in zip(items, new):
            s_ref[bb, 0] = s_i

    def emit(items):
        ns = [slot(bb, c) for bb, c in items]
        ws_qs = [_dot(wq_s[n], sall_s[n].astype(BF16)) for n in ns]
        v_new = [u_s[n] - a[:C] for n, a in zip(ns, ws_qs)]
        o = [a[C:] + _dot(att_s[n], vn.astype(BF16)) for n, a, vn in zip(ns, ws_qs, v_new)]
        o = [_rmsnorm(a, nw) for a in o]
        for (bb, c), a in zip(items, o):
            if is_first(c):
                rows = C - n_pad
                o_ref[bb, 0:rows, :] = (a[n_pad:] * _silu(z_ref[bb, 0:rows, :])).astype(BF16)
            else:
                r0 = c * C - n_pad if isinstance(c, int) else pl.multiple_of(c * C - n_pad, 2 * SUBLANES)
                o_ref[bb, pl.ds(r0, C), :] = (a * _silu(z_ref[bb, pl.ds(r0, C), :])).astype(BF16)

    lead = (n_chunks - 1) % group + 1
    if lead == 1 and n_chunks > 1:
        lead += group
    leading = [(bb, c) for bb in range(bb_count) for c in range(lead)]

    n_loop = (n_chunks - lead) // group
    later = lambda i: [(0, lead + i * group + j) for j in range(group)]
    carry_range = lambda lo, hi: lax.fori_loop(lo, hi, lambda c, _: (carry([(0, c)]), 0)[1], 0)

    for bb in range(bb_count):
        for x_s, x_ref, cs_ref in ((xq_s, xq_ref, cs_q_ref), (xk_s, xk_ref, cs_k_ref),
                                   (xv_s, xv_ref, cs_v_ref)):
            x_s[bb, 0:head, :] = jnp.zeros((head, HEAD_DIM), F32)
            x_s[bb, head - (GDN_CONV - 1):head, :] = cs_ref[bb]
            x_s[bb, head:head + seq, :] = x_ref[bb]
        if n_pad:
            g_s[bb, 0:n_pad, :] = jnp.zeros((n_pad, HEAD_DIM), F32)
            b_s[bb, 0:n_pad, :] = jnp.zeros((n_pad, HEAD_DIM), F32)
        pieces = []
        rest = gb_ref[bb]
        for _ in range(3):
            hi, rest = _split(rest)
            pieces.append(hi)
        spread = lambda sel: _dot(pieces[0], sel) + (_dot(pieces[1], sel) + _dot(pieces[2], sel))
        g_s[bb, n_pad:n_pad + seq, :] = spread(pick_g)
        b_s[bb, n_pad:n_pad + seq, :] = spread(pick_b)
        s_ref[bb, 0] = s0_ref[bb, 0]

    build(leading)
    carry([(bb, 0) for bb in range(bb_count)])

    def build_later(i, _):
        build(later(i), [functools.partial(carry, [(0, 1 + i * group + j)]) for j in range(group)])
        return 0

    if n_loop:
        lax.fori_loop(0, n_loop, build_later, 0)
    if n_chunks > 1:
        carry_range(1 + n_loop * group, n_chunks)
    emit(leading)
    if n_loop:
        lax.fori_loop(0, n_loop, lambda i, _: (emit(later(i)), 0)[1], 0)


def _gdn(qkv, gb, z, conv_state, s0, conv_w, norm_w, *, chunk, n_pad, bb_count):
    b, seq, _ = qkv.shape
    n_chunks = (seq + n_pad) // chunk
    assert n_chunks * chunk == seq + n_pad and b % bb_count == 0
    assert n_chunks == 1 or bb_count == 1
    col = lambda off: (lambda i, h: (i, 0, off + h))
    wcol = lambda off: (lambda i, h: (0, off + h))
    seq_blk = (bb_count, seq, HEAD_DIM)
    cs_blk = (bb_count, GDN_CONV - 1, HEAD_DIM)
    s_blk = (bb_count, 1, HEAD_DIM, HEAD_DIM)
    s_map = lambda i, h: (i, h, 0, 0)
    ext = chunk * n_chunks
    per_chunk = lambda r, c, dt=F32: pltpu.VMEM((bb_count * n_chunks, r, c), dt)
    kern = functools.partial(_gdn_kernel, chunk=chunk, n_pad=n_pad, seq=seq, bb_count=bb_count)
    return pl.pallas_call(
        kern,
        grid=(b // bb_count, GDN_HEADS),
        in_specs=[
            pl.BlockSpec(seq_blk, col(0)),
            pl.BlockSpec(seq_blk, col(GDN_HEADS)),
            pl.BlockSpec(seq_blk, col(2 * GDN_HEADS)),
            pl.BlockSpec((bb_count, seq, LANES), lambda i, h: (i, 0, 0)),
            pl.BlockSpec(seq_blk, col(0)),
            pl.BlockSpec(cs_blk, col(0)),
            pl.BlockSpec(cs_blk, col(GDN_HEADS)),
            pl.BlockSpec(cs_blk, col(2 * GDN_HEADS)),
            pl.BlockSpec(s_blk, s_map),
            pl.BlockSpec((GDN_CONV, HEAD_DIM), wcol(0)),
            pl.BlockSpec((GDN_CONV, HEAD_DIM), wcol(GDN_HEADS)),
            pl.BlockSpec((GDN_CONV, HEAD_DIM), wcol(2 * GDN_HEADS)),
            pl.BlockSpec((1, HEAD_DIM), lambda i, h: (0, 0)),
        ],
        out_specs=[pl.BlockSpec(seq_blk, col(0)), pl.BlockSpec(s_blk, s_map)],
        out_shape=[jax.ShapeDtypeStruct((b, seq, GDN_WIDTH), BF16),
                   jax.ShapeDtypeStruct((b, GDN_HEADS, HEAD_DIM, HEAD_DIM), F32)],
        scratch_shapes=[
            pltpu.VMEM((bb_count, ext + 8, HEAD_DIM), F32),
            pltpu.VMEM((bb_count, ext + 8, HEAD_DIM), F32),
            pltpu.VMEM((bb_count, ext + 8, HEAD_DIM), F32),
            pltpu.VMEM((bb_count, ext, HEAD_DIM), F32),
            pltpu.VMEM((bb_count, ext, HEAD_DIM), F32),
            per_chunk(chunk, HEAD_DIM),
            per_chunk(chunk, chunk, BF16),
            per_chunk(2 * chunk, HEAD_DIM, BF16),
            per_chunk(HEAD_DIM, HEAD_DIM, BF16),
            per_chunk(HEAD_DIM, HEAD_DIM),
            per_chunk(1, HEAD_DIM),
            per_chunk(HEAD_DIM, HEAD_DIM),
        ],
        compiler_params=_params("parallel", "parallel"),
        name="gdn",
    )(qkv, qkv, qkv, gb, z, conv_state, conv_state, conv_state, s0,
      conv_w, conv_w, conv_w, norm_w)


CONF_HALO = 32
CONF_ROWS = 16


def _conf_out_seq_kernel(glu_ref, o_ref, h_ref, cstate_ref, dw_ref, dwb_ref, lnw_ref, lnb_ref, wout_ref,
                         out_ref, ext_s, c_s, sh_s, *, tt):
    t = pl.program_id(1)
    lead = CONF_HALO - (CONF_CONV - 1)

    @pl.when(t == 0)
    def _():
        ext_s[0:lead, :] = jnp.zeros((lead, CONF_WIDTH), F32)
        ext_s[lead:CONF_HALO, :] = cstate_ref[0]

    @pl.when(t > 0)
    def _():
        ext_s[0:CONF_HALO, :] = ext_s[tt:tt + CONF_HALO, :]

    ext_s[CONF_HALO:CONF_HALO + tt, :] = glu_ref[0]
    dw = dw_ref[...]
    bias = dwb_ref[...]

    def rows(i, _):
        r0 = pl.multiple_of(i * CONF_ROWS, SUBLANES)
        blk = ext_s[pl.ds(r0, CONF_ROWS + CONF_HALO), :]
        span = CONF_ROWS + CONF_HALO - SUBLANES
        for s in range(1, SUBLANES):
            sh_s[s - 1] = blk[s:s + span]
        acc = bias
        for j in range(CONF_CONV):
            q, s = divmod(lead + j, SUBLANES)
            rows_j = slice(q * SUBLANES, q * SUBLANES + CONF_ROWS)
            acc = acc + (blk[rows_j] if s == 0 else sh_s[s - 1, rows_j, :]) * dw[j:j + 1]
        c_s[pl.ds(r0, CONF_ROWS), :] = acc
        return 0

    lax.fori_loop(0, tt // CONF_ROWS, rows, 0)
    c = c_s[...]
    mu = jnp.mean(c, axis=-1, keepdims=True)
    xc = c - mu
    var = jnp.mean(xc * xc, axis=-1, keepdims=True)
    c = _silu(xc * lax.rsqrt(var + EPS) * lnw_ref[...] + lnb_ref[...])
    mixed = jnp.concatenate([o_ref[0], c.astype(BF16)], axis=1)
    out_ref[0] = h_ref[0] + _dot(mixed, wout_ref[...])


def _conf_out_seq(glu, o, h, cstate, dw, dwb, lnw, lnb, wout):
    b, seq, _ = glu.shape
    tt = _pick_tile(seq, (688, 512, 256, 128, 64, 16))
    tile = lambda w: pl.BlockSpec((1, tt, w), lambda i, t: (i, t, 0))
    fixed = lambda a: pl.BlockSpec(a.shape, lambda i, t: (0,) * a.ndim)
    return pl.pallas_call(
        functools.partial(_conf_out_seq_kernel, tt=tt),
        grid=(b, seq // tt),
        in_specs=[tile(CONF_WIDTH), tile(GDN_WIDTH), tile(D_MODEL),
                  pl.BlockSpec((1, CONF_CONV - 1, CONF_WIDTH), lambda i, t: (i, 0, 0)),
                  fixed(dw), fixed(dwb), fixed(lnw), fixed(lnb), fixed(wout)],
        out_specs=tile(D_MODEL),
        out_shape=jax.ShapeDtypeStruct((b, seq, D_MODEL), F32),
        scratch_shapes=[pltpu.VMEM((tt + CONF_HALO, CONF_WIDTH), F32),
                        pltpu.VMEM((tt, CONF_WIDTH), F32),
                        pltpu.VMEM((SUBLANES - 1, CONF_ROWS + CONF_HALO - SUBLANES, CONF_WIDTH), F32)],
        compiler_params=_params("parallel", "arbitrary"),
        name="conf_out_seq",
    )(glu, o, h, cstate, dw, dwb, lnw, lnb, wout)


def _conf_out_step_kernel(cext_ref, o_ref, h_ref, dw_ref, dwb_ref, lnw_ref, lnb_ref, wout_ref, out_ref,
                          *, steps):
    dw = dw_ref[...]
    outs = []
    for t in range(steps):
        acc = dwb_ref[...] + cext_ref[t] * dw[0:1]
        for j in range(1, CONF_CONV):
            acc = acc + cext_ref[t + j] * dw[j:j + 1]
        outs.append(acc)
    c = jnp.concatenate(outs, axis=0)
    mu = jnp.mean(c, axis=-1, keepdims=True)
    xc = c - mu
    var = jnp.mean(xc * xc, axis=-1, keepdims=True)
    c = _silu(xc * lax.rsqrt(var + EPS) * lnw_ref[...] + lnb_ref[...])
    mixed = jnp.concatenate([o_ref[...], c.astype(BF16)], axis=1)
    out_ref[...] = h_ref[...] + _dot(mixed, wout_ref[...])


def _conf_out_step(cext, o, h, dw, dwb, lnw, lnb, wout):
    steps = cext.shape[0] - (CONF_CONV - 1)
    n = h.shape[0]
    full = lambda a: pl.BlockSpec(a.shape, lambda i: (0,) * a.ndim)
    args = (cext, o, h, dw, dwb, lnw, lnb, wout)
    return pl.pallas_call(
        functools.partial(_conf_out_step_kernel, steps=steps),
        grid=(1,),
        in_specs=[full(a) for a in args],
        out_specs=pl.BlockSpec((n, D_MODEL), lambda i: (0, 0)),
        out_shape=jax.ShapeDtypeStruct((n, D_MODEL), F32),
        compiler_params=_params("arbitrary"),
        name="conf_out_step",
    )(*args)


N_RANKS = PEER_TOPK + 1
RANK_ROWS = 24
NEG_INF = float("-inf")


def _top_rows(s, count):
    rows = []
    for r in range(count):
        m = jnp.max(s, axis=0, keepdims=True)
        rows.append(m)
        s = jnp.where(s == m, NEG_INF, s)
    return rows


TOP_WAYS = 4


def _top_rows_chained(s, count):
    n = s.shape[0] // TOP_WAYS
    lv = [s[i * n:(i + 1) * n] for i in range(TOP_WAYS)]
    for i, j in ((0, 1), (2, 3), (0, 2), (1, 3), (1, 2)):
        lv[i], lv[j] = jnp.maximum(lv[i], lv[j]), jnp.minimum(lv[i], lv[j])
    rows = []
    for r in range(count):
        m = jnp.max(lv[0], axis=0, keepdims=True)
        rows.append(m)
        hit = lv[0] == m
        for i in range(TOP_WAYS - 1):
            lv[i] = jnp.where(hit, lv[i + 1], lv[i])
        lv[-1] = jnp.where(hit, NEG_INF, lv[-1])
    return rows


def _prefix_count(hits):
    out = jnp.zeros(hits[0].shape, F32)
    for r, hit in enumerate(hits):
        out = jnp.where(hit, float(r + 1), out)
    return out


def _stack_rows(rows, height, tm):
    idx = lax.broadcasted_iota(jnp.int32, (height, tm), 0)
    out = jnp.full((height, tm), NEG_INF, F32)
    for r, row in enumerate(rows):
        out = jnp.where(idx == r, row, out)
    return out


def _peer_score_kernel(h_ref, nw_ref, wqt_ref, keys_ref, xnt_ref, cnt_ref, e1_ref, rank2_ref, e2_ref):
    tm = h_ref.shape[0]
    xn = _rmsnorm(h_ref[...], nw_ref[...])
    xnt = xn.T.astype(BF16)
    xnt_ref[...] = xnt
    qt = _dot(wqt_ref[...], xnt).astype(BF16)
    idx8 = lax.broadcasted_iota(jnp.int32, (SUBLANES, tm), 0)
    for hh in range(PEER_HEADS):
        s = []
        for p in range(2):
            r0 = (hh * 2 + p) * N_KEYS
            s.append(_dot(keys_ref[hh, p], qt[r0:r0 + N_KEYS, :]))
        a = _top_rows_chained(s[0], N_RANKS)
        b = _top_rows_chained(s[1], N_RANKS)
        a_st = _stack_rows(a, RANK_ROWS, tm)
        b_st = _stack_rows(b, RANK_ROWS, tm)
        cands = [a[0] + b_st,
                 jnp.where(lax.broadcasted_iota(jnp.int32, (RANK_ROWS, tm), 0) >= 1,
                           a_st + b[0], NEG_INF)]
        for r1 in range(1, N_RANKS):
            hi = N_RANKS // (r1 + 1) - 1
            if hi >= 1 and r1 <= 4:
                cands.append(jnp.where((idx8 >= 1) & (idx8 <= hi), a[r1] + b_st[0:SUBLANES], NEG_INF))
        cands.append(jnp.where((idx8 >= 5) & (idx8 <= N_RANKS // 2 - 1),
                               a_st[0:SUBLANES] + b[1], NEG_INF))
        cand = jnp.concatenate(cands, axis=0)
        top = _top_rows(cand, N_RANKS)
        z = jnp.zeros_like(top[0])
        for r in range(PEER_TOPK):
            z = z + jnp.exp(top[r] - top[0])
        thr = 0.5 * (top[PEER_TOPK - 1] + top[PEER_TOPK])
        cnt1 = _prefix_count([s[0] >= thr - b[r2] for r2 in range(PEER_TOPK)])
        rank2 = _prefix_count([s[1] < b[r] for r in range(PEER_TOPK)])
        cnt_ref[hh] = cnt1
        e1_ref[hh] = 0.5 * jnp.exp(s[0] - a[0]) / z
        rank2_ref[hh] = rank2.astype(BF16)
        e2_ref[hh] = jnp.exp(s[1] - b[0]).astype(BF16)


def _peer_score(h, nw, wqt, keys):
    n = h.shape[0]
    tm = _pick_tile(n, (PEER_TOKENS, 256, 128))
    heads = lambda: pl.BlockSpec((PEER_HEADS, N_KEYS, tm), lambda i: (0, 0, i))
    return pl.pallas_call(
        _peer_score_kernel,
        grid=(n // tm,),
        in_specs=[pl.BlockSpec((tm, D_MODEL), lambda i: (i, 0)),
                  pl.BlockSpec((1, D_MODEL), lambda i: (0, 0)),
                  pl.BlockSpec(wqt.shape, lambda i: (0, 0)),
                  pl.BlockSpec(keys.shape, lambda i: (0, 0, 0, 0))],
        out_specs=[pl.BlockSpec((D_MODEL, tm), lambda i: (0, i)), heads(), heads(), heads(), heads()],
        out_shape=[jax.ShapeDtypeStruct((D_MODEL, n), BF16)]
        + [jax.ShapeDtypeStruct((PEER_HEADS, N_KEYS, n), dt) for dt in (F32, F32, BF16, BF16)],
        compiler_params=_params("parallel"),
        name="peer_score",
    )(h, nw, wqt, keys)


EXPERT_CHUNK = 1024
PEER_TOKENS = 512
PEER_DENSE_TOKENS = 512
SQRT_HALF = math.sqrt(0.5)


def _peer_dense_kernel(xnt_ref, cnt_ref, e1_ref, rank2_ref, e2_ref, u_ref, vt_ref, h_ref, fw_ref,
                       out_ref, acc_s, w_s, gate_s, cnt_s, e1_s, *, final_norm, n_chunks):
    s = pl.program_id(0)
    c = lax.rem(s, n_chunks)
    cur = lax.rem(c, 2)
    tm = xnt_ref.shape[1]
    half = tm // 2

    @pl.when(s == 0)
    def _():
        acc_s[...] = jnp.zeros_like(acc_s)
        w_s[1] = jnp.zeros(w_s.shape[1:], BF16)

    prev = 1 - cur
    n_blocks = EXPERT_CHUNK // N_KEYS
    n_lt = tm // LANES
    per_lt = PEER_HEADS * n_blocks
    for l in range(n_lt):
        cnt_s[l * per_lt:(l + 1) * per_lt, :] = cnt_ref[:, :, l * LANES:(l + 1) * LANES].reshape(per_lt, LANES)
        e1_s[l * per_lt:(l + 1) * per_lt, :] = e1_ref[:, :, l * LANES:(l + 1) * LANES].reshape(per_lt, LANES)

    def rows(ref, hh, j):
        words = jnp.concatenate([jnp.broadcast_to(ref[pl.ds(l * per_lt + hh * n_blocks + j, 1), :], (SUBLANES, LANES))
                                 for l in range(n_lt)], axis=1)
        packed = jnp.concatenate([words, words], axis=0).astype(BF16)
        return jnp.concatenate([packed] * (N_KEYS // (2 * SUBLANES)), axis=0)

    xnt = xnt_ref[...]
    block = lambda j: slice(j * N_KEYS, (j + 1) * N_KEYS)

    def gate(j):
        total = None
        for hh in range(PEER_HEADS):
            e2 = e2_ref[hh]
            sel = jnp.where(rank2_ref[hh] < rows(cnt_s, hh, j), e2, jnp.zeros_like(e2))
            term = rows(e1_s, hh, j) * sel
            total = term if total is None else total + term
        gate_s[block(j), :] = total

    def project(j):
        return _dot(u_ref[block(j), :], xnt)

    def finish(j, pre):
        act = (pre * (1.0 + lax.erf(pre * SQRT_HALF))).astype(BF16)
        w_s[cur, block(j), :] = act * gate_s[block(j), :]

    ahead = 3
    pres = {j: project(j) for j in range(ahead)}
    gate(0)
    for j in range(n_blocks):
        if j + ahead < n_blocks:
            pres[j + ahead] = project(j + ahead)
        if j == 0:
            acc_s[:, :half] += _dot(vt_ref[0], w_s[prev, :, :half])
        if j == n_blocks // 2:
            acc_s[:, half:] += _dot(vt_ref[0], w_s[prev, :, half:])
        if j + 1 < n_blocks:
            gate(j + 1)
        finish(j, pres.pop(j))

    @pl.when(c == 0)
    def _():
        y = h_ref[...] + acc_s[...].T
        if final_norm:
            y = _rmsnorm(y, fw_ref[...])
        out_ref[...] = y
        acc_s[...] = jnp.zeros_like(acc_s)


def _peer_dense(xnt, cnt, e1, rank2, e2, u, vt, h, fw, *, final_norm):
    n = h.shape[0]
    tm = _pick_tile(n, (PEER_DENSE_TOKENS, PEER_TOKENS, 256, 128))
    n_tiles = n // tm
    n_chunks = u.shape[0] // EXPERT_CHUNK
    assert n_chunks % 2 == 0
    rows_per_chunk = EXPERT_CHUNK // N_KEYS
    tile = lambda s: jnp.minimum(s // n_chunks, n_tiles - 1)
    chunk = lambda s: lax.rem(s, n_chunks)
    done = lambda s: jnp.maximum(s - 1, 0) // n_chunks
    by_chunk = lambda: pl.BlockSpec((PEER_HEADS, rows_per_chunk, tm), lambda s: (0, chunk(s), tile(s)))
    by_tile = lambda: pl.BlockSpec((PEER_HEADS, N_KEYS, tm), lambda s: (0, 0, tile(s)))
    return pl.pallas_call(
        functools.partial(_peer_dense_kernel, final_norm=final_norm, n_chunks=n_chunks),
        grid=(n_tiles * n_chunks + 1,),
        in_specs=[pl.BlockSpec((D_MODEL, tm), lambda s: (0, tile(s))),
                  by_chunk(), by_chunk(), by_tile(), by_tile(),
                  pl.BlockSpec((EXPERT_CHUNK, D_MODEL), lambda s: (chunk(s), 0)),
                  pl.BlockSpec((1, D_MODEL, EXPERT_CHUNK), lambda s: (chunk(s + n_chunks - 1), 0, 0)),
                  pl.BlockSpec((tm, D_MODEL), lambda s: (done(s), 0)),
                  pl.BlockSpec((1, D_MODEL), lambda s: (0, 0))],
        out_specs=pl.BlockSpec((tm, D_MODEL), lambda s: (done(s), 0)),
        out_shape=jax.ShapeDtypeStruct((n, D_MODEL), F32),
        scratch_shapes=[pltpu.VMEM((D_MODEL, tm), F32), pltpu.VMEM((2, EXPERT_CHUNK, tm), BF16),
                        pltpu.VMEM((EXPERT_CHUNK, tm), BF16)]
        + [pltpu.VMEM((PEER_HEADS * rows_per_chunk * tm // LANES, LANES), F32)] * 2,
        compiler_params=_params("arbitrary"),
        name="peer_dense",
    )(xnt, cnt, e1, rank2, e2, u, vt, h, fw)


def _prep_layer(layer, norm_mix_w, w_in, conv_qkv_w, a_log, dt_bias, gdn_norm_w, conf_dw_w, conf_dw_b,
                conf_ln_w, conf_ln_b, w_out, norm_ffn_w, w_query, sub_keys, expert_u, expert_v):
    s0 = QKV_WIDTH
    s1 = s0 + GDN_WIDTH
    s2 = s1 + GDN_HEADS
    s3 = s2 + GDN_HEADS
    w = w_in[layer]
    row = lambda v: v.reshape(1, -1)
    return dict(
        norm_mix_w=row(norm_mix_w[layer]),
        wqkvz=w[:, :s1].astype(BF16),
        wab=jnp.pad(w[:, s1:s3], ((0, 0), (0, LANES - 2 * GDN_HEADS))).astype(BF16),
        wglu=w[:, s3:].astype(BF16),
        alog=row(jnp.pad(a_log[layer], (0, LANES - GDN_HEADS))),
        dtb=row(jnp.pad(dt_bias[layer], (0, LANES - GDN_HEADS))),
        conv_w=conv_qkv_w[layer],
        gdn_norm_w=row(gdn_norm_w[layer]),
        dw=conf_dw_w[layer], dwb=row(conf_dw_b[layer]),
        lnw=row(conf_ln_w[layer]), lnb=row(conf_ln_b[layer]),
        wout=w_out[layer].astype(BF16),
        norm_ffn_w=row(norm_ffn_w[layer]),
        wqt=w_query[layer].T.astype(BF16),
        keys=sub_keys[layer].astype(BF16),
        u=expert_u[layer].astype(BF16),
        vt=jnp.swapaxes(expert_v[layer].astype(BF16).reshape(-1, EXPERT_CHUNK, D_MODEL), 1, 2),
    )


def _peer(h, lw, fw, final_norm):
    xnt, cnt, e1, rank2, e2 = _peer_score(h, lw["norm_ffn_w"], lw["wqt"], lw["keys"])
    return _peer_dense(xnt, cnt, e1, rank2, e2, lw["u"], lw["vt"], h, fw, final_norm=final_norm)


def _mix_prompt(h, lw, b, seq):
    n_pad = (-seq) % GDN_CHUNK
    qkv, z, gb, glu = _in_proj(h, lw["norm_mix_w"], lw["wqkvz"], lw["wab"], lw["wglu"], lw["alog"], lw["dtb"])
    shp = lambda a: a.reshape(b, seq, a.shape[-1])
    qkv, z, gb, glu = shp(qkv), shp(z), shp(gb), shp(glu)
    o, s_new = _gdn(qkv, gb, z,
                    jnp.zeros((b, GDN_CONV - 1, QKV_WIDTH), F32),
                    jnp.zeros((b, GDN_HEADS, HEAD_DIM, HEAD_DIM), F32),
                    lw["conv_w"], lw["gdn_norm_w"], chunk=GDN_CHUNK, n_pad=n_pad, bb_count=1)
    h = _conf_out_seq(glu, o, shp(h), jnp.zeros((b, CONF_CONV - 1, CONF_WIDTH), F32),
                      lw["dw"], lw["dwb"], lw["lnw"], lw["lnb"], lw["wout"]).reshape(b * seq, D_MODEL)
    return h, s_new, qkv[:, seq - (GDN_CONV - 1):], glu[:, seq - (CONF_CONV - 1):]


def _mix_sample(h, lw, b, steps, s0, qkv_state, conf_state):
    n_pad = (-steps) % SUBLANES
    to_tb = lambda a: jnp.swapaxes(a, 0, 1)
    qkv, z, gb, glu = _in_proj(h, lw["norm_mix_w"], lw["wqkvz"], lw["wab"], lw["wglu"], lw["alog"], lw["dtb"])
    bt = lambda a: to_tb(a.reshape(steps, b, a.shape[-1]))
    qkv_bt = bt(qkv)
    o, s_new = _gdn(qkv_bt, bt(gb), bt(z), qkv_state, s0,
                    lw["conv_w"], lw["gdn_norm_w"], chunk=steps + n_pad, n_pad=n_pad, bb_count=8)
    cext = jnp.concatenate([to_tb(conf_state), glu.reshape(steps, b, CONF_WIDTH)], axis=0)
    h = _conf_out_step(cext, to_tb(o).reshape(steps * b, GDN_WIDTH), h,
                       lw["dw"], lw["dwb"], lw["lnw"], lw["lnb"], lw["wout"])
    qkv_ext = jnp.concatenate([qkv_state, qkv_bt], axis=1)
    return (h, s_new, qkv_ext[:, qkv_ext.shape[1] - (GDN_CONV - 1):],
            to_tb(cext[cext.shape[0] - (CONF_CONV - 1):]))


def kernel(x_prompt, x_sample, state_gdn, state_qkv_conv, state_conf_conv, meta_tokens, norm_mix_w, w_in,
           conv_qkv_w, a_log, dt_bias, gdn_norm_w, conf_dw_w, conf_dw_b, conf_ln_w, conf_ln_b, w_out,
           norm_ffn_w, w_query, sub_keys, expert_u, expert_v, final_norm_w):
    depth = w_in.shape[0]
    layers = [_prep_layer(l, norm_mix_w, w_in, conv_qkv_w, a_log, dt_bias, gdn_norm_w, conf_dw_w,
                          conf_dw_b, conf_ln_w, conf_ln_b, w_out, norm_ffn_w, w_query, sub_keys,
                          expert_u, expert_v) for l in range(depth)]
    fw = final_norm_w.reshape(1, D_MODEL)
    b_p, seq0, _ = x_prompt.shape
    seq = seq0 + N_META
    b_s, steps, _ = x_sample.shape
    n_p, n_s = b_p * seq, b_s * steps
    n_fill = (-(n_p + n_s)) % PEER_DENSE_TOKENS
    meta = jnp.broadcast_to(meta_tokens[None], (b_p, N_META, D_MODEL))
    h_p = jnp.concatenate([meta, x_prompt], axis=1).reshape(n_p, D_MODEL)
    h_s = jnp.swapaxes(x_sample, 0, 1).reshape(n_s, D_MODEL)
    outs_p, outs_s = [], []
    for li, lw in enumerate(layers):
        h_p, *st_p = _mix_prompt(h_p, lw, b_p, seq)
        h_s, *st_s = _mix_sample(h_s, lw, b_s, steps, state_gdn[li], state_qkv_conv[li], state_conf_conv[li])
        outs_p.append(st_p)
        outs_s.append(st_s)
        h = _peer(jnp.concatenate([h_p, h_s, jnp.zeros((n_fill, D_MODEL), F32)], axis=0), lw, fw,
                  li == depth - 1)
        h_p, h_s = h[:n_p], h[n_p:n_p + n_s]
    y_p = h_p.reshape(b_p, seq, D_MODEL)[:, N_META:]
    y_s = jnp.swapaxes(h_s.reshape(steps, b_s, D_MODEL), 0, 1)
    stack = lambda outs, i: jnp.stack([o[i] for o in outs])
    return (y_p, y_s, stack(outs_p, 0), stack(outs_p, 1), stack(outs_p, 2),
            stack(outs_s, 0), stack(outs_s, 1), stack(outs_s, 2))
```

```python
import functools
import math

import jax
import jax.numpy as jnp
from jax import lax
from jax.experimental import pallas as pl
from jax.experimental.pallas import tpu as pltpu

F32 = jnp.float32
BF16 = jnp.bfloat16

D_MODEL = 1024
N_META = 16
GDN_HEADS = 4
HEAD_DIM = 128
GDN_WIDTH = GDN_HEADS * HEAD_DIM
CONF_WIDTH = 512
QKV_WIDTH = 3 * GDN_WIDTH
GDN_CONV = 4
GDN_CHUNK = 64
GDN_GROUP = 8
CONF_CONV = 31
N_KEYS = 128
PEER_HEADS = 8
PEER_TOPK = 16
EPS = 1e-6

LANES = 128
SUBLANES = 8
VMEM_LIMIT_BYTES = 56 * 1024 * 1024


def _sigmoid(x):
    return 1.0 / (1.0 + jnp.exp(-x))


def _silu(x):
    return x * _sigmoid(x)


def _softplus(x):
    return jnp.maximum(x, 0.0) + jnp.log1p(jnp.exp(-jnp.abs(x)))


def _rmsnorm(x, w):
    return x * lax.rsqrt(jnp.mean(x * x, axis=-1, keepdims=True) + EPS) * w


def _dot(a, b, precision=None):
    return jnp.dot(a, b, preferred_element_type=F32, precision=precision)


def _dot_nt(a, b, precision=None):
    return lax.dot_general(a, b, (((1,), (1,)), ((), ())), preferred_element_type=F32,
                           precision=precision)


def _dot_tn(a, b, precision=None):
    return lax.dot_general(a, b, (((0,), (0,)), ((), ())), preferred_element_type=F32,
                           precision=precision)


def _split(a):
    hi = a.astype(BF16)
    return hi, a - hi.astype(F32)


def _dot3(a, b):
    a_hi, a_rest = _split(a)
    b_hi, b_rest = _split(b)
    return _dot(a_hi, b_hi) + (_dot(a_hi, b_rest.astype(BF16)) + _dot(a_rest.astype(BF16), b_hi))


def _dot_exact_lhs(a_bf16, b):
    b0, r = _split(b)
    b1, r = _split(r)
    return _dot(a_bf16, b0) + (_dot(a_bf16, b1) + _dot(a_bf16, r.astype(BF16)))


def _pick_tile(n, candidates):
    for c in candidates:
        if n % c == 0:
            return c
    return n


def _params(*semantics):
    return pltpu.CompilerParams(dimension_semantics=semantics, vmem_limit_bytes=VMEM_LIMIT_BYTES)


def _in_proj_kernel(h_ref, nw_ref, wqkvz_ref, wab_ref, wglu_ref, alog_ref, dtb_ref,
                    qkv_ref, z_ref, gb_ref, glu_ref):
    xb = _rmsnorm(h_ref[...], nw_ref[...]).astype(BF16)
    p = _dot(xb, wqkvz_ref[...])
    qkv_ref[...] = p[:, :QKV_WIDTH]
    z_ref[...] = p[:, QKV_WIDTH:]
    ab = _dot(xb, wab_ref[...])
    g = -jnp.exp(alog_ref[...]) * _softplus(ab + dtb_ref[...])
    lane = lax.broadcasted_iota(jnp.int32, ab.shape, 1)
    gb_ref[...] = jnp.where(lane < GDN_HEADS, g, _sigmoid(ab))
    pg = _dot(xb, wglu_ref[...])
    glu_ref[...] = pg[:, :CONF_WIDTH] * _sigmoid(pg[:, CONF_WIDTH:])


def _in_proj(h, nw, wqkvz, wab, wglu, alog, dtb, n_rows=None):
    n = h.shape[0] if n_rows is None else n_rows
    tm = _pick_tile(n, (384, 256, 128))
    row = lambda i: (i, 0)
    fixed = lambda i: (0, 0)
    widths = (QKV_WIDTH, GDN_WIDTH, LANES, CONF_WIDTH)
    return pl.pallas_call(
        _in_proj_kernel,
        grid=(n // tm,),
        in_specs=[
            pl.BlockSpec((tm, D_MODEL), row),
            pl.BlockSpec((1, D_MODEL), fixed),
            pl.BlockSpec(wqkvz.shape, fixed),
            pl.BlockSpec(wab.shape, fixed),
            pl.BlockSpec(wglu.shape, fixed),
            pl.BlockSpec((1, LANES), fixed),
            pl.BlockSpec((1, LANES), fixed),
        ],
        out_specs=[pl.BlockSpec((tm, w), row) for w in widths],
        out_shape=[jax.ShapeDtypeStruct((n, w), F32) for w in widths],
        compiler_params=_params("parallel"),
        name="in_proj",
    )(h, nw, wqkvz, wab, wglu, alog, dtb)


def _gdn_kernel(xq_ref, xk_ref, xv_ref, gb_ref, z_ref, cs_q_ref, cs_k_ref, cs_v_ref, s0_ref,
                cwq_ref, cwk_ref, cwv_ref, nw_ref,
                o_ref, s_ref,
                xq_s, xk_s, xv_s, g_s, b_s, u_s, att_s, wq_s, m_s, bm_s, gl_s, sall_s,
                *, chunk, n_pad, seq, bb_count):
    C = chunk
    n_chunks = (seq + n_pad) // C
    head = 8 + n_pad
    n_fac = int(math.log2(C))
    row_i = lax.broadcasted_iota(jnp.int32, (C, C), 0)
    col_i = lax.broadcasted_iota(jnp.int32, (C, C), 1)
    incl = row_i >= col_i
    strict = row_i > col_i
    tril_b = jnp.where(incl, 1.0, 0.0).astype(BF16)
    src_lane = lax.broadcasted_iota(jnp.int32, (LANES, HEAD_DIM), 0)
    pick_g = jnp.where(src_lane == pl.program_id(1), 1.0, 0.0).astype(BF16)
    pick_b = jnp.where(src_lane == pl.program_id(1) + GDN_HEADS, 1.0, 0.0).astype(BF16)
    eye =jnp.where(row_i == col_i, 1.0, 0.0).astype(F32)
    group = GDN_GROUP
    live = lax.broadcasted_iota(jnp.int32, (C, HEAD_DIM), 0) >= n_pad
    cws = (cwq_ref[...], cwk_ref[...], cwv_ref[...])
    nw = nw_ref[...]

    def conv(blk, cw):
        acc = blk[5:5 + C] * cw[0:1]
        for j in range(1, GDN_CONV):
            acc = acc + blk[5 + j:5 + j + C] * cw[j:j + 1]
        return _silu(acc)

    slot = lambda bb, c: bb * n_chunks + c

    is_first = lambda c: isinstance(c, int) and c == 0

    def build(items, fill=()):
        fill = list(fill)
        tick = lambda: fill.pop(0)() if fill else None
        each = lambda f, *ls: [f(*a) for a in zip(*ls)]
        at = [(bb, c * C if isinstance(c, int) else pl.multiple_of(c * C, SUBLANES)) for bb, c in items]
        qc = [conv(xq_s[bb, pl.ds(t0, C + 8), :], cws[0]) for bb, t0 in at]
        kc = [conv(xk_s[bb, pl.ds(t0, C + 8), :], cws[1]) for bb, t0 in at]
        v = [conv(xv_s[bb, pl.ds(t0, C + 8), :], cws[2]) for bb, t0 in at]
        q = each(lambda a: a * lax.rsqrt(jnp.sum(a * a, axis=-1, keepdims=True) + EPS) * (HEAD_DIM ** -0.5), qc)
        k = each(lambda a: a * lax.rsqrt(jnp.sum(a * a, axis=-1, keepdims=True) + EPS), kc)
        g = [g_s[bb, pl.ds(t0, C), :] for bb, t0 in at]
        beta = [b_s[bb, pl.ds(t0, C), :] for bb, t0 in at]
        if n_pad:
            q, k, v = ([jnp.where(live, a, 0.0) if is_first(c) else a for a, (_, c) in zip(l, items)]
                       for l in (q, k, v))
        cum = each(lambda a: _dot_exact_lhs(tril_b, jnp.concatenate([a, jnp.where(strict, a[:, :C], 0.0)], axis=1)), g)
        tick()
        gcum = [a[:, :HEAD_DIM] for a in cum]
        decay = [jnp.where(incl, jnp.exp(a[:, HEAD_DIM:]), 0.0) for a in cum]
        eg = [jnp.exp(a) for a in gcum]
        kb = each(lambda a, b_: a * b_, k, beta)
        kq = each(lambda a, b_, c_: _dot_nt(jnp.concatenate([a, b_], axis=0).astype(BF16), c_.astype(BF16)), kb, q, k)
        tick()
        p = each(lambda a, d: jnp.where(strict, -(a[:C] * d), 0.0), kq, decay)
        att = each(lambda a, d: jnp.where(incl, a[C:] * d, 0.0), kq, decay)
        t = [eye + a for a in p]
        p = each(lambda a: _dot3(a, a), p)
        tick()
        for step in range(1, n_fac):
            tick()
            if step == n_fac - 1:
                t = each(lambda a, b_: a + _dot3(b_, a), t, p)
            else:
                y = each(lambda a, b_: _dot3(b_, jnp.concatenate([a, b_], axis=1)), t, p)
                t = each(lambda a, b_: a + b_[:, :C], t, y)
                p = [a[:, C:] for a in y]
        x = each(lambda t_, v_, b_, kb_, eg_: _dot3(t_, jnp.concatenate([v_ * b_, kb_ * eg_], axis=1)),
                 t, v, beta, kb, eg)
        while fill:
            tick()
        g_last = [a[C - 1:C, :] for a in gcum]
        kt = each(lambda k_, gl, gc: k_ * jnp.exp(gl - gc), k, g_last, gcum)
        ktx = each(lambda a, b_: _dot_tn(a.astype(BF16), b_.astype(BF16)), kt, x)
        for i, (bb, c) in enumerate(items):
            n = slot(bb, c)
            u_s[n] = x[i][:, :HEAD_DIM]
            att_s[n] = att[i].astype(BF16)
            wq_s[n] = jnp.concatenate([x[i][:, HEAD_DIM:], q[i] * eg[i]], axis=0).astype(BF16)
            bm_s[n] = ktx[i][:, :HEAD_DIM]
            m_s[n] = (-ktx[i][:, HEAD_DIM:]).astype(BF16)
            gl_s[n] = jnp.exp(g_last[i])

    def carry(items):
        s = [s_ref[bb, 0] for bb, _ in items]
        for (bb, c), s_i in zip(items, s):
            sall_s[slot(bb, c)] = s_i
        new = [gl_s[slot(bb, c)] * s_i + _dot(m_s[slot(bb, c)], s_i.astype(BF16)) + bm_s[slot(bb, c)]
               for (bb, c), s_i in zip(items, s)]
        for (bb, _), s_i in zip(items, new):
            s_ref[bb, 0] = s_i

    def emit(items):
        ns = [slot(bb, c) for bb, c in items]
        ws_qs = [_dot(wq_s[n], sall_s[n].astype(BF16)) for n in ns]
        v_new = [u_s[n] - a[:C] for n, a in zip(ns, ws_qs)]
        o = [a[C:] + _dot(att_s[n], vn.astype(BF16)) for n, a, vn in zip(ns, ws_qs, v_new)]
        o = [_rmsnorm(a, nw) for a in o]
        for (bb, c), a in zip(items, o):
            if is_first(c):
                rows = C - n_pad
                o_ref[bb, 0:rows, :] = (a[n_pad:] * _silu(z_ref[bb, 0:rows, :])).astype(BF16)
            else:
                r0 = c * C - n_pad if isinstance(c, int) else pl.multiple_of(c * C - n_pad, 2 * SUBLANES)
                o_ref[bb, pl.ds(r0, C), :] = (a * _silu(z_ref[bb, pl.ds(r0, C), :])).astype(BF16)

    lead = (n_chunks - 1) % group + 1
    if lead == 1 and n_chunks > 1:
        lead += group
    leading = [(bb, c) for bb in range(bb_count) for c in range(lead)]

    n_loop = (n_chunks - lead) // group
    later = lambda i: [(0, lead + i * group + j) for j in range(group)]
    carry_range = lambda lo, hi: lax.fori_loop(lo, hi, lambda c, _: (carry([(0, c)]), 0)[1], 0)

    for bb in range(bb_count):
        for x_s, x_ref, cs_ref in ((xq_s, xq_ref, cs_q_ref), (xk_s, xk_ref, cs_k_ref),
                                   (xv_s, xv_ref, cs_v_ref)):
            x_s[bb, 0:head, :] = jnp.zeros((head, HEAD_DIM), F32)
            x_s[bb, head - (GDN_CONV - 1):head, :] = cs_ref[bb]
            x_s[bb, head:head + seq, :] = x_ref[bb]
        if n_pad:
            g_s[bb, 0:n_pad, :] = jnp.zeros((n_pad, HEAD_DIM), F32)
            b_s[bb, 0:n_pad, :] = jnp.zeros((n_pad, HEAD_DIM), F32)
        pieces = []
        rest = gb_ref[bb]
        for _ in range(3):
            hi, rest = _split(rest)
            pieces.append(hi)
        spread = lambda sel: _dot(pieces[0], sel) + (_dot(pieces[1], sel) + _dot(pieces[2], sel))
        g_s[bb, n_pad:n_pad + seq, :] = spread(pick_g)
        b_s[bb, n_pad:n_pad + seq, :] = spread(pick_b)
        s_ref[bb, 0] = s0_ref[bb, 0]

    build(leading)
    carry([(bb, 0) for bb in range(bb_count)])

    def build_later(i, _):
        build(later(i), [functools.partial(carry, [(0, 1 + i * group + j)]) for j in range(group)])
        return 0

    if n_loop:
        lax.fori_loop(0, n_loop, build_later, 0)
    if n_chunks > 1:
        carry_range(1 + n_loop * group, n_chunks)
    emit(leading)
    if n_loop:
        lax.fori_loop(0, n_loop, lambda i, _: (emit(later(i)), 0)[1], 0)


def _gdn(qkv, gb, z, conv_state, s0, conv_w, norm_w, *, chunk, n_pad, bb_count):
    b, seq, _ = qkv.shape
    n_chunks = (seq + n_pad) // chunk
    assert n_chunks * chunk == seq + n_pad and b % bb_count == 0
    assert n_chunks == 1 or bb_count == 1
    col = lambda off: (lambda i, h: (i, 0, off + h))
    wcol = lambda off: (lambda i, h: (0, off + h))
    seq_blk = (bb_count, seq, HEAD_DIM)
    cs_blk = (bb_count, GDN_CONV - 1, HEAD_DIM)
    s_blk = (bb_count, 1, HEAD_DIM, HEAD_DIM)
    s_map = lambda i, h: (i, h, 0, 0)
    ext = chunk * n_chunks
    per_chunk = lambda r, c, dt=F32: pltpu.VMEM((bb_count * n_chunks, r, c), dt)
    kern = functools.partial(_gdn_kernel, chunk=chunk, n_pad=n_pad, seq=seq, bb_count=bb_count)
    return pl.pallas_call(
        kern,
        grid=(b // bb_count, GDN_HEADS),
        in_specs=[
            pl.BlockSpec(seq_blk, col(0)),
            pl.BlockSpec(seq_blk, col(GDN_HEADS)),
            pl.BlockSpec(seq_blk, col(2 * GDN_HEADS)),
            pl.BlockSpec((bb_count, seq, LANES), lambda i, h: (i, 0, 0)),
            pl.BlockSpec(seq_blk, col(0)),
            pl.BlockSpec(cs_blk, col(0)),
            pl.BlockSpec(cs_blk, col(GDN_HEADS)),
            pl.BlockSpec(cs_blk, col(2 * GDN_HEADS)),
            pl.BlockSpec(s_blk, s_map),
            pl.BlockSpec((GDN_CONV, HEAD_DIM), wcol(0)),
            pl.BlockSpec((GDN_CONV, HEAD_DIM), wcol(GDN_HEADS)),
            pl.BlockSpec((GDN_CONV, HEAD_DIM), wcol(2 * GDN_HEADS)),
            pl.BlockSpec((1, HEAD_DIM), lambda i, h: (0, 0)),
        ],
        out_specs=[pl.BlockSpec(seq_blk, col(0)), pl.BlockSpec(s_blk, s_map)],
        out_shape=[jax.ShapeDtypeStruct((b, seq, GDN_WIDTH), BF16),
                   jax.ShapeDtypeStruct((b, GDN_HEADS, HEAD_DIM, HEAD_DIM), F32)],
        scratch_shapes=[
            pltpu.VMEM((bb_count, ext + 8, HEAD_DIM), F32),
            pltpu.VMEM((bb_count, ext + 8, HEAD_DIM), F32),
            pltpu.VMEM((bb_count, ext + 8, HEAD_DIM), F32),
            pltpu.VMEM((bb_count, ext, HEAD_DIM), F32),
            pltpu.VMEM((bb_count, ext, HEAD_DIM), F32),
            per_chunk(chunk, HEAD_DIM),
            per_chunk(chunk, chunk, BF16),
            per_chunk(2 * chunk, HEAD_DIM, BF16),
            per_chunk(HEAD_DIM, HEAD_DIM, BF16),
            per_chunk(HEAD_DIM, HEAD_DIM),
            per_chunk(1, HEAD_DIM),
            per_chunk(HEAD_DIM, HEAD_DIM),
        ],
        compiler_params=_params("parallel", "parallel"),
        name="gdn",
    )(qkv, qkv, qkv, gb, z, conv_state, conv_state, conv_state, s0,
      conv_w, conv_w, conv_w, norm_w)


CONF_HALO = 32
CONF_ROWS = 16


def _conf_out_seq_kernel(glu_ref, o_ref, h_ref, cstate_ref, dw_ref, dwb_ref, lnw_ref, lnb_ref, wout_ref,
                         out_ref, ext_s, c_s, sh_s, *, tt):
    t = pl.program_id(1)
    lead = CONF_HALO - (CONF_CONV - 1)

    @pl.when(t == 0)
    def _():
        ext_s[0:lead, :] = jnp.zeros((lead, CONF_WIDTH), F32)
        ext_s[lead:CONF_HALO, :] = cstate_ref[0]

    @pl.when(t > 0)
    def _():
        ext_s[0:CONF_HALO, :] = ext_s[tt:tt + CONF_HALO, :]

    ext_s[CONF_HALO:CONF_HALO + tt, :] = glu_ref[0]
    dw = dw_ref[...]
    bias = dwb_ref[...]

    def rows(i, _):
        r0 = pl.multiple_of(i * CONF_ROWS, SUBLANES)
        blk = ext_s[pl.ds(r0, CONF_ROWS + CONF_HALO), :]
        span = CONF_ROWS + CONF_HALO - SUBLANES
        for s in range(1, SUBLANES):
            sh_s[s - 1] = blk[s:s + span]
        acc = bias
        for j in range(CONF_CONV):
            q, s = divmod(lead + j, SUBLANES)
            rows_j = slice(q * SUBLANES, q * SUBLANES + CONF_ROWS)
            acc = acc + (blk[rows_j] if s == 0 else sh_s[s - 1, rows_j, :]) * dw[j:j + 1]
        c_s[pl.ds(r0, CONF_ROWS), :] = acc
        return 0

    lax.fori_loop(0, tt // CONF_ROWS, rows, 0)
    c = c_s[...]
    mu = jnp.mean(c, axis=-1, keepdims=True)
    xc = c - mu
    var = jnp.mean(xc * xc, axis=-1, keepdims=True)
    c = _silu(xc * lax.rsqrt(var + EPS) * lnw_ref[...] + lnb_ref[...])
    mixed = jnp.concatenate([o_ref[0], c.astype(BF16)], axis=1)
    out_ref[...] = h_ref[...] + _dot(mixed, wout_ref[...])


def _conf_out_seq(glu, o, h, cstate, dw, dwb, lnw, lnb, wout):
    b, seq, _ = glu.shape
    tt = _pick_tile(seq, (688, 512, 256, 128, 64, 16))
    per_seq = seq // tt
    tile = lambda w: pl.BlockSpec((1, tt, w), lambda i, t: (i, t, 0))
    flat = pl.BlockSpec((tt, D_MODEL), lambda i, t: (i * per_seq + t, 0))
    fixed = lambda a: pl.BlockSpec(a.shape, lambda i, t: (0,) * a.ndim)
    return pl.pallas_call(
        functools.partial(_conf_out_seq_kernel, tt=tt),
        grid=(b, per_seq),
        in_specs=[tile(CONF_WIDTH), tile(GDN_WIDTH), flat,
                  pl.BlockSpec((1, CONF_CONV - 1, CONF_WIDTH), lambda i, t: (i, 0, 0)),
                  fixed(dw), fixed(dwb), fixed(lnw), fixed(lnb), fixed(wout)],
        out_specs=flat,
        out_shape=jax.ShapeDtypeStruct(h.shape, F32),
        input_output_aliases={2: 0},
        scratch_shapes=[pltpu.VMEM((tt + CONF_HALO, CONF_WIDTH), F32),
                        pltpu.VMEM((tt, CONF_WIDTH), F32),
                        pltpu.VMEM((SUBLANES - 1, CONF_ROWS + CONF_HALO - SUBLANES, CONF_WIDTH), F32)],
        compiler_params=_params("parallel", "arbitrary"),
        name="conf_out_seq",
    )(glu, o, h, cstate, dw, dwb, lnw, lnb, wout)


def _conf_out_step_kernel(cext_ref, o_ref, h_ref, dw_ref, dwb_ref, lnw_ref, lnb_ref, wout_ref, out_ref,
                          *, steps):
    dw = dw_ref[...]
    outs = []
    for t in range(steps):
        acc = dwb_ref[...] + cext_ref[t] * dw[0:1]
        for j in range(1, CONF_CONV):
            acc = acc + cext_ref[t + j] * dw[j:j + 1]
        outs.append(acc)
    c = jnp.concatenate(outs, axis=0)
    mu = jnp.mean(c, axis=-1, keepdims=True)
    xc = c - mu
    var = jnp.mean(xc * xc, axis=-1, keepdims=True)
    c = _silu(xc * lax.rsqrt(var + EPS) * lnw_ref[...] + lnb_ref[...])
    mixed = jnp.concatenate([o_ref[...], c.astype(BF16)], axis=1)
    out_ref[...] = h_ref[...] + _dot(mixed, wout_ref[...])


def _conf_out_step(cext, o, h, dw, dwb, lnw, lnb, wout):
    steps = cext.shape[0] - (CONF_CONV - 1)
    n = h.shape[0]
    full = lambda a: pl.BlockSpec(a.shape, lambda i: (0,) * a.ndim)
    args = (cext, o, h, dw, dwb, lnw, lnb, wout)
    return pl.pallas_call(
        functools.partial(_conf_out_step_kernel, steps=steps),
        grid=(1,),
        in_specs=[full(a) for a in args],
        out_specs=pl.BlockSpec((n, D_MODEL), lambda i: (0, 0)),
        out_shape=jax.ShapeDtypeStruct((n, D_MODEL), F32),
        compiler_params=_params("arbitrary"),
        name="conf_out_step",
    )(*args)


N_RANKS = PEER_TOPK + 1
RANK_ROWS = 24
NEG_INF = float("-inf")


def _top_rows(s, count):
    rows = []
    for r in range(count):
        m = jnp.max(s, axis=0, keepdims=True)
        rows.append(m)
        s = jnp.where(s == m, NEG_INF, s)
    return rows


TOP_WAYS = 4


def _top_rows_chained(s, count):
    n = s.shape[0] // TOP_WAYS
    lv = [s[i * n:(i + 1) * n] for i in range(TOP_WAYS)]
    for i, j in ((0, 1), (2, 3), (0, 2), (1, 3), (1, 2)):
        lv[i], lv[j] = jnp.maximum(lv[i], lv[j]), jnp.minimum(lv[i], lv[j])
    rows = []
    for r in range(count):
        m = jnp.max(lv[0], axis=0, keepdims=True)
        rows.append(m)
        hit = lv[0] == m
        for i in range(TOP_WAYS - 1):
            lv[i] = jnp.where(hit, lv[i + 1], lv[i])
        lv[-1] = jnp.where(hit, NEG_INF, lv[-1])
    return rows


def _prefix_count(hits):
    out = jnp.zeros(hits[0].shape, F32)
    for r, hit in enumerate(hits):
        out = jnp.where(hit, float(r + 1), out)
    return out


def _stack_rows(rows, height, tm):
    idx = lax.broadcasted_iota(jnp.int32, (height, tm), 0)
    out = jnp.full((height, tm), NEG_INF, F32)
    for r, row in enumerate(rows):
        out = jnp.where(idx == r, row, out)
    return out


def _peer_score_kernel(h_ref, nw_ref, wqt_ref, keys_ref, xnt_ref, cnt_ref, e1_ref, rank2_ref, e2_ref):
    tm = h_ref.shape[0]
    xn = _rmsnorm(h_ref[...], nw_ref[...])
    xnt = xn.T.astype(BF16)
    xnt_ref[...] = xnt
    qt = _dot(wqt_ref[...], xnt).astype(BF16)
    idx8 = lax.broadcasted_iota(jnp.int32, (SUBLANES, tm), 0)
    for hh in range(PEER_HEADS):
        s = []
        for p in range(2):
            r0 = (hh * 2 + p) * N_KEYS
            s.append(_dot(keys_ref[hh, p], qt[r0:r0 + N_KEYS, :]))
        a = _top_rows_chained(s[0], N_RANKS)
        b = _top_rows_chained(s[1], N_RANKS)
        a_st = _stack_rows(a, RANK_ROWS, tm)
        b_st = _stack_rows(b, RANK_ROWS, tm)
        cands = [a[0] + b_st,
                 jnp.where(lax.broadcasted_iota(jnp.int32, (RANK_ROWS, tm), 0) >= 1,
                           a_st + b[0], NEG_INF)]
        for r1 in range(1, N_RANKS):
            hi = N_RANKS // (r1 + 1) - 1
            if hi >= 1 and r1 <= 4:
                cands.append(jnp.where((idx8 >= 1) & (idx8 <= hi), a[r1] + b_st[0:SUBLANES], NEG_INF))
        cands.append(jnp.where((idx8 >= 5) & (idx8 <= N_RANKS // 2 - 1),
                               a_st[0:SUBLANES] + b[1], NEG_INF))
        cand = jnp.concatenate(cands, axis=0)
        top = _top_rows(cand, N_RANKS)
        z = jnp.zeros_like(top[0])
        for r in range(PEER_TOPK):
            z = z + jnp.exp(top[r] - top[0])
        thr = 0.5 * (top[PEER_TOPK - 1] + top[PEER_TOPK])
        cnt1 = _prefix_count([s[0] >= thr - b[r2] for r2 in range(PEER_TOPK)])
        rank2 = _prefix_count([s[1] < b[r] for r in range(PEER_TOPK)])
        cnt_ref[hh] = cnt1
        e1_ref[hh] = 0.5 * jnp.exp(s[0] - a[0]) / z
        rank2_ref[hh] = rank2.astype(BF16)
        e2_ref[hh] = jnp.exp(s[1] - b[0]).astype(BF16)


def _peer_score(h, nw, wqt, keys):
    n = h.shape[0]
    tm = _pick_tile(n, (PEER_TOKENS, 256, 128))
    heads = lambda: pl.BlockSpec((PEER_HEADS, N_KEYS, tm), lambda i: (0, 0, i))
    return pl.pallas_call(
        _peer_score_kernel,
        grid=(n // tm,),
        in_specs=[pl.BlockSpec((tm, D_MODEL), lambda i: (i, 0)),
                  pl.BlockSpec((1, D_MODEL), lambda i: (0, 0)),
                  pl.BlockSpec(wqt.shape, lambda i: (0, 0)),
                  pl.BlockSpec(keys.shape, lambda i: (0, 0, 0, 0))],
        out_specs=[pl.BlockSpec((D_MODEL, tm), lambda i: (0, i)), heads(), heads(), heads(), heads()],
        out_shape=[jax.ShapeDtypeStruct((D_MODEL, n), BF16)]
        + [jax.ShapeDtypeStruct((PEER_HEADS, N_KEYS, n), dt) for dt in (F32, F32, BF16, BF16)],
        compiler_params=_params("parallel"),
        name="peer_score",
    )(h, nw, wqt, keys)


EXPERT_CHUNK = 1024
PEER_TOKENS = 512
PEER_DENSE_TOKENS = 512
SQRT_HALF = math.sqrt(0.5)


def _peer_dense_kernel(xnt_ref, cnt_ref, e1_ref, rank2_ref, e2_ref, u_ref, vt_ref, h_ref, fw_ref,
                       out_ref, acc_s, w_s, gate_s, cnt_s, e1_s, *, final_norm, n_chunks):
    s = pl.program_id(0)
    c = lax.rem(s, n_chunks)
    cur = lax.rem(c, 2)
    tm = xnt_ref.shape[1]
    half = tm // 2

    @pl.when(s == 0)
    def _():
        acc_s[...] = jnp.zeros_like(acc_s)
        w_s[1] = jnp.zeros(w_s.shape[1:], BF16)

    prev = 1 - cur
    n_blocks = EXPERT_CHUNK // N_KEYS
    n_lt = tm // LANES
    per_lt = PEER_HEADS * n_blocks
    for l in range(n_lt):
        cnt_s[l * per_lt:(l + 1) * per_lt, :] = cnt_ref[:, :, l * LANES:(l + 1) * LANES].reshape(per_lt, LANES)
        e1_s[l * per_lt:(l + 1) * per_lt, :] = e1_ref[:, :, l * LANES:(l + 1) * LANES].reshape(per_lt, LANES)

    def rows(ref, hh, j):
        words = jnp.concatenate([jnp.broadcast_to(ref[pl.ds(l * per_lt + hh * n_blocks + j, 1), :], (SUBLANES, LANES))
                                 for l in range(n_lt)], axis=1)
        packed = jnp.concatenate([words, words], axis=0).astype(BF16)
        return jnp.concatenate([packed] * (N_KEYS // (2 * SUBLANES)), axis=0)

    xnt = xnt_ref[...]
    block = lambda j: slice(j * N_KEYS, (j + 1) * N_KEYS)

    def gate(j):
        total = None
        for hh in range(PEER_HEADS):
            e2 = e2_ref[hh]
            sel = jnp.where(rank2_ref[hh] < rows(cnt_s, hh, j), e2, jnp.zeros_like(e2))
            term = rows(e1_s, hh, j) * sel
            total = term if total is None else total + term
        gate_s[block(j), :] = total

    def project(j):
        return _dot(u_ref[block(j), :], xnt)

    def finish(j, pre):
        act = (pre * (1.0 + lax.erf(pre * SQRT_HALF))).astype(BF16)
        w_s[cur, block(j), :] = act * gate_s[block(j), :]

    ahead = 3
    pres = {j: project(j) for j in range(ahead)}
    gate(0)
    for j in range(n_blocks):
        if j + ahead < n_blocks:
            pres[j + ahead] = project(j + ahead)
        if j == 0:
            acc_s[:, :half] += _dot(vt_ref[0], w_s[prev, :, :half])
        if j == n_blocks // 2:
            acc_s[:, half:] += _dot(vt_ref[0], w_s[prev, :, half:])
        if j + 1 < n_blocks:
            gate(j + 1)
        finish(j, pres.pop(j))

    @pl.when(c == 0)
    def _():
        y = h_ref[...] + acc_s[...].T
        if final_norm:
            y = _rmsnorm(y, fw_ref[...])
        out_ref[...] = y
        acc_s[...] = jnp.zeros_like(acc_s)


def _peer_dense(xnt, cnt, e1, rank2, e2, u, vt, h, fw, *, final_norm):
    n = h.shape[0]
    tm = _pick_tile(n, (PEER_DENSE_TOKENS, PEER_TOKENS, 256, 128))
    n_tiles = n // tm
    n_chunks = u.shape[0] // EXPERT_CHUNK
    assert n_chunks % 2 == 0
    rows_per_chunk = EXPERT_CHUNK // N_KEYS
    tile = lambda s: jnp.minimum(s // n_chunks, n_tiles - 1)
    chunk = lambda s: lax.rem(s, n_chunks)
    done = lambda s: jnp.maximum(s - 1, 0) // n_chunks
    by_chunk = lambda: pl.BlockSpec((PEER_HEADS, rows_per_chunk, tm), lambda s: (0, chunk(s), tile(s)))
    by_tile = lambda: pl.BlockSpec((PEER_HEADS, N_KEYS, tm), lambda s: (0, 0, tile(s)))
    return pl.pallas_call(
        functools.partial(_peer_dense_kernel, final_norm=final_norm, n_chunks=n_chunks),
        grid=(n_tiles * n_chunks + 1,),
        in_specs=[pl.BlockSpec((D_MODEL, tm), lambda s: (0, tile(s))),
                  by_chunk(), by_chunk(), by_tile(), by_tile(),
                  pl.BlockSpec((EXPERT_CHUNK, D_MODEL), lambda s: (chunk(s), 0)),
                  pl.BlockSpec((1, D_MODEL, EXPERT_CHUNK), lambda s: (chunk(s + n_chunks - 1), 0, 0)),
                  pl.BlockSpec((tm, D_MODEL), lambda s: (done(s), 0)),
                  pl.BlockSpec((1, D_MODEL), lambda s: (0, 0))],
        out_specs=pl.BlockSpec((tm, D_MODEL), lambda s: (done(s), 0)),
        out_shape=jax.ShapeDtypeStruct((n, D_MODEL), F32),
        scratch_shapes=[pltpu.VMEM((D_MODEL, tm), F32), pltpu.VMEM((2, EXPERT_CHUNK, tm), BF16),
                        pltpu.VMEM((EXPERT_CHUNK, tm), BF16)]
        + [pltpu.VMEM((PEER_HEADS * rows_per_chunk * tm // LANES, LANES), F32)] * 2,
        compiler_params=_params("arbitrary"),
        name="peer_dense",
    )(xnt, cnt, e1, rank2, e2, u, vt, h, fw)


def _prep_layer(layer, norm_mix_w, w_in, conv_qkv_w, a_log, dt_bias, gdn_norm_w, conf_dw_w, conf_dw_b,
                conf_ln_w, conf_ln_b, w_out, norm_ffn_w, w_query, sub_keys, expert_u, expert_v):
    s0 = QKV_WIDTH
    s1 = s0 + GDN_WIDTH
    s2 = s1 + GDN_HEADS
    s3 = s2 + GDN_HEADS
    w = w_in[layer]
    row = lambda v: v.reshape(1, -1)
    return dict(
        norm_mix_w=row(norm_mix_w[layer]),
        wqkvz=w[:, :s1].astype(BF16),
        wab=jnp.pad(w[:, s1:s3], ((0, 0), (0, LANES - 2 * GDN_HEADS))).astype(BF16),
        wglu=w[:, s3:].astype(BF16),
        alog=row(jnp.pad(a_log[layer], (0, LANES - GDN_HEADS))),
        dtb=row(jnp.pad(dt_bias[layer], (0, LANES - GDN_HEADS))),
        conv_w=conv_qkv_w[layer],
        gdn_norm_w=row(gdn_norm_w[layer]),
        dw=conf_dw_w[layer], dwb=row(conf_dw_b[layer]),
        lnw=row(conf_ln_w[layer]), lnb=row(conf_ln_b[layer]),
        wout=w_out[layer].astype(BF16),
        norm_ffn_w=row(norm_ffn_w[layer]),
        wqt=w_query[layer].T.astype(BF16),
        keys=sub_keys[layer].astype(BF16),
        u=expert_u[layer].astype(BF16),
        vt=jnp.swapaxes(expert_v[layer].astype(BF16).reshape(-1, EXPERT_CHUNK, D_MODEL), 1, 2),
    )


def _peer(h, lw, fw, final_norm):
    xnt, cnt, e1, rank2, e2 = _peer_score(h, lw["norm_ffn_w"], lw["wqt"], lw["keys"])
    return _peer_dense(xnt, cnt, e1, rank2, e2, lw["u"], lw["vt"], h, fw, final_norm=final_norm)


def _mix_prompt(h, lw, b, seq):
    n_pad = (-seq) % GDN_CHUNK
    qkv, z, gb, glu = _in_proj(h, lw["norm_mix_w"], lw["wqkvz"], lw["wab"], lw["wglu"], lw["alog"], lw["dtb"],
                               n_rows=b * seq)
    shp = lambda a: a.reshape(b, seq, a.shape[-1])
    qkv, z, gb, glu = shp(qkv), shp(z), shp(gb), shp(glu)
    o, s_new = _gdn(qkv, gb, z,
                    jnp.zeros((b, GDN_CONV - 1, QKV_WIDTH), F32),
                    jnp.zeros((b, GDN_HEADS, HEAD_DIM, HEAD_DIM), F32),
                    lw["conv_w"], lw["gdn_norm_w"], chunk=GDN_CHUNK, n_pad=n_pad, bb_count=1)
    h = _conf_out_seq(glu, o, h, jnp.zeros((b, CONF_CONV - 1, CONF_WIDTH), F32),
                      lw["dw"], lw["dwb"], lw["lnw"], lw["lnb"], lw["wout"])
    return h, s_new, qkv[:, seq - (GDN_CONV - 1):], glu[:, seq - (CONF_CONV - 1):]


def _mix_sample(h, lw, b, steps, s0, qkv_state, conf_state):
    n_pad = (-steps) % SUBLANES
    to_tb = lambda a: jnp.swapaxes(a, 0, 1)
    qkv, z, gb, glu = _in_proj(h, lw["norm_mix_w"], lw["wqkvz"], lw["wab"], lw["wglu"], lw["alog"], lw["dtb"])
    bt = lambda a: to_tb(a.reshape(steps, b, a.shape[-1]))
    qkv_bt = bt(qkv)
    o, s_new = _gdn(qkv_bt, bt(gb), bt(z), qkv_state, s0,
                    lw["conv_w"], lw["gdn_norm_w"], chunk=steps + n_pad, n_pad=n_pad, bb_count=8)
    cext = jnp.concatenate([to_tb(conf_state), glu.reshape(steps, b, CONF_WIDTH)], axis=0)
    h = _conf_out_step(cext, to_tb(o).reshape(steps * b, GDN_WIDTH), h,
                       lw["dw"], lw["dwb"], lw["lnw"], lw["lnb"], lw["wout"])
    qkv_ext = jnp.concatenate([qkv_state, qkv_bt], axis=1)
    return (h, s_new, qkv_ext[:, qkv_ext.shape[1] - (GDN_CONV - 1):],
            to_tb(cext[cext.shape[0] - (CONF_CONV - 1):]))


def kernel(x_prompt, x_sample, state_gdn, state_qkv_conv, state_conf_conv, meta_tokens, norm_mix_w, w_in,
           conv_qkv_w, a_log, dt_bias, gdn_norm_w, conf_dw_w, conf_dw_b, conf_ln_w, conf_ln_b, w_out,
           norm_ffn_w, w_query, sub_keys, expert_u, expert_v, final_norm_w):
    depth = w_in.shape[0]
    layers = [_prep_layer(l, norm_mix_w, w_in, conv_qkv_w, a_log, dt_bias, gdn_norm_w, conf_dw_w,
                          conf_dw_b, conf_ln_w, conf_ln_b, w_out, norm_ffn_w, w_query, sub_keys,
                          expert_u, expert_v) for l in range(depth)]
    fw = final_norm_w.reshape(1, D_MODEL)
    b_p, seq0, _ = x_prompt.shape
    seq = seq0 + N_META
    b_s, steps, _ = x_sample.shape
    n_p, n_s = b_p * seq, b_s * steps
    n_fill = (-(n_p + n_s)) % PEER_DENSE_TOKENS
    meta = jnp.broadcast_to(meta_tokens[None], (b_p, N_META, D_MODEL))
    tail = jnp.zeros((n_fill, D_MODEL), F32)
    h = jnp.concatenate([jnp.concatenate([meta, x_prompt], axis=1).reshape(n_p, D_MODEL),
                         jnp.swapaxes(x_sample, 0, 1).reshape(n_s, D_MODEL), tail], axis=0)
    outs_p, outs_s = [], []
    for li, lw in enumerate(layers):
        h_s, *st_s = _mix_sample(h[n_p:n_p + n_s], lw, b_s, steps, state_gdn[li], state_qkv_conv[li],
                                 state_conf_conv[li])
        h, *st_p = _mix_prompt(h, lw, b_p, seq)
        h = lax.dynamic_update_slice(h, jnp.concatenate([h_s, tail], axis=0), (n_p, 0))
        outs_p.append(st_p)
        outs_s.append(st_s)
        h = _peer(h, lw, fw, li == depth - 1)
    y_p = h[:n_p].reshape(b_p, seq, D_MODEL)[:, N_META:]
    y_s = jnp.swapaxes(h[n_p:n_p + n_s].reshape(steps, b_s, D_MODEL), 0, 1)
    stack = lambda outs, i: jnp.stack([o[i] for o in outs])
    return (y_p, y_s, stack(outs_p, 0), stack(outs_p, 1), stack(outs_p, 2),
            stack(outs_s, 0), stack(outs_s, 1), stack(outs_s, 2))
```

```python
import functools
import math

import jax
import jax.numpy as jnp
from jax import lax
from jax.experimental import pallas as pl
from jax.experimental.pallas import tpu as pltpu

F32 = jnp.float32
BF16 = jnp.bfloat16

D_MODEL = 1024
N_META = 16
GDN_HEADS = 4
HEAD_DIM = 128
GDN_WIDTH = GDN_HEADS * HEAD_DIM
CONF_WIDTH = 512
QKV_WIDTH = 3 * GDN_WIDTH
GDN_CONV = 4
GDN_CHUNK = 64
GDN_GROUP = 8
CONF_CONV = 31
N_KEYS = 128
PEER_HEADS = 8
PEER_TOPK = 16
EPS = 1e-6

LANES = 128
SUBLANES = 8
VMEM_LIMIT_BYTES = 56 * 1024 * 1024


def _sigmoid(x):
    return 1.0 / (1.0 + jnp.exp(-x))


def _silu(x):
    return x * _sigmoid(x)


def _softplus(x):
    return jnp.maximum(x, 0.0) + jnp.log1p(jnp.exp(-jnp.abs(x)))


def _rmsnorm(x, w):
    return x * lax.rsqrt(jnp.mean(x * x, axis=-1, keepdims=True) + EPS) * w


def _dot(a, b, precision=None):
    return jnp.dot(a, b, preferred_element_type=F32, precision=precision)


def _dot_nt(a, b, precision=None):
    return lax.dot_general(a, b, (((1,), (1,)), ((), ())), preferred_element_type=F32,
                           precision=precision)


def _dot_tn(a, b, precision=None):
    return lax.dot_general(a, b, (((0,), (0,)), ((), ())), preferred_element_type=F32,
                           precision=precision)


def _split(a):
    hi = a.astype(BF16)
    return hi, a - hi.astype(F32)


def _dot3(a, b):
    a_hi, a_rest = _split(a)
    b_hi, b_rest = _split(b)
    return _dot(a_hi, b_hi) + (_dot(a_hi, b_rest.astype(BF16)) + _dot(a_rest.astype(BF16), b_hi))


def _dot_exact_lhs(a_bf16, b):
    b0, r = _split(b)
    b1, r = _split(r)
    return _dot(a_bf16, b0) + (_dot(a_bf16, b1) + _dot(a_bf16, r.astype(BF16)))


def _pick_tile(n, candidates):
    for c in candidates:
        if n % c == 0:
            return c
    return n


def _params(*semantics):
    return pltpu.CompilerParams(dimension_semantics=semantics, vmem_limit_bytes=VMEM_LIMIT_BYTES)


def _in_proj_kernel(h_ref, nw_ref, wqkvz_ref, wab_ref, wglu_ref, alog_ref, dtb_ref,
                    qkv_ref, z_ref, gb_ref, glu_ref):
    xb = _rmsnorm(h_ref[...], nw_ref[...]).astype(BF16)
    p = _dot(xb, wqkvz_ref[...])
    qkv_ref[...] = p[:, :QKV_WIDTH]
    z_ref[...] = p[:, QKV_WIDTH:]
    ab = _dot(xb, wab_ref[...])
    g = -jnp.exp(alog_ref[...]) * _softplus(ab + dtb_ref[...])
    lane = lax.broadcasted_iota(jnp.int32, ab.shape, 1)
    gb_ref[...] = jnp.where(lane < GDN_HEADS, g, _sigmoid(ab))
    pg = _dot(xb, wglu_ref[...])
    glu_ref[...] = pg[:, :CONF_WIDTH] * _sigmoid(pg[:, CONF_WIDTH:])


def _in_proj(h, nw, wqkvz, wab, wglu, alog, dtb, n_rows=None):
    n = h.shape[0] if n_rows is None else n_rows
    tm = _pick_tile(n, (384, 256, 128))
    row = lambda i: (i, 0)
    fixed = lambda i: (0, 0)
    widths = (QKV_WIDTH, GDN_WIDTH, LANES, CONF_WIDTH)
    return pl.pallas_call(
        _in_proj_kernel,
        grid=(n // tm,),
        in_specs=[
            pl.BlockSpec((tm, D_MODEL), row),
            pl.BlockSpec((1, D_MODEL), fixed),
            pl.BlockSpec(wqkvz.shape, fixed),
            pl.BlockSpec(wab.shape, fixed),
            pl.BlockSpec(wglu.shape, fixed),
            pl.BlockSpec((1, LANES), fixed),
            pl.BlockSpec((1, LANES), fixed),
        ],
        out_specs=[pl.BlockSpec((tm, w), row) for w in widths],
        out_shape=[jax.ShapeDtypeStruct((n, w), F32) for w in widths],
        compiler_params=_params("parallel"),
        name="in_proj",
    )(h, nw, wqkvz, wab, wglu, alog, dtb)


def _gdn_kernel(xq_ref, xk_ref, xv_ref, gb_ref, z_ref, cs_q_ref, cs_k_ref, cs_v_ref, s0_ref,
                cwq_ref, cwk_ref, cwv_ref, nw_ref,
                o_ref, s_ref,
                xq_s, xk_s, xv_s, g_s, b_s, u_s, att_s, wq_s, m_s, bm_s, gl_s, sall_s,
                *, chunk, n_pad, seq, bb_count):
    C = chunk
    n_chunks = (seq + n_pad) // C
    head = 8 + n_pad
    n_fac = int(math.log2(C))
    row_i = lax.broadcasted_iota(jnp.int32, (C, C), 0)
    col_i = lax.broadcasted_iota(jnp.int32, (C, C), 1)
    incl = row_i >= col_i
    strict = row_i > col_i
    tril_b = jnp.where(incl, 1.0, 0.0).astype(BF16)
    src_lane = lax.broadcasted_iota(jnp.int32, (LANES, HEAD_DIM), 0)
    pick_g = jnp.where(src_lane == pl.program_id(1), 1.0, 0.0).astype(BF16)
    pick_b = jnp.where(src_lane == pl.program_id(1) + GDN_HEADS, 1.0, 0.0).astype(BF16)
    eye =jnp.where(row_i == col_i, 1.0, 0.0).astype(F32)
    group = GDN_GROUP
    live = lax.broadcasted_iota(jnp.int32, (C, HEAD_DIM), 0) >= n_pad
    cws = (cwq_ref[...], cwk_ref[...], cwv_ref[...])
    nw = nw_ref[...]

    def conv(blk, cw):
        acc = blk[5:5 + C] * cw[0:1]
        for j in range(1, GDN_CONV):
            acc = acc + blk[5 + j:5 + j + C] * cw[j:j + 1]
        return _silu(acc)

    slot = lambda bb, c: bb * n_chunks + c

    is_first = lambda c: isinstance(c, int) and c == 0

    def build(items, fill=()):
        fill = list(fill)
        tick = lambda: fill.pop(0)() if fill else None
        each = lambda f, *ls: [f(*a) for a in zip(*ls)]
        at = [(bb, c * C if isinstance(c, int) else pl.multiple_of(c * C, SUBLANES)) for bb, c in items]
        qc = [conv(xq_s[bb, pl.ds(t0, C + 8), :], cws[0]) for bb, t0 in at]
        kc = [conv(xk_s[bb, pl.ds(t0, C + 8), :], cws[1]) for bb, t0 in at]
        v = [conv(xv_s[bb, pl.ds(t0, C + 8), :], cws[2]) for bb, t0 in at]
        q = each(lambda a: a * lax.rsqrt(jnp.sum(a * a, axis=-1, keepdims=True) + EPS) * (HEAD_DIM ** -0.5), qc)
        k = each(lambda a: a * lax.rsqrt(jnp.sum(a * a, axis=-1, keepdims=True) + EPS), kc)
        g = [g_s[bb, pl.ds(t0, C), :] for bb, t0 in at]
        beta = [b_s[bb, pl.ds(t0, C), :] for bb, t0 in at]
        if n_pad:
            q, k, v = ([jnp.where(live, a, 0.0) if is_first(c) else a for a, (_, c) in zip(l, items)]
                       for l in (q, k, v))
        cum = each(lambda a: _dot_exact_lhs(tril_b, jnp.concatenate([a, jnp.where(strict, a[:, :C], 0.0)], axis=1)), g)
        tick()
        gcum = [a[:, :HEAD_DIM] for a in cum]
        decay = [jnp.where(incl, jnp.exp(a[:, HEAD_DIM:]), 0.0) for a in cum]
        eg = [jnp.exp(a) for a in gcum]
        kb = each(lambda a, b_: a * b_, k, beta)
        kq = each(lambda a, b_, c_: _dot_nt(jnp.concatenate([a, b_], axis=0).astype(BF16), c_.astype(BF16)), kb, q, k)
        tick()
        p = each(lambda a, d: jnp.where(strict, -(a[:C] * d), 0.0), kq, decay)
        att = each(lambda a, d: jnp.where(incl, a[C:] * d, 0.0), kq, decay)
        t = [eye + a for a in p]
        p = each(lambda a: _dot3(a, a), p)
        tick()
        for step in range(1, n_fac):
            tick()
            if step == n_fac - 1:
                t = each(lambda a, b_: a + _dot3(b_, a), t, p)
            else:
                y = each(lambda a, b_: _dot3(b_, jnp.concatenate([a, b_], axis=1)), t, p)
                t = each(lambda a, b_: a + b_[:, :C], t, y)
                p = [a[:, C:] for a in y]
        x = each(lambda t_, v_, b_, kb_, eg_: _dot3(t_, jnp.concatenate([v_ * b_, kb_ * eg_], axis=1)),
                 t, v, beta, kb, eg)
        while fill:
            tick()
        g_last = [a[C - 1:C, :] for a in gcum]
        kt = each(lambda k_, gl, gc: k_ * jnp.exp(gl - gc), k, g_last, gcum)
        ktx = each(lambda a, b_: _dot_tn(a.astype(BF16), b_.astype(BF16)), kt, x)
        for i, (bb, c) in enumerate(items):
            n = slot(bb, c)
            u_s[n] = x[i][:, :HEAD_DIM]
            att_s[n] = att[i].astype(BF16)
            wq_s[n] = jnp.concatenate([x[i][:, HEAD_DIM:], q[i] * eg[i]], axis=0).astype(BF16)
            bm_s[n] = ktx[i][:, :HEAD_DIM]
            m_s[n] = (-ktx[i][:, HEAD_DIM:]).astype(BF16)
            gl_s[n] = jnp.exp(g_last[i])

    def carry(items):
        s = [s_ref[bb, 0] for bb, _ in items]
        for (bb, c), s_i in zip(items, s):
            sall_s[slot(bb, c)] = s_i
        new = [gl_s[slot(bb, c)] * s_i + _dot(m_s[slot(bb, c)], s_i.astype(BF16)) + bm_s[slot(bb, c)]
               for (bb, c), s_i in zip(items, s)]
        for (bb, _), s_i in zip(items, new):
            s_ref[bb, 0] = s_i

    def emit(items):
        ns = [slot(bb, c) for bb, c in items]
        ws_qs = [_dot(wq_s[n], sall_s[n].astype(BF16)) for n in ns]
        v_new = [u_s[n] - a[:C] for n, a in zip(ns, ws_qs)]
        o = [a[C:] + _dot(att_s[n], vn.astype(BF16)) for n, a, vn in zip(ns, ws_qs, v_new)]
        o = [_rmsnorm(a, nw) for a in o]
        for (bb, c), a in zip(items, o):
            if is_first(c):
                rows = C - n_pad
                o_ref[bb, 0:rows, :] = (a[n_pad:] * _silu(z_ref[bb, 0:rows, :])).astype(BF16)
            else:
                r0 = c * C - n_pad if isinstance(c, int) else pl.multiple_of(c * C - n_pad, 2 * SUBLANES)
                o_ref[bb, pl.ds(r0, C), :] = (a * _silu(z_ref[bb, pl.ds(r0, C), :])).astype(BF16)

    lead = (n_chunks - 1) % group + 1
    if lead == 1 and n_chunks > 1:
        lead += group
    leading = [(bb, c) for bb in range(bb_count) for c in range(lead)]

    n_loop = (n_chunks - lead) // group
    later = lambda i: [(0, lead + i * group + j) for j in range(group)]
    carry_range = lambda lo, hi: lax.fori_loop(lo, hi, lambda c, _: (carry([(0, c)]), 0)[1], 0)

    for bb in range(bb_count):
        for x_s, x_ref, cs_ref in ((xq_s, xq_ref, cs_q_ref), (xk_s, xk_ref, cs_k_ref),
                                   (xv_s, xv_ref, cs_v_ref)):
            x_s[bb, 0:head, :] = jnp.zeros((head, HEAD_DIM), F32)
            x_s[bb, head - (GDN_CONV - 1):head, :] = cs_ref[bb]
            x_s[bb, head:head + seq, :] = x_ref[bb]
        if n_pad:
            g_s[bb, 0:n_pad, :] = jnp.zeros((n_pad, HEAD_DIM), F32)
            b_s[bb, 0:n_pad, :] = jnp.zeros((n_pad, HEAD_DIM), F32)
        pieces = []
        rest = gb_ref[bb]
        for _ in range(3):
            hi, rest = _split(rest)
            pieces.append(hi)
        spread = lambda sel: _dot(pieces[0], sel) + (_dot(pieces[1], sel) + _dot(pieces[2], sel))
        g_s[bb, n_pad:n_pad + seq, :] = spread(pick_g)
        b_s[bb, n_pad:n_pad + seq, :] = spread(pick_b)
        s_ref[bb, 0] = s0_ref[bb, 0]

    build(leading)
    carry([(bb, 0) for bb in range(bb_count)])

    def build_later(i, _):
        build(later(i), [functools.partial(carry, [(0, 1 + i * group + j)]) for j in range(group)])
        return 0

    if n_loop:
        lax.fori_loop(0, n_loop, build_later, 0)
    if n_chunks > 1:
        carry_range(1 + n_loop * group, n_chunks)
    emit(leading)
    if n_loop:
        lax.fori_loop(0, n_loop, lambda i, _: (emit(later(i)), 0)[1], 0)


def _gdn(qkv, gb, z, conv_state, s0, conv_w, norm_w, *, chunk, n_pad, bb_count):
    b, seq, _ = qkv.shape
    n_chunks = (seq + n_pad) // chunk
    assert n_chunks * chunk == seq + n_pad and b % bb_count == 0
    assert n_chunks == 1 or bb_count == 1
    col = lambda off: (lambda i, h: (i, 0, off + h))
    wcol = lambda off: (lambda i, h: (0, off + h))
    seq_blk = (bb_count, seq, HEAD_DIM)
    cs_blk = (bb_count, GDN_CONV - 1, HEAD_DIM)
    s_blk = (bb_count, 1, HEAD_DIM, HEAD_DIM)
    s_map = lambda i, h: (i, h, 0, 0)
    ext = chunk * n_chunks
    per_chunk = lambda r, c, dt=F32: pltpu.VMEM((bb_count * n_chunks, r, c), dt)
    kern = functools.partial(_gdn_kernel, chunk=chunk, n_pad=n_pad, seq=seq, bb_count=bb_count)
    return pl.pallas_call(
        kern,
        grid=(b // bb_count, GDN_HEADS),
        in_specs=[
            pl.BlockSpec(seq_blk, col(0)),
            pl.BlockSpec(seq_blk, col(GDN_HEADS)),
            pl.BlockSpec(seq_blk, col(2 * GDN_HEADS)),
            pl.BlockSpec((bb_count, seq, LANES), lambda i, h: (i, 0, 0)),
            pl.BlockSpec(seq_blk, col(0)),
            pl.BlockSpec(cs_blk, col(0)),
            pl.BlockSpec(cs_blk, col(GDN_HEADS)),
            pl.BlockSpec(cs_blk, col(2 * GDN_HEADS)),
            pl.BlockSpec(s_blk, s_map),
            pl.BlockSpec((GDN_CONV, HEAD_DIM), wcol(0)),
            pl.BlockSpec((GDN_CONV, HEAD_DIM), wcol(GDN_HEADS)),
            pl.BlockSpec((GDN_CONV, HEAD_DIM), wcol(2 * GDN_HEADS)),
            pl.BlockSpec((1, HEAD_DIM), lambda i, h: (0, 0)),
        ],
        out_specs=[pl.BlockSpec(seq_blk, col(0)), pl.BlockSpec(s_blk, s_map)],
        out_shape=[jax.ShapeDtypeStruct((b, seq, GDN_WIDTH), BF16),
                   jax.ShapeDtypeStruct((b, GDN_HEADS, HEAD_DIM, HEAD_DIM), F32)],
        scratch_shapes=[
            pltpu.VMEM((bb_count, ext + 8, HEAD_DIM), F32),
            pltpu.VMEM((bb_count, ext + 8, HEAD_DIM), F32),
            pltpu.VMEM((bb_count, ext + 8, HEAD_DIM), F32),
            pltpu.VMEM((bb_count, ext, HEAD_DIM), F32),
            pltpu.VMEM((bb_count, ext, HEAD_DIM), F32),
            per_chunk(chunk, HEAD_DIM),
            per_chunk(chunk, chunk, BF16),
            per_chunk(2 * chunk, HEAD_DIM, BF16),
            per_chunk(HEAD_DIM, HEAD_DIM, BF16),
            per_chunk(HEAD_DIM, HEAD_DIM),
            per_chunk(1, HEAD_DIM),
            per_chunk(HEAD_DIM, HEAD_DIM),
        ],
        compiler_params=_params("parallel", "parallel"),
        name="gdn",
    )(qkv, qkv, qkv, gb, z, conv_state, conv_state, conv_state, s0,
      conv_w, conv_w, conv_w, norm_w)


CONF_HALO = 32
CONF_ROWS = 16


def _conf_out_seq_kernel(glu_ref, o_ref, h_ref, cstate_ref, dw_ref, dwb_ref, lnw_ref, lnb_ref, wout_ref,
                         out_ref, ext_s, c_s, sh_s, *, tt):
    t = pl.program_id(1)
    lead = CONF_HALO - (CONF_CONV - 1)

    @pl.when(t == 0)
    def _():
        ext_s[0:lead, :] = jnp.zeros((lead, CONF_WIDTH), F32)
        ext_s[lead:CONF_HALO, :] = cstate_ref[0]

    @pl.when(t > 0)
    def _():
        ext_s[0:CONF_HALO, :] = ext_s[tt:tt + CONF_HALO, :]

    ext_s[CONF_HALO:CONF_HALO + tt, :] = glu_ref[0]
    dw = dw_ref[...]
    bias = dwb_ref[...]

    def rows(i, _):
        r0 = pl.multiple_of(i * CONF_ROWS, SUBLANES)
        blk = ext_s[pl.ds(r0, CONF_ROWS + CONF_HALO), :]
        span = CONF_ROWS + CONF_HALO - SUBLANES
        for s in range(1, SUBLANES):
            sh_s[s - 1] = blk[s:s + span]
        acc = bias
        for j in range(CONF_CONV):
            q, s = divmod(lead + j, SUBLANES)
            rows_j = slice(q * SUBLANES, q * SUBLANES + CONF_ROWS)
            acc = acc + (blk[rows_j] if s == 0 else sh_s[s - 1, rows_j, :]) * dw[j:j + 1]
        c_s[pl.ds(r0, CONF_ROWS), :] = acc
        return 0

    lax.fori_loop(0, tt // CONF_ROWS, rows, 0)
    c = c_s[...]
    mu = jnp.mean(c, axis=-1, keepdims=True)
    xc = c - mu
    var = jnp.mean(xc * xc, axis=-1, keepdims=True)
    c = _silu(xc * lax.rsqrt(var + EPS) * lnw_ref[...] + lnb_ref[...])
    mixed = jnp.concatenate([o_ref[0], c.astype(BF16)], axis=1)
    out_ref[...] = h_ref[...] + _dot(mixed, wout_ref[...])


def _conf_out_seq(glu, o, h, cstate, dw, dwb, lnw, lnb, wout):
    b, seq, _ = glu.shape
    tt = _pick_tile(seq, (688, 512, 256, 128, 64, 16))
    per_seq = seq // tt
    tile = lambda w: pl.BlockSpec((1, tt, w), lambda i, t: (i, t, 0))
    flat = pl.BlockSpec((tt, D_MODEL), lambda i, t: (i * per_seq + t, 0))
    fixed = lambda a: pl.BlockSpec(a.shape, lambda i, t: (0,) * a.ndim)
    return pl.pallas_call(
        functools.partial(_conf_out_seq_kernel, tt=tt),
        grid=(b, per_seq),
        in_specs=[tile(CONF_WIDTH), tile(GDN_WIDTH), flat,
                  pl.BlockSpec((1, CONF_CONV - 1, CONF_WIDTH), lambda i, t: (i, 0, 0)),
                  fixed(dw), fixed(dwb), fixed(lnw), fixed(lnb), fixed(wout)],
        out_specs=flat,
        out_shape=jax.ShapeDtypeStruct(h.shape, F32),
        input_output_aliases={2: 0},
        scratch_shapes=[pltpu.VMEM((tt + CONF_HALO, CONF_WIDTH), F32),
                        pltpu.VMEM((tt, CONF_WIDTH), F32),
                        pltpu.VMEM((SUBLANES - 1, CONF_ROWS + CONF_HALO - SUBLANES, CONF_WIDTH), F32)],
        compiler_params=_params("parallel", "arbitrary"),
        name="conf_out_seq",
    )(glu, o, h, cstate, dw, dwb, lnw, lnb, wout)


def _conf_out_step_kernel(cext_ref, o_ref, h_ref, dw_ref, dwb_ref, lnw_ref, lnb_ref, wout_ref, out_ref,
                          *, steps):
    t = pl.program_id(0)

    @pl.when(t < steps)
    def _():
        dw = dw_ref[...]
        c = dwb_ref[...] + cext_ref[t] * dw[0:1]
        for j in range(1, CONF_CONV):
            c = c + cext_ref[t + j] * dw[j:j + 1]
        mu = jnp.mean(c, axis=-1, keepdims=True)
        xc = c - mu
        var = jnp.mean(xc * xc, axis=-1, keepdims=True)
        c = _silu(xc * lax.rsqrt(var + EPS) * lnw_ref[...] + lnb_ref[...])
        mixed = jnp.concatenate([o_ref[...], c.astype(BF16)], axis=1)
        out_ref[...] = h_ref[...] + _dot(mixed, wout_ref[...])

    @pl.when(t >= steps)
    def _():
        out_ref[...] = jnp.zeros_like(out_ref)


def _conf_out_step(cext, o, h, row0, n_zero, dw, dwb, lnw, lnb, wout):
    steps = cext.shape[0] - (CONF_CONV - 1)
    b = cext.shape[1]
    assert row0 % b == 0 and n_zero % b == 0 and row0 + steps * b + n_zero <= h.shape[0]
    full = lambda a: pl.BlockSpec(a.shape, lambda t: (0,) * a.ndim)
    rows = pl.BlockSpec((b, D_MODEL), lambda t: (row0 // b + t, 0))
    return pl.pallas_call(
        functools.partial(_conf_out_step_kernel, steps=steps),
        grid=(steps + n_zero // b,),
        in_specs=[full(cext), pl.BlockSpec((b, GDN_WIDTH), lambda t: (jnp.minimum(t, steps - 1), 0)), rows,
                  full(dw), full(dwb), full(lnw), full(lnb), full(wout)],
        out_specs=rows,
        out_shape=jax.ShapeDtypeStruct(h.shape, F32),
        input_output_aliases={2: 0},
        compiler_params=_params("arbitrary"),
        name="conf_out_step",
    )(cext, o, h, dw, dwb, lnw, lnb, wout)


N_RANKS = PEER_TOPK + 1
RANK_ROWS = 24
NEG_INF = float("-inf")


def _top_rows(s, count):
    rows = []
    for r in range(count):
        m = jnp.max(s, axis=0, keepdims=True)
        rows.append(m)
        s = jnp.where(s == m, NEG_INF, s)
    return rows


TOP_WAYS = 4


def _top_rows_chained(s, count):
    n = s.shape[0] // TOP_WAYS
    lv = [s[i * n:(i + 1) * n] for i in range(TOP_WAYS)]
    for i, j in ((0, 1), (2, 3), (0, 2), (1, 3), (1, 2)):
        lv[i], lv[j] = jnp.maximum(lv[i], lv[j]), jnp.minimum(lv[i], lv[j])
    rows = []
    for r in range(count):
        m = jnp.max(lv[0], axis=0, keepdims=True)
        rows.append(m)
        hit = lv[0] == m
        for i in range(TOP_WAYS - 1):
            lv[i] = jnp.where(hit, lv[i + 1], lv[i])
        lv[-1] = jnp.where(hit, NEG_INF, lv[-1])
    return rows


def _prefix_count(hits):
    out = jnp.zeros(hits[0].shape, F32)
    for r, hit in enumerate(hits):
        out = jnp.where(hit, float(r + 1), out)
    return out


def _stack_rows(rows, height, tm):
    idx = lax.broadcasted_iota(jnp.int32, (height, tm), 0)
    out = jnp.full((height, tm), NEG_INF, F32)
    for r, row in enumerate(rows):
        out = jnp.where(idx == r, row, out)
    return out


def _peer_score_kernel(h_ref, nw_ref, wqt_ref, keys_ref, xnt_ref, cnt_ref, e1_ref, rank2_ref, e2_ref):
    tm = h_ref.shape[0]
    xn = _rmsnorm(h_ref[...], nw_ref[...])
    xnt = xn.T.astype(BF16)
    xnt_ref[...] = xnt
    qt = _dot(wqt_ref[...], xnt).astype(BF16)
    idx8 = lax.broadcasted_iota(jnp.int32, (SUBLANES, tm), 0)
    for hh in range(PEER_HEADS):
        s = []
        for p in range(2):
            r0 = (hh * 2 + p) * N_KEYS
            s.append(_dot(keys_ref[hh, p], qt[r0:r0 + N_KEYS, :]))
        a = _top_rows_chained(s[0], N_RANKS)
        b = _top_rows_chained(s[1], N_RANKS)
        a_st = _stack_rows(a, RANK_ROWS, tm)
        b_st = _stack_rows(b, RANK_ROWS, tm)
        cands = [a[0] + b_st,
                 jnp.where(lax.broadcasted_iota(jnp.int32, (RANK_ROWS, tm), 0) >= 1,
                           a_st + b[0], NEG_INF)]
        for r1 in range(1, N_RANKS):
            hi = N_RANKS // (r1 + 1) - 1
            if hi >= 1 and r1 <= 4:
                cands.append(jnp.where((idx8 >= 1) & (idx8 <= hi), a[r1] + b_st[0:SUBLANES], NEG_INF))
        cands.append(jnp.where((idx8 >= 5) & (idx8 <= N_RANKS // 2 - 1),
                               a_st[0:SUBLANES] + b[1], NEG_INF))
        cand = jnp.concatenate(cands, axis=0)
        top = _top_rows(cand, N_RANKS)
        z = jnp.zeros_like(top[0])
        for r in range(PEER_TOPK):
            z = z + jnp.exp(top[r] - top[0])
        thr = 0.5 * (top[PEER_TOPK - 1] + top[PEER_TOPK])
        cnt1 = _prefix_count([s[0] >= thr - b[r2] for r2 in range(PEER_TOPK)])
        rank2 = _prefix_count([s[1] < b[r] for r in range(PEER_TOPK)])
        cnt_ref[hh] = cnt1
        e1_ref[hh] = 0.5 * jnp.exp(s[0] - a[0]) / z
        rank2_ref[hh] = rank2.astype(BF16)
        e2_ref[hh] = jnp.exp(s[1] - b[0]).astype(BF16)


def _peer_score(h, nw, wqt, keys):
    n = h.shape[0]
    tm = _pick_tile(n, (PEER_TOKENS, 256, 128))
    heads = lambda: pl.BlockSpec((PEER_HEADS, N_KEYS, tm), lambda i: (0, 0, i))
    return pl.pallas_call(
        _peer_score_kernel,
        grid=(n // tm,),
        in_specs=[pl.BlockSpec((tm, D_MODEL), lambda i: (i, 0)),
                  pl.BlockSpec((1, D_MODEL), lambda i: (0, 0)),
                  pl.BlockSpec(wqt.shape, lambda i: (0, 0)),
                  pl.BlockSpec(keys.shape, lambda i: (0, 0, 0, 0))],
        out_specs=[pl.BlockSpec((D_MODEL, tm), lambda i: (0, i)), heads(), heads(), heads(), heads()],
        out_shape=[jax.ShapeDtypeStruct((D_MODEL, n), BF16)]
        + [jax.ShapeDtypeStruct((PEER_HEADS, N_KEYS, n), dt) for dt in (F32, F32, BF16, BF16)],
        compiler_params=_params("parallel"),
        name="peer_score",
    )(h, nw, wqt, keys)


EXPERT_CHUNK = 1024
PEER_TOKENS = 512
PEER_DENSE_TOKENS = 512
SQRT_HALF = math.sqrt(0.5)


def _peer_dense_kernel(xnt_ref, cnt_ref, e1_ref, rank2_ref, e2_ref, u_ref, vt_ref, h_ref, fw_ref,
                       out_ref, acc_s, w_s, gate_s, cnt_s, e1_s, *, final_norm, n_chunks):
    s = pl.program_id(0)
    c = lax.rem(s, n_chunks)
    cur = lax.rem(c, 2)
    tm = xnt_ref.shape[1]
    half = tm // 2

    @pl.when(s == 0)
    def _():
        acc_s[...] = jnp.zeros_like(acc_s)
        w_s[1] = jnp.zeros(w_s.shape[1:], BF16)

    prev = 1 - cur
    n_blocks = EXPERT_CHUNK // N_KEYS
    n_lt = tm // LANES
    per_lt = PEER_HEADS * n_blocks
    for l in range(n_lt):
        cnt_s[l * per_lt:(l + 1) * per_lt, :] = cnt_ref[:, :, l * LANES:(l + 1) * LANES].reshape(per_lt, LANES)
        e1_s[l * per_lt:(l + 1) * per_lt, :] = e1_ref[:, :, l * LANES:(l + 1) * LANES].reshape(per_lt, LANES)

    def rows(ref, hh, j):
        words = jnp.concatenate([jnp.broadcast_to(ref[pl.ds(l * per_lt + hh * n_blocks + j, 1), :], (SUBLANES, LANES))
                                 for l in range(n_lt)], axis=1)
        packed = jnp.concatenate([words, words], axis=0).astype(BF16)
        return jnp.concatenate([packed] * (N_KEYS // (2 * SUBLANES)), axis=0)

    xnt = xnt_ref[...]
    block = lambda j: slice(j * N_KEYS, (j + 1) * N_KEYS)

    def gate(j):
        total = None
        for hh in range(PEER_HEADS):
            e2 = e2_ref[hh]
            sel = jnp.where(rank2_ref[hh] < rows(cnt_s, hh, j), e2, jnp.zeros_like(e2))
            term = rows(e1_s, hh, j) * sel
            total = term if total is None else total + term
        gate_s[block(j), :] = total

    def project(j):
        return _dot(u_ref[block(j), :], xnt)

    def finish(j, pre):
        act = (pre * (1.0 + lax.erf(pre * SQRT_HALF))).astype(BF16)
        w_s[cur, block(j), :] = act * gate_s[block(j), :]

    ahead = 3
    pres = {j: project(j) for j in range(ahead)}
    gate(0)
    for j in range(n_blocks):
        if j + ahead < n_blocks:
            pres[j + ahead] = project(j + ahead)
        if j == 0:
            acc_s[:, :half] += _dot(vt_ref[0], w_s[prev, :, :half])
        if j == n_blocks // 2:
            acc_s[:, half:] += _dot(vt_ref[0], w_s[prev, :, half:])
        if j + 1 < n_blocks:
            gate(j + 1)
        finish(j, pres.pop(j))

    @pl.when(c == 0)
    def _():
        y = h_ref[...] + acc_s[...].T
        if final_norm:
            y = _rmsnorm(y, fw_ref[...])
        out_ref[...] = y
        acc_s[...] = jnp.zeros_like(acc_s)


def _peer_dense(xnt, cnt, e1, rank2, e2, u, vt, h, fw, *, layer, final_norm):
    n = h.shape[0]
    tm = _pick_tile(n, (PEER_DENSE_TOKENS, PEER_TOKENS, 256, 128))
    n_tiles = n // tm
    n_chunks = u.shape[1] // EXPERT_CHUNK
    assert n_chunks % 2 == 0
    rows_per_chunk = EXPERT_CHUNK // N_KEYS
    tile = lambda s: jnp.minimum(s // n_chunks, n_tiles - 1)
    chunk = lambda s: lax.rem(s, n_chunks)
    done = lambda s: jnp.maximum(s - 1, 0) // n_chunks
    by_chunk = lambda: pl.BlockSpec((PEER_HEADS, rows_per_chunk, tm), lambda s: (0, chunk(s), tile(s)))
    by_tile = lambda: pl.BlockSpec((PEER_HEADS, N_KEYS, tm), lambda s: (0, 0, tile(s)))
    return pl.pallas_call(
        functools.partial(_peer_dense_kernel, final_norm=final_norm, n_chunks=n_chunks),
        grid=(n_tiles * n_chunks + 1,),
        in_specs=[pl.BlockSpec((D_MODEL, tm), lambda s: (0, tile(s))),
                  by_chunk(), by_chunk(), by_tile(), by_tile(),
                  pl.BlockSpec((None, EXPERT_CHUNK, D_MODEL), lambda s: (layer, chunk(s), 0)),
                  pl.BlockSpec((None, 1, D_MODEL, EXPERT_CHUNK), lambda s: (layer, chunk(s + n_chunks - 1), 0, 0)),
                  pl.BlockSpec((tm, D_MODEL), lambda s: (done(s), 0)),
                  pl.BlockSpec((1, D_MODEL), lambda s: (0, 0))],
        out_specs=pl.BlockSpec((tm, D_MODEL), lambda s: (done(s), 0)),
        out_shape=jax.ShapeDtypeStruct((n, D_MODEL), F32),
        scratch_shapes=[pltpu.VMEM((D_MODEL, tm), F32), pltpu.VMEM((2, EXPERT_CHUNK, tm), BF16),
                        pltpu.VMEM((EXPERT_CHUNK, tm), BF16)]
        + [pltpu.VMEM((PEER_HEADS * rows_per_chunk * tm // LANES, LANES), F32)] * 2,
        compiler_params=_params("arbitrary"),
        name="peer_dense",
    )(xnt, cnt, e1, rank2, e2, u, vt, h, fw)


def _prep_layer(layer, norm_mix_w, w_in, conv_qkv_w, a_log, dt_bias, gdn_norm_w, conf_dw_w, conf_dw_b,
                conf_ln_w, conf_ln_b, w_out, norm_ffn_w, w_query, sub_keys):
    s0 = QKV_WIDTH
    s1 = s0 + GDN_WIDTH
    s2 = s1 + GDN_HEADS
    s3 = s2 + GDN_HEADS
    w = w_in[layer]
    row = lambda v: v.reshape(1, -1)
    return dict(
        norm_mix_w=row(norm_mix_w[layer]),
        wqkvz=w[:, :s1].astype(BF16),
        wab=jnp.pad(w[:, s1:s3], ((0, 0), (0, LANES - 2 * GDN_HEADS))).astype(BF16),
        wglu=w[:, s3:].astype(BF16),
        alog=row(jnp.pad(a_log[layer], (0, LANES - GDN_HEADS))),
        dtb=row(jnp.pad(dt_bias[layer], (0, LANES - GDN_HEADS))),
        conv_w=conv_qkv_w[layer],
        gdn_norm_w=row(gdn_norm_w[layer]),
        dw=conf_dw_w[layer], dwb=row(conf_dw_b[layer]),
        lnw=row(conf_ln_w[layer]), lnb=row(conf_ln_b[layer]),
        wout=w_out[layer].astype(BF16),
        norm_ffn_w=row(norm_ffn_w[layer]),
        wqt=w_query[layer].T.astype(BF16),
        keys=sub_keys[layer].astype(BF16),
    )


def _peer(h, lw, u, vt, fw, layer, final_norm):
    xnt, cnt, e1, rank2, e2 = _peer_score(h, lw["norm_ffn_w"], lw["wqt"], lw["keys"])
    return _peer_dense(xnt, cnt, e1, rank2, e2, u, vt, h, fw, layer=layer, final_norm=final_norm)


def _mix_prompt(h, lw, b, seq):
    n_pad = (-seq) % GDN_CHUNK
    qkv, z, gb, glu = _in_proj(h, lw["norm_mix_w"], lw["wqkvz"], lw["wab"], lw["wglu"], lw["alog"], lw["dtb"],
                               n_rows=b * seq)
    shp = lambda a: a.reshape(b, seq, a.shape[-1])
    qkv, z, gb, glu = shp(qkv), shp(z), shp(gb), shp(glu)
    o, s_new = _gdn(qkv, gb, z,
                    jnp.zeros((b, GDN_CONV - 1, QKV_WIDTH), F32),
                    jnp.zeros((b, GDN_HEADS, HEAD_DIM, HEAD_DIM), F32),
                    lw["conv_w"], lw["gdn_norm_w"], chunk=GDN_CHUNK, n_pad=n_pad, bb_count=1)
    h = _conf_out_seq(glu, o, h, jnp.zeros((b, CONF_CONV - 1, CONF_WIDTH), F32),
                      lw["dw"], lw["dwb"], lw["lnw"], lw["lnb"], lw["wout"])
    return h, s_new, qkv[:, seq - (GDN_CONV - 1):], glu[:, seq - (CONF_CONV - 1):]


def _mix_sample(h, row0, n_zero, lw, b, steps, s0, qkv_state, conf_state):
    n_pad = (-steps) % SUBLANES
    to_tb = lambda a: jnp.swapaxes(a, 0, 1)
    qkv, z, gb, glu = _in_proj(h[row0:row0 + steps * b], lw["norm_mix_w"], lw["wqkvz"], lw["wab"], lw["wglu"],
                               lw["alog"], lw["dtb"])
    bt = lambda a: to_tb(a.reshape(steps, b, a.shape[-1]))
    qkv_bt = bt(qkv)
    o, s_new = _gdn(qkv_bt, bt(gb), bt(z), qkv_state, s0,
                    lw["conv_w"], lw["gdn_norm_w"], chunk=steps + n_pad, n_pad=n_pad, bb_count=8)
    cext = jnp.concatenate([to_tb(conf_state), glu.reshape(steps, b, CONF_WIDTH)], axis=0)
    h = _conf_out_step(cext, to_tb(o).reshape(steps * b, GDN_WIDTH), h, row0, n_zero,
                       lw["dw"], lw["dwb"], lw["lnw"], lw["lnb"], lw["wout"])
    qkv_ext = jnp.concatenate([qkv_state, qkv_bt], axis=1)
    return (h, s_new, qkv_ext[:, qkv_ext.shape[1] - (GDN_CONV - 1):],
            to_tb(cext[cext.shape[0] - (CONF_CONV - 1):]))


def kernel(x_prompt, x_sample, state_gdn, state_qkv_conv, state_conf_conv, meta_tokens, norm_mix_w, w_in,
           conv_qkv_w, a_log, dt_bias, gdn_norm_w, conf_dw_w, conf_dw_b, conf_ln_w, conf_ln_b, w_out,
           norm_ffn_w, w_query, sub_keys, expert_u, expert_v, final_norm_w):
    depth = w_in.shape[0]
    layers = [_prep_layer(l, norm_mix_w, w_in, conv_qkv_w, a_log, dt_bias, gdn_norm_w, conf_dw_w,
                          conf_dw_b, conf_ln_w, conf_ln_b, w_out, norm_ffn_w, w_query, sub_keys)
              for l in range(depth)]
    u_all = expert_u.astype(BF16)
    vt_all = jnp.swapaxes(expert_v.astype(BF16).reshape(depth, -1, EXPERT_CHUNK, D_MODEL), 2, 3)
    fw = final_norm_w.reshape(1, D_MODEL)
    b_p, seq0, _ = x_prompt.shape
    seq = seq0 + N_META
    b_s, steps, _ = x_sample.shape
    n_p, n_s = b_p * seq, b_s * steps
    n_fill = (-(n_p + n_s)) % PEER_DENSE_TOKENS
    meta = jnp.broadcast_to(meta_tokens[None], (b_p, N_META, D_MODEL))
    h = jnp.concatenate([jnp.concatenate([meta, x_prompt], axis=1).reshape(n_p, D_MODEL),
                         jnp.swapaxes(x_sample, 0, 1).reshape(n_s, D_MODEL),
                         jnp.zeros((n_fill, D_MODEL), F32)], axis=0)
    outs_p, outs_s = [], []
    for li, lw in enumerate(layers):
        h, *st_p = _mix_prompt(h, lw, b_p, seq)
        h, *st_s = _mix_sample(h, n_p, n_fill, lw, b_s, steps, state_gdn[li], state_qkv_conv[li],
                               state_conf_conv[li])
        outs_p.append(st_p)
        outs_s.append(st_s)
        h = _peer(h, lw, u_all, vt_all, fw, li, li == depth - 1)
    y_p = h[:n_p].reshape(b_p, seq, D_MODEL)[:, N_META:]
    y_s = jnp.swapaxes(h[n_p:n_p + n_s].reshape(steps, b_s, D_MODEL), 0, 1)
    stack = lambda outs, i: jnp.stack([o[i] for o in outs])
    return (y_p, y_s, stack(outs_p, 0), stack(outs_p, 1), stack(outs_p, 2),
            stack(outs_s, 0), stack(outs_s, 1), stack(outs_s, 2))
```

```python
import functools
import math

import jax
import jax.numpy as jnp
from jax import lax
from jax.experimental import pallas as pl
from jax.experimental.pallas import tpu as pltpu

F32 = jnp.float32
BF16 = jnp.bfloat16

D_MODEL = 1024
N_META = 16
GDN_HEADS = 4
HEAD_DIM = 128
GDN_WIDTH = GDN_HEADS * HEAD_DIM
CONF_WIDTH = 512
QKV_WIDTH = 3 * GDN_WIDTH
GDN_CONV = 4
GDN_CHUNK = 64
GDN_GROUP = 8
CONF_CONV = 31
N_KEYS = 128
PEER_HEADS = 8
PEER_TOPK = 16
EPS = 1e-6

LANES = 128
SUBLANES = 8
VMEM_LIMIT_BYTES = 56 * 1024 * 1024


def _sigmoid(x):
    return 1.0 / (1.0 + jnp.exp(-x))


def _silu(x):
    return x * _sigmoid(x)


def _softplus(x):
    return jnp.maximum(x, 0.0) + jnp.log1p(jnp.exp(-jnp.abs(x)))


def _rmsnorm(x, w):
    return x * lax.rsqrt(jnp.mean(x * x, axis=-1, keepdims=True) + EPS) * w


def _dot(a, b, precision=None):
    return jnp.dot(a, b, preferred_element_type=F32, precision=precision)


def _dot_nt(a, b, precision=None):
    return lax.dot_general(a, b, (((1,), (1,)), ((), ())), preferred_element_type=F32,
                           precision=precision)


def _dot_tn(a, b, precision=None):
    return lax.dot_general(a, b, (((0,), (0,)), ((), ())), preferred_element_type=F32,
                           precision=precision)


def _split(a):
    hi = a.astype(BF16)
    return hi, a - hi.astype(F32)


def _dot3(a, b):
    a_hi, a_rest = _split(a)
    b_hi, b_rest = _split(b)
    return _dot(a_hi, b_hi) + (_dot(a_hi, b_rest.astype(BF16)) + _dot(a_rest.astype(BF16), b_hi))


def _dot_exact_lhs(a_bf16, b):
    b0, r = _split(b)
    b1, r = _split(r)
    return _dot(a_bf16, b0) + (_dot(a_bf16, b1) + _dot(a_bf16, r.astype(BF16)))


def _pick_tile(n, candidates):
    for c in candidates:
        if n % c == 0:
            return c
    return n


def _params(*semantics):
    return pltpu.CompilerParams(dimension_semantics=semantics, vmem_limit_bytes=VMEM_LIMIT_BYTES)


def _in_proj_kernel(h_ref, nw_ref, wqkvz_ref, wab_ref, wglu_ref, alog_ref, dtb_ref,
                    qkv_ref, z_ref, gb_ref, glu_ref):
    xb = _rmsnorm(h_ref[...], nw_ref[...]).astype(BF16)
    p = _dot(xb, wqkvz_ref[...])
    qkv_ref[...] = p[:, :QKV_WIDTH]
    z_ref[...] = p[:, QKV_WIDTH:]
    ab = _dot(xb, wab_ref[...])
    g = -jnp.exp(alog_ref[...]) * _softplus(ab + dtb_ref[...])
    lane = lax.broadcasted_iota(jnp.int32, ab.shape, 1)
    gb_ref[...] = jnp.where(lane < GDN_HEADS, g, _sigmoid(ab))
    pg = _dot(xb, wglu_ref[...])
    glu_ref[...] = pg[:, :CONF_WIDTH] * _sigmoid(pg[:, CONF_WIDTH:])


def _in_proj(h, nw, wqkvz, wab, wglu, alog, dtb, n_rows=None):
    n = h.shape[0] if n_rows is None else n_rows
    tm = _pick_tile(n, (384, 256, 128))
    row = lambda i: (i, 0)
    fixed = lambda i: (0, 0)
    widths = (QKV_WIDTH, GDN_WIDTH, LANES, CONF_WIDTH)
    return pl.pallas_call(
        _in_proj_kernel,
        grid=(n // tm,),
        in_specs=[
            pl.BlockSpec((tm, D_MODEL), row),
            pl.BlockSpec((1, D_MODEL), fixed),
            pl.BlockSpec(wqkvz.shape, fixed),
            pl.BlockSpec(wab.shape, fixed),
            pl.BlockSpec(wglu.shape, fixed),
            pl.BlockSpec((1, LANES), fixed),
            pl.BlockSpec((1, LANES), fixed),
        ],
        out_specs=[pl.BlockSpec((tm, w), row) for w in widths],
        out_shape=[jax.ShapeDtypeStruct((n, w), F32) for w in widths],
        compiler_params=_params("parallel"),
        name="in_proj",
    )(h, nw, wqkvz, wab, wglu, alog, dtb)


def _gdn_kernel(xq_ref, xk_ref, xv_ref, gb_ref, z_ref, cs_q_ref, cs_k_ref, cs_v_ref, s0_ref,
                cwq_ref, cwk_ref, cwv_ref, nw_ref,
                o_ref, s_ref,
                xq_s, xk_s, xv_s, g_s, b_s, u_s, att_s, wq_s, m_s, bm_s, gl_s, sall_s,
                *, chunk, n_pad, seq, bb_count):
    C = chunk
    n_chunks = (seq + n_pad) // C
    head = 8 + n_pad
    n_fac = int(math.log2(C))
    row_i = lax.broadcasted_iota(jnp.int32, (C, C), 0)
    col_i = lax.broadcasted_iota(jnp.int32, (C, C), 1)
    incl = row_i >= col_i
    strict = row_i > col_i
    tril_b = jnp.where(incl, 1.0, 0.0).astype(BF16)
    src_lane = lax.broadcasted_iota(jnp.int32, (LANES, HEAD_DIM), 0)
    pick_g = jnp.where(src_lane == pl.program_id(1), 1.0, 0.0).astype(BF16)
    pick_b = jnp.where(src_lane == pl.program_id(1) + GDN_HEADS, 1.0, 0.0).astype(BF16)
    eye =jnp.where(row_i == col_i, 1.0, 0.0).astype(F32)
    group = GDN_GROUP
    live = lax.broadcasted_iota(jnp.int32, (C, HEAD_DIM), 0) >= n_pad
    cws = (cwq_ref[...], cwk_ref[...], cwv_ref[...])
    nw = nw_ref[...]

    def conv(blk, cw):
        acc = blk[5:5 + C] * cw[0:1]
        for j in range(1, GDN_CONV):
            acc = acc + blk[5 + j:5 + j + C] * cw[j:j + 1]
        return _silu(acc)

    slot = lambda bb, c: bb * n_chunks + c

    is_first = lambda c: isinstance(c, int) and c == 0

    def build(items, fill=()):
        fill = list(fill)
        tick = lambda: fill.pop(0)() if fill else None
        each = lambda f, *ls: [f(*a) for a in zip(*ls)]
        at = [(bb, c * C if isinstance(c, int) else pl.multiple_of(c * C, SUBLANES)) for bb, c in items]
        qc = [conv(xq_s[bb, pl.ds(t0, C + 8), :], cws[0]) for bb, t0 in at]
        kc = [conv(xk_s[bb, pl.ds(t0, C + 8), :], cws[1]) for bb, t0 in at]
        v = [conv(xv_s[bb, pl.ds(t0, C + 8), :], cws[2]) for bb, t0 in at]
        q = each(lambda a: a * lax.rsqrt(jnp.sum(a * a, axis=-1, keepdims=True) + EPS) * (HEAD_DIM ** -0.5), qc)
        k = each(lambda a: a * lax.rsqrt(jnp.sum(a * a, axis=-1, keepdims=True) + EPS), kc)
        g = [g_s[bb, pl.ds(t0, C), :] for bb, t0 in at]
        beta = [b_s[bb, pl.ds(t0, C), :] for bb, t0 in at]
        if n_pad:
            q, k, v = ([jnp.where(live, a, 0.0) if is_first(c) else a for a, (_, c) in zip(l, items)]
                       for l in (q, k, v))
        cum = each(lambda a: _dot_exact_lhs(tril_b, jnp.concatenate([a, jnp.where(strict, a[:, :C], 0.0)], axis=1)), g)
        tick()
        gcum = [a[:, :HEAD_DIM] for a in cum]
        decay = [jnp.where(incl, jnp.exp(a[:, HEAD_DIM:]), 0.0) for a in cum]
        eg = [jnp.exp(a) for a in gcum]
        kb = each(lambda a, b_: a * b_, k, beta)
        kq = each(lambda a, b_, c_: _dot_nt(jnp.concatenate([a, b_], axis=0).astype(BF16), c_.astype(BF16)), kb, q, k)
        tick()
        p = each(lambda a, d: jnp.where(strict, -(a[:C] * d), 0.0), kq, decay)
        att = each(lambda a, d: jnp.where(incl, a[C:] * d, 0.0), kq, decay)
        t = [eye + a for a in p]
        p = each(lambda a: _dot3(a, a), p)
        tick()
        for step in range(1, n_fac):
            tick()
            if step == n_fac - 1:
                t = each(lambda a, b_: a + _dot3(b_, a), t, p)
            else:
                y = each(lambda a, b_: _dot3(b_, jnp.concatenate([a, b_], axis=1)), t, p)
                t = each(lambda a, b_: a + b_[:, :C], t, y)
                p = [a[:, C:] for a in y]
        x = each(lambda t_, v_, b_, kb_, eg_: _dot3(t_, jnp.concatenate([v_ * b_, kb_ * eg_], axis=1)),
                 t, v, beta, kb, eg)
        while fill:
            tick()
        g_last = [a[C - 1:C, :] for a in gcum]
        kt = each(lambda k_, gl, gc: k_ * jnp.exp(gl - gc), k, g_last, gcum)
        ktx = each(lambda a, b_: _dot_tn(a.astype(BF16), b_.astype(BF16)), kt, x)
        for i, (bb, c) in enumerate(items):
            n = slot(bb, c)
            u_s[n] = x[i][:, :HEAD_DIM]
            att_s[n] = att[i].astype(BF16)
            wq_s[n] = jnp.concatenate([x[i][:, HEAD_DIM:], q[i] * eg[i]], axis=0).astype(BF16)
            bm_s[n] = ktx[i][:, :HEAD_DIM]
            m_s[n] = (-ktx[i][:, HEAD_DIM:]).astype(BF16)
            gl_s[n] = jnp.exp(g_last[i])

    def carry(items):
        s = [s_ref[bb, 0] for bb, _ in items]
        for (bb, c), s_i in zip(items, s):
            sall_s[slot(bb, c)] = s_i
        new = [gl_s[slot(bb, c)] * s_i + _dot(m_s[slot(bb, c)], s_i.astype(BF16)) + bm_s[slot(bb, c)]
               for (bb, c), s_i in zip(items, s)]
        for (bb, _), s_i in zip(items, new):
            s_ref[bb, 0] = s_i

    def emit(items):
        ns = [slot(bb, c) for bb, c in items]
        ws_qs = [_dot(wq_s[n], sall_s[n].astype(BF16)) for n in ns]
        v_new = [u_s[n] - a[:C] for n, a in zip(ns, ws_qs)]
        o = [a[C:] + _dot(att_s[n], vn.astype(BF16)) for n, a, vn in zip(ns, ws_qs, v_new)]
        o = [_rmsnorm(a, nw) for a in o]
        for (bb, c), a in zip(items, o):
            if is_first(c):
                rows = C - n_pad
                o_ref[bb, 0:rows, :] = (a[n_pad:] * _silu(z_ref[bb, 0:rows, :])).astype(BF16)
            else:
                r0 = c * C - n_pad if isinstance(c, int) else pl.multiple_of(c * C - n_pad, 2 * SUBLANES)
                o_ref[bb, pl.ds(r0, C), :] = (a * _silu(z_ref[bb, pl.ds(r0, C), :])).astype(BF16)

    lead = (n_chunks - 1) % group + 1
    if lead == 1 and n_chunks > 1:
        lead += group
    leading = [(bb, c) for bb in range(bb_count) for c in range(lead)]

    n_loop = (n_chunks - lead) // group
    later = lambda i: [(0, lead + i * group + j) for j in range(group)]
    carry_range = lambda lo, hi: lax.fori_loop(lo, hi, lambda c, _: (carry([(0, c)]), 0)[1], 0)

    for bb in range(bb_count):
        for x_s, x_ref, cs_ref in ((xq_s, xq_ref, cs_q_ref), (xk_s, xk_ref, cs_k_ref),
                                   (xv_s, xv_ref, cs_v_ref)):
            x_s[bb, 0:head, :] = jnp.zeros((head, HEAD_DIM), F32)
            x_s[bb, head - (GDN_CONV - 1):head, :] = cs_ref[bb]
            x_s[bb, head:head + seq, :] = x_ref[bb]
        if n_pad:
            g_s[bb, 0:n_pad, :] = jnp.zeros((n_pad, HEAD_DIM), F32)
            b_s[bb, 0:n_pad, :] = jnp.zeros((n_pad, HEAD_DIM), F32)
        pieces = []
        rest = gb_ref[bb]
        for _ in range(3):
            hi, rest = _split(rest)
            pieces.append(hi)
        spread = lambda sel: _dot(pieces[0], sel) + (_dot(pieces[1], sel) + _dot(pieces[2], sel))
        g_s[bb, n_pad:n_pad + seq, :] = spread(pick_g)
        b_s[bb, n_pad:n_pad + seq, :] = spread(pick_b)
        s_ref[bb, 0] = s0_ref[bb, 0]

    build(leading)
    carry([(bb, 0) for bb in range(bb_count)])

    def build_later(i, _):
        build(later(i), [functools.partial(carry, [(0, 1 + i * group + j)]) for j in range(group)])
        return 0

    if n_loop:
        lax.fori_loop(0, n_loop, build_later, 0)
    if n_chunks > 1:
        carry_range(1 + n_loop * group, n_chunks)
    emit(leading)
    if n_loop:
        lax.fori_loop(0, n_loop, lambda i, _: (emit(later(i)), 0)[1], 0)


def _gdn(qkv, gb, z, conv_state, s0, conv_w, norm_w, *, chunk, n_pad, bb_count):
    b, seq, _ = qkv.shape
    n_chunks = (seq + n_pad) // chunk
    assert n_chunks * chunk == seq + n_pad and b % bb_count == 0
    assert n_chunks == 1 or bb_count == 1
    col = lambda off: (lambda i, h: (i, 0, off + h))
    wcol = lambda off: (lambda i, h: (0, off + h))
    seq_blk = (bb_count, seq, HEAD_DIM)
    cs_blk = (bb_count, GDN_CONV - 1, HEAD_DIM)
    s_blk = (bb_count, 1, HEAD_DIM, HEAD_DIM)
    s_map = lambda i, h: (i, h, 0, 0)
    ext = chunk * n_chunks
    per_chunk = lambda r, c, dt=F32: pltpu.VMEM((bb_count * n_chunks, r, c), dt)
    kern = functools.partial(_gdn_kernel, chunk=chunk, n_pad=n_pad, seq=seq, bb_count=bb_count)
    return pl.pallas_call(
        kern,
        grid=(b // bb_count, GDN_HEADS),
        in_specs=[
            pl.BlockSpec(seq_blk, col(0)),
            pl.BlockSpec(seq_blk, col(GDN_HEADS)),
            pl.BlockSpec(seq_blk, col(2 * GDN_HEADS)),
            pl.BlockSpec((bb_count, seq, LANES), lambda i, h: (i, 0, 0)),
            pl.BlockSpec(seq_blk, col(0)),
            pl.BlockSpec(cs_blk, col(0)),
            pl.BlockSpec(cs_blk, col(GDN_HEADS)),
            pl.BlockSpec(cs_blk, col(2 * GDN_HEADS)),
            pl.BlockSpec(s_blk, s_map),
            pl.BlockSpec((GDN_CONV, HEAD_DIM), wcol(0)),
            pl.BlockSpec((GDN_CONV, HEAD_DIM), wcol(GDN_HEADS)),
            pl.BlockSpec((GDN_CONV, HEAD_DIM), wcol(2 * GDN_HEADS)),
            pl.BlockSpec((1, HEAD_DIM), lambda i, h: (0, 0)),
        ],
        out_specs=[pl.BlockSpec(seq_blk, col(0)), pl.BlockSpec(s_blk, s_map)],
        out_shape=[jax.ShapeDtypeStruct((b, seq, GDN_WIDTH), BF16),
                   jax.ShapeDtypeStruct((b, GDN_HEADS, HEAD_DIM, HEAD_DIM), F32)],
        scratch_shapes=[
            pltpu.VMEM((bb_count, ext + 8, HEAD_DIM), F32),
            pltpu.VMEM((bb_count, ext + 8, HEAD_DIM), F32),
            pltpu.VMEM((bb_count, ext + 8, HEAD_DIM), F32),
            pltpu.VMEM((bb_count, ext, HEAD_DIM), F32),
            pltpu.VMEM((bb_count, ext, HEAD_DIM), F32),
            per_chunk(chunk, HEAD_DIM),
            per_chunk(chunk, chunk, BF16),
            per_chunk(2 * chunk, HEAD_DIM, BF16),
            per_chunk(HEAD_DIM, HEAD_DIM, BF16),
            per_chunk(HEAD_DIM, HEAD_DIM),
            per_chunk(1, HEAD_DIM),
            per_chunk(HEAD_DIM, HEAD_DIM),
        ],
        compiler_params=_params("parallel", "parallel"),
        name="gdn",
    )(qkv, qkv, qkv, gb, z, conv_state, conv_state, conv_state, s0,
      conv_w, conv_w, conv_w, norm_w)


CONF_HALO = 32
CONF_ROWS = 16


def _conf_out_seq_kernel(glu_ref, o_ref, h_ref, cstate_ref, dw_ref, dwb_ref, lnw_ref, lnb_ref, wout_ref,
                         out_ref, ext_s, c_s, sh_s, *, tt):
    t = pl.program_id(1)
    lead = CONF_HALO - (CONF_CONV - 1)

    @pl.when(t == 0)
    def _():
        ext_s[0:lead, :] = jnp.zeros((lead, CONF_WIDTH), F32)
        ext_s[lead:CONF_HALO, :] = cstate_ref[0]

    @pl.when(t > 0)
    def _():
        ext_s[0:CONF_HALO, :] = ext_s[tt:tt + CONF_HALO, :]

    ext_s[CONF_HALO:CONF_HALO + tt, :] = glu_ref[0]
    dw = dw_ref[...]
    bias = dwb_ref[...]

    def rows(i, _):
        r0 = pl.multiple_of(i * CONF_ROWS, SUBLANES)
        blk = ext_s[pl.ds(r0, CONF_ROWS + CONF_HALO), :]
        span = CONF_ROWS + CONF_HALO - SUBLANES
        for s in range(1, SUBLANES):
            sh_s[s - 1] = blk[s:s + span]
        acc = bias
        for j in range(CONF_CONV):
            q, s = divmod(lead + j, SUBLANES)
            rows_j = slice(q * SUBLANES, q * SUBLANES + CONF_ROWS)
            acc = acc + (blk[rows_j] if s == 0 else sh_s[s - 1, rows_j, :]) * dw[j:j + 1]
        c_s[pl.ds(r0, CONF_ROWS), :] = acc
        return 0

    lax.fori_loop(0, tt // CONF_ROWS, rows, 0)
    c = c_s[...]
    mu = jnp.mean(c, axis=-1, keepdims=True)
    xc = c - mu
    var = jnp.mean(xc * xc, axis=-1, keepdims=True)
    c = _silu(xc * lax.rsqrt(var + EPS) * lnw_ref[...] + lnb_ref[...])
    mixed = jnp.concatenate([o_ref[0], c.astype(BF16)], axis=1)
    out_ref[...] = h_ref[...] + _dot(mixed, wout_ref[...])


def _conf_out_seq(glu, o, h, cstate, dw, dwb, lnw, lnb, wout):
    b, seq, _ = glu.shape
    tt = _pick_tile(seq, (688, 512, 256, 128, 64, 16))
    per_seq = seq // tt
    tile = lambda w: pl.BlockSpec((1, tt, w), lambda i, t: (i, t, 0))
    flat = pl.BlockSpec((tt, D_MODEL), lambda i, t: (i * per_seq + t, 0))
    fixed = lambda a: pl.BlockSpec(a.shape, lambda i, t: (0,) * a.ndim)
    return pl.pallas_call(
        functools.partial(_conf_out_seq_kernel, tt=tt),
        grid=(b, per_seq),
        in_specs=[tile(CONF_WIDTH), tile(GDN_WIDTH), flat,
                  pl.BlockSpec((1, CONF_CONV - 1, CONF_WIDTH), lambda i, t: (i, 0, 0)),
                  fixed(dw), fixed(dwb), fixed(lnw), fixed(lnb), fixed(wout)],
        out_specs=flat,
        out_shape=jax.ShapeDtypeStruct(h.shape, F32),
        input_output_aliases={2: 0},
        scratch_shapes=[pltpu.VMEM((tt + CONF_HALO, CONF_WIDTH), F32),
                        pltpu.VMEM((tt, CONF_WIDTH), F32),
                        pltpu.VMEM((SUBLANES - 1, CONF_ROWS + CONF_HALO - SUBLANES, CONF_WIDTH), F32)],
        compiler_params=_params("parallel", "arbitrary"),
        name="conf_out_seq",
    )(glu, o, h, cstate, dw, dwb, lnw, lnb, wout)


def _conf_out_step_kernel(cext_ref, o_ref, h_ref, dw_ref, dwb_ref, lnw_ref, lnb_ref, wout_ref, out_ref,
                          *, steps):
    t = pl.program_id(0)

    @pl.when(t < steps)
    def _():
        dw = dw_ref[...]
        c = dwb_ref[...] + cext_ref[t] * dw[0:1]
        for j in range(1, CONF_CONV):
            c = c + cext_ref[t + j] * dw[j:j + 1]
        mu = jnp.mean(c, axis=-1, keepdims=True)
        xc = c - mu
        var = jnp.mean(xc * xc, axis=-1, keepdims=True)
        c = _silu(xc * lax.rsqrt(var + EPS) * lnw_ref[...] + lnb_ref[...])
        mixed = jnp.concatenate([o_ref[...], c.astype(BF16)], axis=1)
        out_ref[...] = h_ref[...] + _dot(mixed, wout_ref[...])

    @pl.when(t >= steps)
    def _():
        out_ref[...] = jnp.zeros_like(out_ref)


def _conf_out_step(cext, o, h, row0, n_zero, dw, dwb, lnw, lnb, wout):
    steps = cext.shape[0] - (CONF_CONV - 1)
    b = cext.shape[1]
    assert row0 % b == 0 and n_zero % b == 0 and row0 + steps * b + n_zero <= h.shape[0]
    full = lambda a: pl.BlockSpec(a.shape, lambda t: (0,) * a.ndim)
    rows = pl.BlockSpec((b, D_MODEL), lambda t: (row0 // b + t, 0))
    return pl.pallas_call(
        functools.partial(_conf_out_step_kernel, steps=steps),
        grid=(steps + n_zero // b,),
        in_specs=[full(cext), pl.BlockSpec((b, GDN_WIDTH), lambda t: (jnp.minimum(t, steps - 1), 0)), rows,
                  full(dw), full(dwb), full(lnw), full(lnb), full(wout)],
        out_specs=rows,
        out_shape=jax.ShapeDtypeStruct(h.shape, F32),
        input_output_aliases={2: 0},
        compiler_params=_params("arbitrary"),
        name="conf_out_step",
    )(cext, o, h, dw, dwb, lnw, lnb, wout)


N_RANKS = PEER_TOPK + 1
RANK_ROWS = 24
NEG_INF = float("-inf")


def _top_rows(s, count):
    rows = []
    for r in range(count):
        m = jnp.max(s, axis=0, keepdims=True)
        rows.append(m)
        s = jnp.where(s == m, NEG_INF, s)
    return rows


TOP_WAYS = 4


def _top_rows_chained(s, count):
    n = s.shape[0] // TOP_WAYS
    lv = [s[i * n:(i + 1) * n] for i in range(TOP_WAYS)]
    for i, j in ((0, 1), (2, 3), (0, 2), (1, 3), (1, 2)):
        lv[i], lv[j] = jnp.maximum(lv[i], lv[j]), jnp.minimum(lv[i], lv[j])
    rows = []
    for r in range(count):
        m = jnp.max(lv[0], axis=0, keepdims=True)
        rows.append(m)
        hit = lv[0] == m
        for i in range(TOP_WAYS - 1):
            lv[i] = jnp.where(hit, lv[i + 1], lv[i])
        lv[-1] = jnp.where(hit, NEG_INF, lv[-1])
    return rows


def _prefix_count(hits):
    out = jnp.zeros(hits[0].shape, F32)
    for r, hit in enumerate(hits):
        out = jnp.where(hit, float(r + 1), out)
    return out


def _stack_rows(rows, height, tm):
    idx = lax.broadcasted_iota(jnp.int32, (height, tm), 0)
    out = jnp.full((height, tm), NEG_INF, F32)
    for r, row in enumerate(rows):
        out = jnp.where(idx == r, row, out)
    return out


def _peer_score_kernel(h_ref, nw_ref, wqt_ref, keys_ref, xnt_ref, cnt_ref, e1_ref, rank2_ref, e2_ref):
    tm = h_ref.shape[0]
    xn = _rmsnorm(h_ref[...], nw_ref[...])
    xnt = xn.T.astype(BF16)
    xnt_ref[...] = xnt
    qt = _dot(wqt_ref[...], xnt).astype(BF16)
    idx8 = lax.broadcasted_iota(jnp.int32, (SUBLANES, tm), 0)
    for hh in range(PEER_HEADS):
        s = []
        for p in range(2):
            r0 = (hh * 2 + p) * N_KEYS
            s.append(_dot(keys_ref[hh, p], qt[r0:r0 + N_KEYS, :]))
        a = _top_rows_chained(s[0], N_RANKS)
        b = _top_rows_chained(s[1], N_RANKS)
        a_st = _stack_rows(a, RANK_ROWS, tm)
        b_st = _stack_rows(b, RANK_ROWS, tm)
        cands = [a[0] + b_st,
                 jnp.where(lax.broadcasted_iota(jnp.int32, (RANK_ROWS, tm), 0) >= 1,
                           a_st + b[0], NEG_INF)]
        for r1 in range(1, N_RANKS):
            hi = N_RANKS // (r1 + 1) - 1
            if hi >= 1 and r1 <= 4:
                cands.append(jnp.where((idx8 >= 1) & (idx8 <= hi), a[r1] + b_st[0:SUBLANES], NEG_INF))
        cands.append(jnp.where((idx8 >= 5) & (idx8 <= N_RANKS // 2 - 1),
                               a_st[0:SUBLANES] + b[1], NEG_INF))
        cand = jnp.concatenate(cands, axis=0)
        top = _top_rows(cand, N_RANKS)
        z = jnp.zeros_like(top[0])
        for r in range(PEER_TOPK):
            z = z + jnp.exp(top[r] - top[0])
        thr = 0.5 * (top[PEER_TOPK - 1] + top[PEER_TOPK])
        cnt1 = _prefix_count([s[0] >= thr - b[r2] for r2 in range(PEER_TOPK)])
        rank2 = _prefix_count([s[1] < b[r] for r in range(PEER_TOPK)])
        cnt_ref[hh] = cnt1
        e1_ref[hh] = 0.5 * jnp.exp(s[0] - a[0]) / z
        rank2_ref[hh] = rank2.astype(BF16)
        e2_ref[hh] = jnp.exp(s[1] - b[0]).astype(BF16)


def _peer_score(h, nw, wqt, keys):
    n = h.shape[0]
    tm = _pick_tile(n, (PEER_TOKENS, 256, 128))
    heads = lambda: pl.BlockSpec((PEER_HEADS, N_KEYS, tm), lambda i: (0, 0, i))
    return pl.pallas_call(
        _peer_score_kernel,
        grid=(n // tm,),
        in_specs=[pl.BlockSpec((tm, D_MODEL), lambda i: (i, 0)),
                  pl.BlockSpec((1, D_MODEL), lambda i: (0, 0)),
                  pl.BlockSpec(wqt.shape, lambda i: (0, 0)),
                  pl.BlockSpec(keys.shape, lambda i: (0, 0, 0, 0))],
        out_specs=[pl.BlockSpec((D_MODEL, tm), lambda i: (0, i)), heads(), heads(), heads(), heads()],
        out_shape=[jax.ShapeDtypeStruct((D_MODEL, n), BF16)]
        + [jax.ShapeDtypeStruct((PEER_HEADS, N_KEYS, n), dt) for dt in (F32, F32, BF16, BF16)],
        compiler_params=_params("parallel"),
        name="peer_score",
    )(h, nw, wqt, keys)


EXPERT_CHUNK = 1024
PEER_TOKENS = 512
PEER_DENSE_TOKENS = 512
SQRT_HALF = math.sqrt(0.5)


def _peer_dense_kernel(xnt_ref, cnt_ref, e1_ref, rank2_ref, e2_ref, u_ref, vt_ref, h_ref, fw_ref,
                       out_ref, acc_s, w_s, gate_s, cnt_s, e1_s, *, final_norm, n_chunks):
    s = pl.program_id(0)
    c = lax.rem(s, n_chunks)
    cur = lax.rem(c, 2)
    tm = xnt_ref.shape[1]
    half = tm // 2

    @pl.when(s == 0)
    def _():
        acc_s[...] = jnp.zeros_like(acc_s)
        w_s[1] = jnp.zeros(w_s.shape[1:], BF16)

    prev = 1 - cur
    n_blocks = EXPERT_CHUNK // N_KEYS
    n_lt = tm // LANES
    per_lt = PEER_HEADS * n_blocks
    for l in range(n_lt):
        cnt_s[l * per_lt:(l + 1) * per_lt, :] = cnt_ref[:, :, l * LANES:(l + 1) * LANES].reshape(per_lt, LANES)
        e1_s[l * per_lt:(l + 1) * per_lt, :] = e1_ref[:, :, l * LANES:(l + 1) * LANES].reshape(per_lt, LANES)

    def rows(ref, hh, j):
        words = jnp.concatenate([jnp.broadcast_to(ref[pl.ds(l * per_lt + hh * n_blocks + j, 1), :], (SUBLANES, LANES))
                                 for l in range(n_lt)], axis=1)
        packed = jnp.concatenate([words, words], axis=0).astype(BF16)
        return jnp.concatenate([packed] * (N_KEYS // (2 * SUBLANES)), axis=0)

    xnt = xnt_ref[...]
    block = lambda j: slice(j * N_KEYS, (j + 1) * N_KEYS)

    def gate(j):
        total = None
        for hh in range(PEER_HEADS):
            e2 = e2_ref[hh]
            sel = jnp.where(rank2_ref[hh] < rows(cnt_s, hh, j), e2, jnp.zeros_like(e2))
            term = rows(e1_s, hh, j) * sel
            total = term if total is None else total + term
        gate_s[block(j), :] = total

    def project(j):
        return _dot(u_ref[block(j), :], xnt)

    def finish(j, pre):
        act = (pre * (1.0 + lax.erf(pre * SQRT_HALF))).astype(BF16)
        w_s[cur, block(j), :] = act * gate_s[block(j), :]

    ahead = 3
    pres = {j: project(j) for j in range(ahead)}
    gate(0)
    for j in range(n_blocks):
        if j + ahead < n_blocks:
            pres[j + ahead] = project(j + ahead)
        if j == 0:
            acc_s[:, :half] += _dot(vt_ref[0], w_s[prev, :, :half])
        if j == n_blocks // 2:
            acc_s[:, half:] += _dot(vt_ref[0], w_s[prev, :, half:])
        if j + 1 < n_blocks:
            gate(j + 1)
        finish(j, pres.pop(j))

    @pl.when(c == 0)
    def _():
        y = h_ref[...] + acc_s[...].T
        if final_norm:
            y = _rmsnorm(y, fw_ref[...])
        out_ref[...] = y
        acc_s[...] = jnp.zeros_like(acc_s)


def _peer_dense(xnt, cnt, e1, rank2, e2, u, vt, h, fw, *, layer, final_norm):
    n = h.shape[0]
    tm = _pick_tile(n, (PEER_DENSE_TOKENS, PEER_TOKENS, 256, 128))
    n_tiles = n // tm
    n_chunks = u.shape[1] // EXPERT_CHUNK
    assert n_chunks % 2 == 0
    rows_per_chunk = EXPERT_CHUNK // N_KEYS
    tile = lambda s: jnp.minimum(s // n_chunks, n_tiles - 1)
    chunk = lambda s: lax.rem(s, n_chunks)
    done = lambda s: jnp.maximum(s - 1, 0) // n_chunks
    by_chunk = lambda: pl.BlockSpec((PEER_HEADS, rows_per_chunk, tm), lambda s: (0, chunk(s), tile(s)))
    by_tile = lambda: pl.BlockSpec((PEER_HEADS, N_KEYS, tm), lambda s: (0, 0, tile(s)))
    return pl.pallas_call(
        functools.partial(_peer_dense_kernel, final_norm=final_norm, n_chunks=n_chunks),
        grid=(n_tiles * n_chunks + 1,),
        in_specs=[pl.BlockSpec((D_MODEL, tm), lambda s: (0, tile(s))),
                  by_chunk(), by_chunk(), by_tile(), by_tile(),
                  pl.BlockSpec((None, EXPERT_CHUNK, D_MODEL), lambda s: (layer, chunk(s), 0)),
                  pl.BlockSpec((None, 1, D_MODEL, EXPERT_CHUNK), lambda s: (layer, chunk(s + n_chunks - 1), 0, 0)),
                  pl.BlockSpec((tm, D_MODEL), lambda s: (done(s), 0)),
                  pl.BlockSpec((1, D_MODEL), lambda s: (0, 0))],
        out_specs=pl.BlockSpec((tm, D_MODEL), lambda s: (done(s), 0)),
        out_shape=jax.ShapeDtypeStruct((n, D_MODEL), F32),
        scratch_shapes=[pltpu.VMEM((D_MODEL, tm), F32), pltpu.VMEM((2, EXPERT_CHUNK, tm), BF16),
                        pltpu.VMEM((EXPERT_CHUNK, tm), BF16)]
        + [pltpu.VMEM((PEER_HEADS * rows_per_chunk * tm // LANES, LANES), F32)] * 2,
        compiler_params=_params("arbitrary"),
        name="peer_dense",
    )(xnt, cnt, e1, rank2, e2, u, vt, h, fw)


def _prep_layer(layer, norm_mix_w, w_in, conv_qkv_w, a_log, dt_bias, gdn_norm_w, conf_dw_w, conf_dw_b,
                conf_ln_w, conf_ln_b, w_out, norm_ffn_w, w_query, sub_keys):
    s0 = QKV_WIDTH
    s1 = s0 + GDN_WIDTH
    s2 = s1 + GDN_HEADS
    s3 = s2 + GDN_HEADS
    w = w_in[layer]
    row = lambda v: v.reshape(1, -1)
    return dict(
        norm_mix_w=row(norm_mix_w[layer]),
        wqkvz=w[:, :s1].astype(BF16),
        wab=jnp.pad(w[:, s1:s3], ((0, 0), (0, LANES - 2 * GDN_HEADS))).astype(BF16),
        wglu=w[:, s3:].astype(BF16),
        alog=row(jnp.pad(a_log[layer], (0, LANES - GDN_HEADS))),
        dtb=row(jnp.pad(dt_bias[layer], (0, LANES - GDN_HEADS))),
        conv_w=conv_qkv_w[layer],
        gdn_norm_w=row(gdn_norm_w[layer]),
        dw=conf_dw_w[layer], dwb=row(conf_dw_b[layer]),
        lnw=row(conf_ln_w[layer]), lnb=row(conf_ln_b[layer]),
        wout=w_out[layer].astype(BF16),
        norm_ffn_w=row(norm_ffn_w[layer]),
        wqt=w_query[layer].T.astype(BF16),
        keys=sub_keys[layer].astype(BF16),
    )


def _peer(h, lw, u, vt, fw, layer, final_norm):
    xnt, cnt, e1, rank2, e2 = _peer_score(h, lw["norm_ffn_w"], lw["wqt"], lw["keys"])
    return _peer_dense(xnt, cnt, e1, rank2, e2, u, vt, h, fw, layer=layer, final_norm=final_norm)


def _mix_prompt(h, lw, b, seq):
    n_pad = (-seq) % GDN_CHUNK
    qkv, z, gb, glu = _in_proj(h, lw["norm_mix_w"], lw["wqkvz"], lw["wab"], lw["wglu"], lw["alog"], lw["dtb"],
                               n_rows=b * seq)
    shp = lambda a: a.reshape(b, seq, a.shape[-1])
    qkv, z, gb, glu = shp(qkv), shp(z), shp(gb), shp(glu)
    o, s_new = _gdn(qkv, gb, z,
                    jnp.zeros((b, GDN_CONV - 1, QKV_WIDTH), F32),
                    jnp.zeros((b, GDN_HEADS, HEAD_DIM, HEAD_DIM), F32),
                    lw["conv_w"], lw["gdn_norm_w"], chunk=GDN_CHUNK, n_pad=n_pad, bb_count=1)
    h = _conf_out_seq(glu, o, h, jnp.zeros((b, CONF_CONV - 1, CONF_WIDTH), F32),
                      lw["dw"], lw["dwb"], lw["lnw"], lw["lnb"], lw["wout"])
    return h, s_new, qkv[:, seq - (GDN_CONV - 1):], glu[:, seq - (CONF_CONV - 1):]


def _mix_sample(h, row0, n_zero, lw, b, steps, s0, qkv_state, conf_state):
    n_pad = (-steps) % SUBLANES
    to_tb = lambda a: jnp.swapaxes(a, 0, 1)
    qkv, z, gb, glu = _in_proj(h[row0:row0 + steps * b], lw["norm_mix_w"], lw["wqkvz"], lw["wab"], lw["wglu"],
                               lw["alog"], lw["dtb"])
    bt = lambda a: to_tb(a.reshape(steps, b, a.shape[-1]))
    qkv_bt = bt(qkv)
    o, s_new = _gdn(qkv_bt, bt(gb), bt(z), qkv_state, s0,
                    lw["conv_w"], lw["gdn_norm_w"], chunk=steps + n_pad, n_pad=n_pad, bb_count=8)
    cext = jnp.concatenate([to_tb(conf_state), glu.reshape(steps, b, CONF_WIDTH)], axis=0)
    h = _conf_out_step(cext, to_tb(o).reshape(steps * b, GDN_WIDTH), h, row0, n_zero,
                       lw["dw"], lw["dwb"], lw["lnw"], lw["lnb"], lw["wout"])
    qkv_ext = jnp.concatenate([qkv_state, qkv_bt], axis=1)
    return (h, s_new, qkv_ext[:, qkv_ext.shape[1] - (GDN_CONV - 1):],
            to_tb(cext[cext.shape[0] - (CONF_CONV - 1):]))


DROP_ROWS = 512


def _drop_meta_kernel(h_ref, y_ref):
    y_ref[0] = h_ref[...]


def _drop_meta(h, b, seq, n_meta):
    real = seq - n_meta
    rows = _pick_tile(real, (DROP_ROWS, 256, 128, 64, 16))
    assert seq % SUBLANES == 0 and n_meta % SUBLANES == 0 and rows % SUBLANES == 0
    source = pl.BlockSpec((pl.Element(rows), pl.Element(D_MODEL)),
                          lambda i, j: (pl.multiple_of(i * seq + n_meta + j * rows, SUBLANES), 0))
    return pl.pallas_call(
        _drop_meta_kernel,
        grid=(b, real // rows),
        in_specs=[source],
        out_specs=pl.BlockSpec((1, rows, D_MODEL), lambda i, j: (i, j, 0)),
        out_shape=jax.ShapeDtypeStruct((b, real, D_MODEL), h.dtype),
        compiler_params=_params("parallel", "parallel"),
        name="drop_meta",
    )(h)


def kernel(x_prompt, x_sample, state_gdn, state_qkv_conv, state_conf_conv, meta_tokens, norm_mix_w, w_in,
           conv_qkv_w, a_log, dt_bias, gdn_norm_w, conf_dw_w, conf_dw_b, conf_ln_w, conf_ln_b, w_out,
           norm_ffn_w, w_query, sub_keys, expert_u, expert_v, final_norm_w):
    depth = w_in.shape[0]
    layers = [_prep_layer(l, norm_mix_w, w_in, conv_qkv_w, a_log, dt_bias, gdn_norm_w, conf_dw_w,
                          conf_dw_b, conf_ln_w, conf_ln_b, w_out, norm_ffn_w, w_query, sub_keys)
              for l in range(depth)]
    u_all = expert_u.astype(BF16)
    vt_all = jnp.swapaxes(expert_v.astype(BF16).reshape(depth, -1, EXPERT_CHUNK, D_MODEL), 2, 3)
    fw = final_norm_w.reshape(1, D_MODEL)
    b_p, seq0, _ = x_prompt.shape
    seq = seq0 + N_META
    b_s, steps, _ = x_sample.shape
    n_p, n_s = b_p * seq, b_s * steps
    n_fill = (-(n_p + n_s)) % PEER_DENSE_TOKENS
    meta = jnp.broadcast_to(meta_tokens[None], (b_p, N_META, D_MODEL))
    h = jnp.concatenate([jnp.concatenate([meta, x_prompt], axis=1).reshape(n_p, D_MODEL),
                         jnp.swapaxes(x_sample, 0, 1).reshape(n_s, D_MODEL),
                         jnp.zeros((n_fill, D_MODEL), F32)], axis=0)
    outs_p, outs_s = [], []
    for li, lw in enumerate(layers):
        h, *st_p = _mix_prompt(h, lw, b_p, seq)
        h, *st_s = _mix_sample(h, n_p, n_fill, lw, b_s, steps, state_gdn[li], state_qkv_conv[li],
                               state_conf_conv[li])
        outs_p.append(st_p)
        outs_s.append(st_s)
        h = _peer(h, lw, u_all, vt_all, fw, li, li == depth - 1)
    y_p = _drop_meta(h, b_p, seq, N_META)
    y_s = jnp.swapaxes(h[n_p:n_p + n_s].reshape(steps, b_s, D_MODEL), 0, 1)
    stack = lambda outs, i: jnp.stack([o[i] for o in outs])
    return (y_p, y_s, stack(outs_p, 0), stack(outs_p, 1), stack(outs_p, 2),
            stack(outs_s, 0), stack(outs_s, 1), stack(outs_s, 2))
```

```python
import functools
import math

import jax
import jax.numpy as jnp
from jax import lax
from jax.experimental import pallas as pl
from jax.experimental.pallas import tpu as pltpu

F32 = jnp.float32
BF16 = jnp.bfloat16

D_MODEL = 1024
N_META = 16
GDN_HEADS = 4
HEAD_DIM = 128
GDN_WIDTH = GDN_HEADS * HEAD_DIM
CONF_WIDTH = 512
QKV_WIDTH = 3 * GDN_WIDTH
GDN_CONV = 4
GDN_CHUNK = 64
GDN_GROUP = 8
CONF_CONV = 31
N_KEYS = 128
PEER_HEADS = 8
PEER_TOPK = 16
EPS = 1e-6

LANES = 128
SUBLANES = 8
VMEM_LIMIT_BYTES = 56 * 1024 * 1024


def _sigmoid(x):
    return 1.0 / (1.0 + jnp.exp(-x))


def _silu(x):
    return x * _sigmoid(x)


def _softplus(x):
    return jnp.maximum(x, 0.0) + jnp.log1p(jnp.exp(-jnp.abs(x)))


def _rmsnorm(x, w):
    return x * lax.rsqrt(jnp.mean(x * x, axis=-1, keepdims=True) + EPS) * w


def _dot(a, b, precision=None):
    return jnp.dot(a, b, preferred_element_type=F32, precision=precision)


def _dot_nt(a, b, precision=None):
    return lax.dot_general(a, b, (((1,), (1,)), ((), ())), preferred_element_type=F32,
                           precision=precision)


def _dot_tn(a, b, precision=None):
    return lax.dot_general(a, b, (((0,), (0,)), ((), ())), preferred_element_type=F32,
                           precision=precision)


def _split(a):
    hi = a.astype(BF16)
    return hi, a - hi.astype(F32)


def _dot3(a, b):
    a_hi, a_rest = _split(a)
    b_hi, b_rest = _split(b)
    return _dot(a_hi, b_hi) + (_dot(a_hi, b_rest.astype(BF16)) + _dot(a_rest.astype(BF16), b_hi))


def _dot_exact_lhs(a_bf16, b):
    b0, r = _split(b)
    b1, r = _split(r)
    return _dot(a_bf16, b0) + (_dot(a_bf16, b1) + _dot(a_bf16, r.astype(BF16)))


def _pick_tile(n, candidates):
    for c in candidates:
        if n % c == 0:
            return c
    return n


def _params(*semantics):
    return pltpu.CompilerParams(dimension_semantics=semantics, vmem_limit_bytes=VMEM_LIMIT_BYTES)


def _in_proj_kernel(h_ref, nw_ref, wqkvz_ref, wab_ref, wglu_ref, alog_ref, dtb_ref,
                    qkv_ref, z_ref, gb_ref, glu_ref):
    xb = _rmsnorm(h_ref[...], nw_ref[...]).astype(BF16)
    p = _dot(xb, wqkvz_ref[...])
    qkv_ref[...] = p[:, :QKV_WIDTH].astype(qkv_ref.dtype)
    z_ref[...] = p[:, QKV_WIDTH:].astype(z_ref.dtype)
    ab = _dot(xb, wab_ref[...])
    g = -jnp.exp(alog_ref[...]) * _softplus(ab + dtb_ref[...])
    lane = lax.broadcasted_iota(jnp.int32, ab.shape, 1)
    gb_ref[...] = jnp.where(lane < GDN_HEADS, g, _sigmoid(ab))
    pg = _dot(xb, wglu_ref[...])
    glu_ref[...] = (pg[:, :CONF_WIDTH] * _sigmoid(pg[:, CONF_WIDTH:])).astype(glu_ref.dtype)


def _in_proj(h, nw, wqkvz, wab, wglu, alog, dtb, n_rows=None, act_dtype=F32):
    n = h.shape[0] if n_rows is None else n_rows
    tm = _pick_tile(n, (384, 256, 128))
    row = lambda i: (i, 0)
    fixed = lambda i: (0, 0)
    widths = (QKV_WIDTH, GDN_WIDTH, LANES, CONF_WIDTH)
    return pl.pallas_call(
        _in_proj_kernel,
        grid=(n // tm,),
        in_specs=[
            pl.BlockSpec((tm, D_MODEL), row),
            pl.BlockSpec((1, D_MODEL), fixed),
            pl.BlockSpec(wqkvz.shape, fixed),
            pl.BlockSpec(wab.shape, fixed),
            pl.BlockSpec(wglu.shape, fixed),
            pl.BlockSpec((1, LANES), fixed),
            pl.BlockSpec((1, LANES), fixed),
        ],
        out_specs=[pl.BlockSpec((tm, w), row) for w in widths],
        out_shape=[jax.ShapeDtypeStruct((n, w), F32 if w == LANES else act_dtype) for w in widths],
        compiler_params=_params("parallel"),
        name="in_proj",
    )(h, nw, wqkvz, wab, wglu, alog, dtb)


def _gdn_kernel(xq_ref, xk_ref, xv_ref, gb_ref, z_ref, cs_q_ref, cs_k_ref, cs_v_ref, s0_ref,
                cwq_ref, cwk_ref, cwv_ref, nw_ref,
                o_ref, s_ref,
                xq_s, xk_s, xv_s, g_s, b_s, u_s, att_s, wq_s, m_s, bm_s, gl_s, sall_s,
                *, chunk, n_pad, seq, bb_count):
    C = chunk
    n_chunks = (seq + n_pad) // C
    head = 8 + n_pad
    n_fac = int(math.log2(C))
    row_i = lax.broadcasted_iota(jnp.int32, (C, C), 0)
    col_i = lax.broadcasted_iota(jnp.int32, (C, C), 1)
    incl = row_i >= col_i
    strict = row_i > col_i
    tril_b = jnp.where(incl, 1.0, 0.0).astype(BF16)
    src_lane = lax.broadcasted_iota(jnp.int32, (LANES, HEAD_DIM), 0)
    pick_gb = jnp.concatenate([jnp.where(src_lane == pl.program_id(1) + off, 1.0, 0.0).astype(BF16)
                               for off in (0, GDN_HEADS)], axis=1)
    eye =jnp.where(row_i == col_i, 1.0, 0.0).astype(F32)
    group = GDN_GROUP
    live = lax.broadcasted_iota(jnp.int32, (C, HEAD_DIM), 0) >= n_pad
    cws = (cwq_ref[...], cwk_ref[...], cwv_ref[...])
    nw = nw_ref[...]

    def conv(blk, cw):
        acc = blk[5:5 + C] * cw[0:1]
        for j in range(1, GDN_CONV):
            acc = acc + blk[5 + j:5 + j + C] * cw[j:j + 1]
        return _silu(acc)

    slot = lambda bb, c: bb * n_chunks + c

    is_first = lambda c: isinstance(c, int) and c == 0

    def build(items, fill=()):
        fill = list(fill)
        tick = lambda: fill.pop(0)() if fill else None
        each = lambda f, *ls: [f(*a) for a in zip(*ls)]
        at = [(bb, c * C if isinstance(c, int) else pl.multiple_of(c * C, SUBLANES)) for bb, c in items]
        qc = [conv(xq_s[bb, pl.ds(t0, C + 8), :], cws[0]) for bb, t0 in at]
        kc = [conv(xk_s[bb, pl.ds(t0, C + 8), :], cws[1]) for bb, t0 in at]
        v = [conv(xv_s[bb, pl.ds(t0, C + 8), :], cws[2]) for bb, t0 in at]
        q = each(lambda a: a * lax.rsqrt(jnp.sum(a * a, axis=-1, keepdims=True) + EPS) * (HEAD_DIM ** -0.5), qc)
        k = each(lambda a: a * lax.rsqrt(jnp.sum(a * a, axis=-1, keepdims=True) + EPS), kc)
        g = [g_s[bb, pl.ds(t0, C), :] for bb, t0 in at]
        beta = [b_s[bb, pl.ds(t0, C), :] for bb, t0 in at]
        if n_pad:
            q, k, v = ([jnp.where(live, a, 0.0) if is_first(c) else a for a, (_, c) in zip(l, items)]
                       for l in (q, k, v))
        cum = each(lambda a: _dot_exact_lhs(tril_b, jnp.concatenate([a, jnp.where(strict, a[:, :C], 0.0)], axis=1)), g)
        tick()
        gcum = [a[:, :HEAD_DIM] for a in cum]
        decay = [jnp.where(incl, jnp.exp(a[:, HEAD_DIM:]), 0.0) for a in cum]
        eg = [jnp.exp(a) for a in gcum]
        kb = each(lambda a, b_: a * b_, k, beta)
        kq = each(lambda a, b_, c_: _dot_nt(jnp.concatenate([a, b_], axis=0).astype(BF16), c_.astype(BF16)), kb, q, k)
        tick()
        p = each(lambda a, d: jnp.where(strict, -(a[:C] * d), 0.0), kq, decay)
        att = each(lambda a, d: jnp.where(incl, a[C:] * d, 0.0), kq, decay)
        t = [eye + a for a in p]
        p = each(lambda a: _dot3(a, a), p)
        tick()
        for step in range(1, n_fac):
            tick()
            if step == n_fac - 1:
                t = each(lambda a, b_: a + _dot3(b_, a), t, p)
            else:
                y = each(lambda a, b_: _dot3(b_, jnp.concatenate([a, b_], axis=1)), t, p)
                t = each(lambda a, b_: a + b_[:, :C], t, y)
                p = [a[:, C:] for a in y]
        x = each(lambda t_, v_, b_, kb_, eg_: _dot3(t_, jnp.concatenate([v_ * b_, kb_ * eg_], axis=1)),
                 t, v, beta, kb, eg)
        while fill:
            tick()
        g_last = [a[C - 1:C, :] for a in gcum]
        kt = each(lambda k_, gl, gc: k_ * jnp.exp(gl - gc), k, g_last, gcum)
        ktx = each(lambda a, b_: _dot_tn(a.astype(BF16), b_.astype(BF16)), kt, x)
        for i, (bb, c) in enumerate(items):
            n = slot(bb, c)
            u_s[n] = x[i][:, :HEAD_DIM]
            att_s[n] = att[i].astype(BF16)
            wq_s[n] = jnp.concatenate([x[i][:, HEAD_DIM:], q[i] * eg[i]], axis=0).astype(BF16)
            bm_s[n] = ktx[i][:, :HEAD_DIM]
            m_s[n] = (-ktx[i][:, HEAD_DIM:]).astype(BF16)
            gl_s[n] = jnp.exp(g_last[i])

    def carry(items):
        s = [s_ref[bb, 0] for bb, _ in items]
        for (bb, c), s_i in zip(items, s):
            sall_s[slot(bb, c)] = s_i
        new = [gl_s[slot(bb, c)] * s_i + _dot(m_s[slot(bb, c)], s_i.astype(BF16)) + bm_s[slot(bb, c)]
               for (bb, c), s_i in zip(items, s)]
        for (bb, _), s_i in zip(items, new):
            s_ref[bb, 0] = s_i

    def emit(items):
        ns = [slot(bb, c) for bb, c in items]
        ws_qs = [_dot(wq_s[n], sall_s[n].astype(BF16)) for n in ns]
        v_new = [u_s[n] - a[:C] for n, a in zip(ns, ws_qs)]
        o = [a[C:] + _dot(att_s[n], vn.astype(BF16)) for n, a, vn in zip(ns, ws_qs, v_new)]
        o = [_rmsnorm(a, nw) for a in o]
        for (bb, c), a in zip(items, o):
            if is_first(c):
                rows = C - n_pad
                o_ref[bb, 0:rows, :] = (a[n_pad:] * _silu(z_ref[bb, 0:rows, :].astype(F32))).astype(BF16)
            else:
                r0 = c * C - n_pad if isinstance(c, int) else pl.multiple_of(c * C - n_pad, 2 * SUBLANES)
                o_ref[bb, pl.ds(r0, C), :] = (a * _silu(z_ref[bb, pl.ds(r0, C), :].astype(F32))).astype(BF16)

    lead = (n_chunks - 1) % group + 1
    if lead == 1 and n_chunks > 1:
        lead += group
    leading = [(bb, c) for bb in range(bb_count) for c in range(lead)]

    n_loop = (n_chunks - lead) // group
    later = lambda i: [(0, lead + i * group + j) for j in range(group)]
    carry_range = lambda lo, hi: lax.fori_loop(lo, hi, lambda c, _: (carry([(0, c)]), 0)[1], 0)

    for bb in range(bb_count):
        for x_s, x_ref, cs_ref in ((xq_s, xq_ref, cs_q_ref), (xk_s, xk_ref, cs_k_ref),
                                   (xv_s, xv_ref, cs_v_ref)):
            x_s[bb, 0:head, :] = jnp.zeros((head, HEAD_DIM), F32)
            x_s[bb, head - (GDN_CONV - 1):head, :] = cs_ref[bb]
            x_s[bb, head:head + seq, :] = x_ref[bb].astype(F32)
        if n_pad:
            g_s[bb, 0:n_pad, :] = jnp.zeros((n_pad, HEAD_DIM), F32)
            b_s[bb, 0:n_pad, :] = jnp.zeros((n_pad, HEAD_DIM), F32)
        pieces = []
        rest = gb_ref[bb]
        for _ in range(3):
            hi, rest = _split(rest)
            pieces.append(hi)
        both = _dot(pieces[0], pick_gb) + (_dot(pieces[1], pick_gb) + _dot(pieces[2], pick_gb))
        g_s[bb, n_pad:n_pad + seq, :] = both[:, :HEAD_DIM]
        b_s[bb, n_pad:n_pad + seq, :] = both[:, HEAD_DIM:]
        s_ref[bb, 0] = s0_ref[bb, 0]

    build(leading)
    carry([(bb, 0) for bb in range(bb_count)])

    def build_later(i, _):
        build(later(i), [functools.partial(carry, [(0, 1 + i * group + j)]) for j in range(group)])
        return 0

    if n_loop:
        lax.fori_loop(0, n_loop, build_later, 0)
    if n_chunks > 1:
        carry_range(1 + n_loop * group, n_chunks)
    emit(leading)
    if n_loop:
        lax.fori_loop(0, n_loop, lambda i, _: (emit(later(i)), 0)[1], 0)


def _gdn(qkv, gb, z, conv_state, s0, conv_w, norm_w, *, chunk, n_pad, bb_count):
    b, seq, _ = qkv.shape
    n_chunks = (seq + n_pad) // chunk
    assert n_chunks * chunk == seq + n_pad and b % bb_count == 0
    assert n_chunks == 1 or bb_count == 1
    col = lambda off: (lambda i, h: (i, 0, off + h))
    wcol = lambda off: (lambda i, h: (0, off + h))
    seq_blk = (bb_count, seq, HEAD_DIM)
    cs_blk = (bb_count, GDN_CONV - 1, HEAD_DIM)
    s_blk = (bb_count, 1, HEAD_DIM, HEAD_DIM)
    s_map = lambda i, h: (i, h, 0, 0)
    ext = chunk * n_chunks
    per_chunk = lambda r, c, dt=F32: pltpu.VMEM((bb_count * n_chunks, r, c), dt)
    kern = functools.partial(_gdn_kernel, chunk=chunk, n_pad=n_pad, seq=seq, bb_count=bb_count)
    return pl.pallas_call(
        kern,
        grid=(b // bb_count, GDN_HEADS),
        in_specs=[
            pl.BlockSpec(seq_blk, col(0)),
            pl.BlockSpec(seq_blk, col(GDN_HEADS)),
            pl.BlockSpec(seq_blk, col(2 * GDN_HEADS)),
            pl.BlockSpec((bb_count, seq, LANES), lambda i, h: (i, 0, 0)),
            pl.BlockSpec(seq_blk, col(0)),
            pl.BlockSpec(cs_blk, col(0)),
            pl.BlockSpec(cs_blk, col(GDN_HEADS)),
            pl.BlockSpec(cs_blk, col(2 * GDN_HEADS)),
            pl.BlockSpec(s_blk, s_map),
            pl.BlockSpec((GDN_CONV, HEAD_DIM), wcol(0)),
            pl.BlockSpec((GDN_CONV, HEAD_DIM), wcol(GDN_HEADS)),
            pl.BlockSpec((GDN_CONV, HEAD_DIM), wcol(2 * GDN_HEADS)),
            pl.BlockSpec((1, HEAD_DIM), lambda i, h: (0, 0)),
        ],
        out_specs=[pl.BlockSpec(seq_blk, col(0)), pl.BlockSpec(s_blk, s_map)],
        out_shape=[jax.ShapeDtypeStruct((b, seq, GDN_WIDTH), BF16),
                   jax.ShapeDtypeStruct((b, GDN_HEADS, HEAD_DIM, HEAD_DIM), F32)],
        scratch_shapes=[
            pltpu.VMEM((bb_count, ext + 8, HEAD_DIM), F32),
            pltpu.VMEM((bb_count, ext + 8, HEAD_DIM), F32),
            pltpu.VMEM((bb_count, ext + 8, HEAD_DIM), F32),
            pltpu.VMEM((bb_count, ext, HEAD_DIM), F32),
            pltpu.VMEM((bb_count, ext, HEAD_DIM), F32),
            per_chunk(chunk, HEAD_DIM),
            per_chunk(chunk, chunk, BF16),
            per_chunk(2 * chunk, HEAD_DIM, BF16),
            per_chunk(HEAD_DIM, HEAD_DIM, BF16),
            per_chunk(HEAD_DIM, HEAD_DIM),
            per_chunk(1, HEAD_DIM),
            per_chunk(HEAD_DIM, HEAD_DIM),
        ],
        compiler_params=_params("parallel", "parallel"),
        name="gdn",
    )(qkv, qkv, qkv, gb, z, conv_state, conv_state, conv_state, s0,
      conv_w, conv_w, conv_w, norm_w)


CONF_HALO = 32
CONF_ROWS = 16


def _conf_out_seq_kernel(glu_ref, o_ref, h_ref, cstate_ref, dw_ref, dwb_ref, lnw_ref, lnb_ref, wout_ref,
                         out_ref, ext_s, c_s, sh_s, *, tt):
    t = pl.program_id(1)
    lead = CONF_HALO - (CONF_CONV - 1)

    @pl.when(t == 0)
    def _():
        ext_s[0:lead, :] = jnp.zeros((lead, CONF_WIDTH), F32)
        ext_s[lead:CONF_HALO, :] = cstate_ref[0]

    @pl.when(t > 0)
    def _():
        ext_s[0:CONF_HALO, :] = ext_s[tt:tt + CONF_HALO, :]

    ext_s[CONF_HALO:CONF_HALO + tt, :] = glu_ref[0].astype(F32)
    dw = dw_ref[...]
    bias = dwb_ref[...]

    def rows(i, _):
        r0 = pl.multiple_of(i * CONF_ROWS, SUBLANES)
        blk = ext_s[pl.ds(r0, CONF_ROWS + CONF_HALO), :]
        span = CONF_ROWS + CONF_HALO - SUBLANES
        for s in range(1, SUBLANES):
            sh_s[s - 1] = blk[s:s + span]
        acc = bias
        for j in range(CONF_CONV):
            q, s = divmod(lead + j, SUBLANES)
            rows_j = slice(q * SUBLANES, q * SUBLANES + CONF_ROWS)
            acc = acc + (blk[rows_j] if s == 0 else sh_s[s - 1, rows_j, :]) * dw[j:j + 1]
        c_s[pl.ds(r0, CONF_ROWS), :] = acc
        return 0

    lax.fori_loop(0, tt // CONF_ROWS, rows, 0)
    c = c_s[...]
    mu = jnp.mean(c, axis=-1, keepdims=True)
    xc = c - mu
    var = jnp.mean(xc * xc, axis=-1, keepdims=True)
    c = _silu(xc * lax.rsqrt(var + EPS) * lnw_ref[...] + lnb_ref[...])
    mixed = jnp.concatenate([o_ref[0], c.astype(BF16)], axis=1)
    out_ref[...] = h_ref[...] + _dot(mixed, wout_ref[...])


def _conf_out_seq(glu, o, h, cstate, dw, dwb, lnw, lnb, wout):
    b, seq, _ = glu.shape
    tt = _pick_tile(seq, (688, 512, 256, 128, 64, 16))
    per_seq = seq // tt
    tile = lambda w: pl.BlockSpec((1, tt, w), lambda i, t: (i, t, 0))
    flat = pl.BlockSpec((tt, D_MODEL), lambda i, t: (i * per_seq + t, 0))
    fixed = lambda a: pl.BlockSpec(a.shape, lambda i, t: (0,) * a.ndim)
    return pl.pallas_call(
        functools.partial(_conf_out_seq_kernel, tt=tt),
        grid=(b, per_seq),
        in_specs=[tile(CONF_WIDTH), tile(GDN_WIDTH), flat,
                  pl.BlockSpec((1, CONF_CONV - 1, CONF_WIDTH), lambda i, t: (i, 0, 0)),
                  fixed(dw), fixed(dwb), fixed(lnw), fixed(lnb), fixed(wout)],
        out_specs=flat,
        out_shape=jax.ShapeDtypeStruct(h.shape, F32),
        input_output_aliases={2: 0},
        scratch_shapes=[pltpu.VMEM((tt + CONF_HALO, CONF_WIDTH), F32),
                        pltpu.VMEM((tt, CONF_WIDTH), F32),
                        pltpu.VMEM((SUBLANES - 1, CONF_ROWS + CONF_HALO - SUBLANES, CONF_WIDTH), F32)],
        compiler_params=_params("parallel", "arbitrary"),
        name="conf_out_seq",
    )(glu, o, h, cstate, dw, dwb, lnw, lnb, wout)


def _conf_out_step_kernel(cext_ref, o_ref, h_ref, dw_ref, dwb_ref, lnw_ref, lnb_ref, wout_ref, out_ref,
                          *, steps):
    t = pl.program_id(0)

    @pl.when(t < steps)
    def _():
        dw = dw_ref[...]
        c = dwb_ref[...] + cext_ref[t] * dw[0:1]
        for j in range(1, CONF_CONV):
            c = c + cext_ref[t + j] * dw[j:j + 1]
        mu = jnp.mean(c, axis=-1, keepdims=True)
        xc = c - mu
        var = jnp.mean(xc * xc, axis=-1, keepdims=True)
        c = _silu(xc * lax.rsqrt(var + EPS) * lnw_ref[...] + lnb_ref[...])
        mixed = jnp.concatenate([o_ref[...], c.astype(BF16)], axis=1)
        out_ref[...] = h_ref[...] + _dot(mixed, wout_ref[...])

    @pl.when(t >= steps)
    def _():
        out_ref[...] = jnp.zeros_like(out_ref)


def _conf_out_step(cext, o, h, row0, n_zero, dw, dwb, lnw, lnb, wout):
    steps = cext.shape[0] - (CONF_CONV - 1)
    b = cext.shape[1]
    assert row0 % b == 0 and n_zero % b == 0 and row0 + steps * b + n_zero <= h.shape[0]
    full = lambda a: pl.BlockSpec(a.shape, lambda t: (0,) * a.ndim)
    rows = pl.BlockSpec((b, D_MODEL), lambda t: (row0 // b + t, 0))
    return pl.pallas_call(
        functools.partial(_conf_out_step_kernel, steps=steps),
        grid=(steps + n_zero // b,),
        in_specs=[full(cext), pl.BlockSpec((b, GDN_WIDTH), lambda t: (jnp.minimum(t, steps - 1), 0)), rows,
                  full(dw), full(dwb), full(lnw), full(lnb), full(wout)],
        out_specs=rows,
        out_shape=jax.ShapeDtypeStruct(h.shape, F32),
        input_output_aliases={2: 0},
        compiler_params=_params("arbitrary"),
        name="conf_out_step",
    )(cext, o, h, dw, dwb, lnw, lnb, wout)


N_RANKS = PEER_TOPK + 1
RANK_ROWS = 24
NEG_INF = float("-inf")


def _top_rows(s, count):
    rows = []
    for r in range(count):
        m = jnp.max(s, axis=0, keepdims=True)
        rows.append(m)
        s = jnp.where(s == m, NEG_INF, s)
    return rows


TOP_WAYS = 4


def _top_rows_chained(s, count):
    n = s.shape[0] // TOP_WAYS
    lv = [s[i * n:(i + 1) * n] for i in range(TOP_WAYS)]
    for i, j in ((0, 1), (2, 3), (0, 2), (1, 3), (1, 2)):
        lv[i], lv[j] = jnp.maximum(lv[i], lv[j]), jnp.minimum(lv[i], lv[j])
    rows = []
    for r in range(count):
        m = jnp.max(lv[0], axis=0, keepdims=True)
        rows.append(m)
        hit = lv[0] == m
        for i in range(TOP_WAYS - 1):
            lv[i] = jnp.where(hit, lv[i + 1], lv[i])
        lv[-1] = jnp.where(hit, NEG_INF, lv[-1])
    return rows


def _prefix_count(hits):
    out = jnp.zeros(hits[0].shape, F32)
    for r, hit in enumerate(hits):
        out = jnp.where(hit, float(r + 1), out)
    return out


def _stack_rows(rows, height, tm):
    idx = lax.broadcasted_iota(jnp.int32, (height, tm), 0)
    out = jnp.full((height, tm), NEG_INF, F32)
    for r, row in enumerate(rows):
        out = jnp.where(idx == r, row, out)
    return out


def _peer_score_kernel(h_ref, nw_ref, wqt_ref, keys_ref, xnt_ref, cnt_ref, e1_ref, rank2_ref, e2_ref):
    tm = h_ref.shape[0]
    xn = _rmsnorm(h_ref[...], nw_ref[...])
    xnt = xn.T.astype(BF16)
    xnt_ref[...] = xnt
    qt = _dot(wqt_ref[...], xnt).astype(BF16)
    idx8 = lax.broadcasted_iota(jnp.int32, (SUBLANES, tm), 0)
    for hh in range(PEER_HEADS):
        s = []
        for p in range(2):
            r0 = (hh * 2 + p) * N_KEYS
            s.append(_dot(keys_ref[hh, p], qt[r0:r0 + N_KEYS, :]))
        a = _top_rows_chained(s[0], N_RANKS)
        b = _top_rows_chained(s[1], N_RANKS)
        a_st = _stack_rows(a, RANK_ROWS, tm)
        b_st = _stack_rows(b, RANK_ROWS, tm)
        cands = [a[0] + b_st,
                 jnp.where(lax.broadcasted_iota(jnp.int32, (RANK_ROWS, tm), 0) >= 1,
                           a_st + b[0], NEG_INF)]
        for r1 in range(1, N_RANKS):
            hi = N_RANKS // (r1 + 1) - 1
            if hi >= 1 and r1 <= 4:
                cands.append(jnp.where((idx8 >= 1) & (idx8 <= hi), a[r1] + b_st[0:SUBLANES], NEG_INF))
        cands.append(jnp.where((idx8 >= 5) & (idx8 <= N_RANKS // 2 - 1),
                               a_st[0:SUBLANES] + b[1], NEG_INF))
        cand = jnp.concatenate(cands, axis=0)
        top = _top_rows(cand, N_RANKS)
        z = jnp.zeros_like(top[0])
        for r in range(PEER_TOPK):
            z = z + jnp.exp(top[r] - top[0])
        thr = 0.5 * (top[PEER_TOPK - 1] + top[PEER_TOPK])
        cnt1 = _prefix_count([s[0] >= thr - b[r2] for r2 in range(PEER_TOPK)])
        rank2 = _prefix_count([s[1] < b[r] for r in range(PEER_TOPK)])
        cnt_ref[hh] = cnt1
        e1_ref[hh] = 0.5 * jnp.exp(s[0] - a[0]) / z
        rank2_ref[hh] = rank2.astype(BF16)
        e2_ref[hh] = jnp.exp(s[1] - b[0]).astype(BF16)


def _peer_score(h, nw, wqt, keys):
    n = h.shape[0]
    tm = _pick_tile(n, (PEER_TOKENS, 256, 128))
    heads = lambda: pl.BlockSpec((PEER_HEADS, N_KEYS, tm), lambda i: (0, 0, i))
    return pl.pallas_call(
        _peer_score_kernel,
        grid=(n // tm,),
        in_specs=[pl.BlockSpec((tm, D_MODEL), lambda i: (i, 0)),
                  pl.BlockSpec((1, D_MODEL), lambda i: (0, 0)),
                  pl.BlockSpec(wqt.shape, lambda i: (0, 0)),
                  pl.BlockSpec(keys.shape, lambda i: (0, 0, 0, 0))],
        out_specs=[pl.BlockSpec((D_MODEL, tm), lambda i: (0, i)), heads(), heads(), heads(), heads()],
        out_shape=[jax.ShapeDtypeStruct((D_MODEL, n), BF16)]
        + [jax.ShapeDtypeStruct((PEER_HEADS, N_KEYS, n), dt) for dt in (F32, F32, BF16, BF16)],
        compiler_params=_params("parallel"),
        name="peer_score",
    )(h, nw, wqt, keys)


EXPERT_CHUNK = 1024
PEER_TOKENS = 512
PEER_DENSE_TOKENS = 512
SQRT_HALF = math.sqrt(0.5)


def _peer_dense_kernel(xnt_ref, cnt_ref, e1_ref, rank2_ref, e2_ref, u_ref, vt_ref, h_ref, fw_ref,
                       out_ref, acc_s, w_s, gate_s, cnt_s, e1_s, *, final_norm, n_chunks):
    s = pl.program_id(0)
    c = lax.rem(s, n_chunks)
    cur = lax.rem(c, 2)
    tm = xnt_ref.shape[1]
    half = tm // 2

    @pl.when(s == 0)
    def _():
        acc_s[...] = jnp.zeros_like(acc_s)
        w_s[1] = jnp.zeros(w_s.shape[1:], BF16)

    prev = 1 - cur
    n_blocks = EXPERT_CHUNK // N_KEYS
    n_lt = tm // LANES
    per_lt = PEER_HEADS * n_blocks
    for l in range(n_lt):
        cnt_s[l * per_lt:(l + 1) * per_lt, :] = cnt_ref[:, :, l * LANES:(l + 1) * LANES].reshape(per_lt, LANES)
        e1_s[l * per_lt:(l + 1) * per_lt, :] = e1_ref[:, :, l * LANES:(l + 1) * LANES].reshape(per_lt, LANES)

    def rows(ref, hh, j):
        words = jnp.concatenate([jnp.broadcast_to(ref[pl.ds(l * per_lt + hh * n_blocks + j, 1), :], (SUBLANES, LANES))
                                 for l in range(n_lt)], axis=1)
        packed = jnp.concatenate([words, words], axis=0).astype(BF16)
        return jnp.concatenate([packed] * (N_KEYS // (2 * SUBLANES)), axis=0)

    xnt = xnt_ref[...]
    block = lambda j: slice(j * N_KEYS, (j + 1) * N_KEYS)

    def gate(j):
        total = None
        for hh in range(PEER_HEADS):
            e2 = e2_ref[hh]
            sel = jnp.where(rank2_ref[hh] < rows(cnt_s, hh, j), e2, jnp.zeros_like(e2))
            term = rows(e1_s, hh, j) * sel
            total = term if total is None else total + term
        gate_s[block(j), :] = total

    def project(j):
        return _dot(u_ref[block(j), :], xnt)

    def finish(j, pre):
        act = (pre * (1.0 + lax.erf(pre * SQRT_HALF))).astype(BF16)
        w_s[cur, block(j), :] = act * gate_s[block(j), :]

    ahead = 3
    pres = {j: project(j) for j in range(ahead)}
    gate(0)
    for j in range(n_blocks):
        if j + ahead < n_blocks:
            pres[j + ahead] = project(j + ahead)
        if j == 0:
            acc_s[:, :half] += _dot(vt_ref[0], w_s[prev, :, :half])
        if j == n_blocks // 2:
            acc_s[:, half:] += _dot(vt_ref[0], w_s[prev, :, half:])
        if j + 1 < n_blocks:
            gate(j + 1)
        finish(j, pres.pop(j))

    @pl.when(c == 0)
    def _():
        y = h_ref[...] + acc_s[...].T
        if final_norm:
            y = _rmsnorm(y, fw_ref[...])
        out_ref[...] = y
        acc_s[...] = jnp.zeros_like(acc_s)


def _peer_dense(xnt, cnt, e1, rank2, e2, u, vt, h, fw, *, layer, final_norm):
    n = h.shape[0]
    tm = _pick_tile(n, (PEER_DENSE_TOKENS, PEER_TOKENS, 256, 128))
    n_tiles = n // tm
    n_chunks = u.shape[1] // EXPERT_CHUNK
    assert n_chunks % 2 == 0
    rows_per_chunk = EXPERT_CHUNK // N_KEYS
    tile = lambda s: jnp.minimum(s // n_chunks, n_tiles - 1)
    chunk = lambda s: lax.rem(s, n_chunks)
    done = lambda s: jnp.maximum(s - 1, 0) // n_chunks
    by_chunk = lambda: pl.BlockSpec((PEER_HEADS, rows_per_chunk, tm), lambda s: (0, chunk(s), tile(s)))
    by_tile = lambda: pl.BlockSpec((PEER_HEADS, N_KEYS, tm), lambda s: (0, 0, tile(s)))
    return pl.pallas_call(
        functools.partial(_peer_dense_kernel, final_norm=final_norm, n_chunks=n_chunks),
        grid=(n_tiles * n_chunks + 1,),
        in_specs=[pl.BlockSpec((D_MODEL, tm), lambda s: (0, tile(s))),
                  by_chunk(), by_chunk(), by_tile(), by_tile(),
                  pl.BlockSpec((None, EXPERT_CHUNK, D_MODEL), lambda s: (layer, chunk(s), 0)),
                  pl.BlockSpec((None, 1, D_MODEL, EXPERT_CHUNK), lambda s: (layer, chunk(s + n_chunks - 1), 0, 0)),
                  pl.BlockSpec((tm, D_MODEL), lambda s: (done(s), 0)),
                  pl.BlockSpec((1, D_MODEL), lambda s: (0, 0))],
        out_specs=pl.BlockSpec((tm, D_MODEL), lambda s: (done(s), 0)),
        out_shape=jax.ShapeDtypeStruct((n, D_MODEL), F32),
        scratch_shapes=[pltpu.VMEM((D_MODEL, tm), F32), pltpu.VMEM((2, EXPERT_CHUNK, tm), BF16),
                        pltpu.VMEM((EXPERT_CHUNK, tm), BF16)]
        + [pltpu.VMEM((PEER_HEADS * rows_per_chunk * tm // LANES, LANES), F32)] * 2,
        compiler_params=_params("arbitrary"),
        name="peer_dense",
    )(xnt, cnt, e1, rank2, e2, u, vt, h, fw)


def _prep_layer(layer, norm_mix_w, w_in, conv_qkv_w, a_log, dt_bias, gdn_norm_w, conf_dw_w, conf_dw_b,
                conf_ln_w, conf_ln_b, w_out, norm_ffn_w, w_query, sub_keys):
    s0 = QKV_WIDTH
    s1 = s0 + GDN_WIDTH
    s2 = s1 + GDN_HEADS
    s3 = s2 + GDN_HEADS
    w = w_in[layer]
    row = lambda v: v.reshape(1, -1)
    return dict(
        norm_mix_w=row(norm_mix_w[layer]),
        wqkvz=w[:, :s1].astype(BF16),
        wab=jnp.pad(w[:, s1:s3], ((0, 0), (0, LANES - 2 * GDN_HEADS))).astype(BF16),
        wglu=w[:, s3:].astype(BF16),
        alog=row(jnp.pad(a_log[layer], (0, LANES - GDN_HEADS))),
        dtb=row(jnp.pad(dt_bias[layer], (0, LANES - GDN_HEADS))),
        conv_w=conv_qkv_w[layer],
        gdn_norm_w=row(gdn_norm_w[layer]),
        dw=conf_dw_w[layer], dwb=row(conf_dw_b[layer]),
        lnw=row(conf_ln_w[layer]), lnb=row(conf_ln_b[layer]),
        wout=w_out[layer].astype(BF16),
        norm_ffn_w=row(norm_ffn_w[layer]),
        wqt=w_query[layer].T.astype(BF16),
        keys=sub_keys[layer].astype(BF16),
    )


def _peer(h, lw, u, vt, fw, layer, final_norm):
    xnt, cnt, e1, rank2, e2 = _peer_score(h, lw["norm_ffn_w"], lw["wqt"], lw["keys"])
    return _peer_dense(xnt, cnt, e1, rank2, e2, u, vt, h, fw, layer=layer, final_norm=final_norm)


def _mix_prompt(h, lw, b, seq):
    n_pad = (-seq) % GDN_CHUNK
    qkv, z, gb, glu = _in_proj(h, lw["norm_mix_w"], lw["wqkvz"], lw["wab"], lw["wglu"], lw["alog"], lw["dtb"],
                               n_rows=b * seq, act_dtype=BF16)
    shp = lambda a: a.reshape(b, seq, a.shape[-1])
    qkv, z, gb, glu = shp(qkv), shp(z), shp(gb), shp(glu)
    o, s_new = _gdn(qkv, gb, z,
                    jnp.zeros((b, GDN_CONV - 1, QKV_WIDTH), F32),
                    jnp.zeros((b, GDN_HEADS, HEAD_DIM, HEAD_DIM), F32),
                    lw["conv_w"], lw["gdn_norm_w"], chunk=GDN_CHUNK, n_pad=n_pad, bb_count=1)
    h = _conf_out_seq(glu, o, h, jnp.zeros((b, CONF_CONV - 1, CONF_WIDTH), F32),
                      lw["dw"], lw["dwb"], lw["lnw"], lw["lnb"], lw["wout"])
    return h, s_new, qkv[:, seq - (GDN_CONV - 1):].astype(F32), glu[:, seq - (CONF_CONV - 1):].astype(F32)


def _mix_sample(h, row0, n_zero, lw, b, steps, s0, qkv_state, conf_state):
    n_pad = (-steps) % SUBLANES
    to_tb = lambda a: jnp.swapaxes(a, 0, 1)
    qkv, z, gb, glu = _in_proj(h[row0:row0 + steps * b], lw["norm_mix_w"], lw["wqkvz"], lw["wab"], lw["wglu"],
                               lw["alog"], lw["dtb"])
    bt = lambda a: to_tb(a.reshape(steps, b, a.shape[-1]))
    qkv_bt = bt(qkv)
    o, s_new = _gdn(qkv_bt, bt(gb), bt(z), qkv_state, s0,
                    lw["conv_w"], lw["gdn_norm_w"], chunk=steps + n_pad, n_pad=n_pad, bb_count=8)
    cext = jnp.concatenate([to_tb(conf_state), glu.reshape(steps, b, CONF_WIDTH)], axis=0)
    h = _conf_out_step(cext, to_tb(o).reshape(steps * b, GDN_WIDTH), h, row0, n_zero,
                       lw["dw"], lw["dwb"], lw["lnw"], lw["lnb"], lw["wout"])
    qkv_ext = jnp.concatenate([qkv_state, qkv_bt], axis=1)
    return (h, s_new, qkv_ext[:, qkv_ext.shape[1] - (GDN_CONV - 1):],
            to_tb(cext[cext.shape[0] - (CONF_CONV - 1):]))


DROP_ROWS = 512


def _drop_meta_kernel(h_ref, y_ref):
    y_ref[0] = h_ref[...]


def _drop_meta(h, b, seq, n_meta):
    real = seq - n_meta
    rows = _pick_tile(real, (DROP_ROWS, 256, 128, 64, 16))
    assert seq % SUBLANES == 0 and n_meta % SUBLANES == 0 and rows % SUBLANES == 0
    source = pl.BlockSpec((pl.Element(rows), pl.Element(D_MODEL)),
                          lambda i, j: (pl.multiple_of(i * seq + n_meta + j * rows, SUBLANES), 0))
    return pl.pallas_call(
        _drop_meta_kernel,
        grid=(b, real // rows),
        in_specs=[source],
        out_specs=pl.BlockSpec((1, rows, D_MODEL), lambda i, j: (i, j, 0)),
        out_shape=jax.ShapeDtypeStruct((b, real, D_MODEL), h.dtype),
        compiler_params=_params("parallel", "parallel"),
        name="drop_meta",
    )(h)


def kernel(x_prompt, x_sample, state_gdn, state_qkv_conv, state_conf_conv, meta_tokens, norm_mix_w, w_in,
           conv_qkv_w, a_log, dt_bias, gdn_norm_w, conf_dw_w, conf_dw_b, conf_ln_w, conf_ln_b, w_out,
           norm_ffn_w, w_query, sub_keys, expert_u, expert_v, final_norm_w):
    depth = w_in.shape[0]
    layers = [_prep_layer(l, norm_mix_w, w_in, conv_qkv_w, a_log, dt_bias, gdn_norm_w, conf_dw_w,
                          conf_dw_b, conf_ln_w, conf_ln_b, w_out, norm_ffn_w, w_query, sub_keys)
              for l in range(depth)]
    u_all = expert_u.astype(BF16)
    vt_all = jnp.swapaxes(expert_v.astype(BF16).reshape(depth, -1, EXPERT_CHUNK, D_MODEL), 2, 3)
    fw = final_norm_w.reshape(1, D_MODEL)
    b_p, seq0, _ = x_prompt.shape
    seq = seq0 + N_META
    b_s, steps, _ = x_sample.shape
    n_p, n_s = b_p * seq, b_s * steps
    n_fill = (-(n_p + n_s)) % PEER_DENSE_TOKENS
    meta = jnp.broadcast_to(meta_tokens[None], (b_p, N_META, D_MODEL))
    h = jnp.concatenate([jnp.concatenate([meta, x_prompt], axis=1).reshape(n_p, D_MODEL),
                         jnp.swapaxes(x_sample, 0, 1).reshape(n_s, D_MODEL),
                         jnp.zeros((n_fill, D_MODEL), F32)], axis=0)
    outs_p, outs_s = [], []
    for li, lw in enumerate(layers):
        h, *st_p = _mix_prompt(h, lw, b_p, seq)
        h, *st_s = _mix_sample(h, n_p, n_fill, lw, b_s, steps, state_gdn[li], state_qkv_conv[li],
                               state_conf_conv[li])
        outs_p.append(st_p)
        outs_s.append(st_s)
        h = _peer(h, lw, u_all, vt_all, fw, li, li == depth - 1)
    y_p = _drop_meta(h, b_p, seq, N_META)
    y_s = jnp.swapaxes(h[n_p:n_p + n_s].reshape(steps, b_s, D_MODEL), 0, 1)
    stack = lambda outs, i: jnp.stack([o[i] for o in outs])
    return (y_p, y_s, stack(outs_p, 0), stack(outs_p, 1), stack(outs_p, 2),
            stack(outs_s, 0), stack(outs_s, 1), stack(outs_s, 2))
```

```python
import functools
import math

import jax
import jax.numpy as jnp
from jax import lax
from jax.experimental import pallas as pl
from jax.experimental.pallas import tpu as pltpu

F32 = jnp.float32
BF16 = jnp.bfloat16

D_MODEL = 1024
N_META = 16
GDN_HEADS = 4
HEAD_DIM = 128
GDN_WIDTH = GDN_HEADS * HEAD_DIM
CONF_WIDTH = 512
QKV_WIDTH = 3 * GDN_WIDTH
GDN_CONV = 4
GDN_CHUNK = 64
GDN_GROUP = 8
CONF_CONV = 31
N_KEYS = 128
PEER_HEADS = 8
PEER_TOPK = 16
EPS = 1e-6

LANES = 128
SUBLANES = 8
VMEM_LIMIT_BYTES = 56 * 1024 * 1024


def _sigmoid(x):
    return 1.0 / (1.0 + jnp.exp(-x))


def _silu(x):
    return x * _sigmoid(x)


def _softplus(x):
    return jnp.maximum(x, 0.0) + jnp.log1p(jnp.exp(-jnp.abs(x)))


def _rmsnorm(x, w):
    return x * lax.rsqrt(jnp.mean(x * x, axis=-1, keepdims=True) + EPS) * w


def _dot(a, b, precision=None):
    return jnp.dot(a, b, preferred_element_type=F32, precision=precision)


def _dot_nt(a, b, precision=None):
    return lax.dot_general(a, b, (((1,), (1,)), ((), ())), preferred_element_type=F32,
                           precision=precision)


def _dot_tn(a, b, precision=None):
    return lax.dot_general(a, b, (((0,), (0,)), ((), ())), preferred_element_type=F32,
                           precision=precision)


def _split(a):
    hi = a.astype(BF16)
    return hi, a - hi.astype(F32)


def _dot3(a, b):
    a_hi, a_rest = _split(a)
    b_hi, b_rest = _split(b)
    return _dot(a_hi, b_hi) + (_dot(a_hi, b_rest.astype(BF16)) + _dot(a_rest.astype(BF16), b_hi))


def _dot_exact_lhs(a_bf16, b):
    b0, r = _split(b)
    b1, r = _split(r)
    return _dot(a_bf16, b0) + (_dot(a_bf16, b1) + _dot(a_bf16, r.astype(BF16)))


def _pick_tile(n, candidates):
    for c in candidates:
        if n % c == 0:
            return c
    return n


def _params(*semantics):
    return pltpu.CompilerParams(dimension_semantics=semantics, vmem_limit_bytes=VMEM_LIMIT_BYTES)


def _in_proj_kernel(h_ref, nw_ref, wqkvz_ref, wab_ref, wglu_ref, alog_ref, dtb_ref,
                    qkv_ref, z_ref, gb_ref, glu_ref):
    xb = _rmsnorm(h_ref[...], nw_ref[...]).astype(BF16)
    p = _dot(xb, wqkvz_ref[...])
    qkv_ref[...] = p[:, :QKV_WIDTH]
    z_ref[...] = p[:, QKV_WIDTH:]
    ab = _dot(xb, wab_ref[...])
    g = -jnp.exp(alog_ref[...]) * _softplus(ab + dtb_ref[...])
    lane = lax.broadcasted_iota(jnp.int32, ab.shape, 1)
    gb_ref[...] = jnp.where(lane < GDN_HEADS, g, _sigmoid(ab))
    pg = _dot(xb, wglu_ref[...])
    glu_ref[...] = pg[:, :CONF_WIDTH] * _sigmoid(pg[:, CONF_WIDTH:])


def _in_proj(h, nw, wqkvz, wab, wglu, alog, dtb, n_rows=None):
    n = h.shape[0] if n_rows is None else n_rows
    tm = _pick_tile(n, (384, 256, 128))
    row = lambda i: (i, 0)
    fixed = lambda i: (0, 0)
    widths = (QKV_WIDTH, GDN_WIDTH, LANES, CONF_WIDTH)
    return pl.pallas_call(
        _in_proj_kernel,
        grid=(n // tm,),
        in_specs=[
            pl.BlockSpec((tm, D_MODEL), row),
            pl.BlockSpec((1, D_MODEL), fixed),
            pl.BlockSpec(wqkvz.shape, fixed),
            pl.BlockSpec(wab.shape, fixed),
            pl.BlockSpec(wglu.shape, fixed),
            pl.BlockSpec((1, LANES), fixed),
            pl.BlockSpec((1, LANES), fixed),
        ],
        out_specs=[pl.BlockSpec((tm, w), row) for w in widths],
        out_shape=[jax.ShapeDtypeStruct((n, w), F32) for w in widths],
        compiler_params=_params("parallel"),
        name="in_proj",
    )(h, nw, wqkvz, wab, wglu, alog, dtb)


def _gdn_kernel(xq_ref, xk_ref, xv_ref, gb_ref, z_ref, cs_q_ref, cs_k_ref, cs_v_ref, s0_ref,
                cwq_ref, cwk_ref, cwv_ref, nw_ref,
                o_ref, s_ref,
                xq_s, xk_s, xv_s, g_s, b_s, u_s, att_s, wq_s, m_s, bm_s, gl_s, sall_s,
                *, chunk, n_pad, seq, bb_count):
    C = chunk
    n_chunks = (seq + n_pad) // C
    head = 8 + n_pad
    n_fac = int(math.log2(C))
    row_i = lax.broadcasted_iota(jnp.int32, (C, C), 0)
    col_i = lax.broadcasted_iota(jnp.int32, (C, C), 1)
    incl = row_i >= col_i
    strict = row_i > col_i
    tril_b = jnp.where(incl, 1.0, 0.0).astype(BF16)
    src_lane = lax.broadcasted_iota(jnp.int32, (LANES, HEAD_DIM), 0)
    pick_gb = jnp.concatenate([jnp.where(src_lane == pl.program_id(1) + off, 1.0, 0.0).astype(BF16)
                               for off in (0, GDN_HEADS)], axis=1)
    eye =jnp.where(row_i == col_i, 1.0, 0.0).astype(F32)
    group = GDN_GROUP
    live = lax.broadcasted_iota(jnp.int32, (C, HEAD_DIM), 0) >= n_pad
    cws = (cwq_ref[...], cwk_ref[...], cwv_ref[...])
    nw = nw_ref[...]

    def conv(blk, cw):
        acc = blk[5:5 + C] * cw[0:1]
        for j in range(1, GDN_CONV):
            acc = acc + blk[5 + j:5 + j + C] * cw[j:j + 1]
        return _silu(acc)

    slot = lambda bb, c: bb * n_chunks + c

    is_first = lambda c: isinstance(c, int) and c == 0

    def build(items, fill=()):
        fill = list(fill)
        tick = lambda: fill.pop(0)() if fill else None
        each = lambda f, *ls: [f(*a) for a in zip(*ls)]
        at = [(bb, c * C if isinstance(c, int) else pl.multiple_of(c * C, SUBLANES)) for bb, c in items]
        qc = [conv(xq_s[bb, pl.ds(t0, C + 8), :], cws[0]) for bb, t0 in at]
        kc = [conv(xk_s[bb, pl.ds(t0, C + 8), :], cws[1]) for bb, t0 in at]
        v = [conv(xv_s[bb, pl.ds(t0, C + 8), :], cws[2]) for bb, t0 in at]
        q = each(lambda a: a * lax.rsqrt(jnp.sum(a * a, axis=-1, keepdims=True) + EPS) * (HEAD_DIM ** -0.5), qc)
        k = each(lambda a: a * lax.rsqrt(jnp.sum(a * a, axis=-1, keepdims=True) + EPS), kc)
        g = [g_s[bb, pl.ds(t0, C), :] for bb, t0 in at]
        beta = [b_s[bb, pl.ds(t0, C), :] for bb, t0 in at]
        if n_pad:
            q, k, v = ([jnp.where(live, a, 0.0) if is_first(c) else a for a, (_, c) in zip(l, items)]
                       for l in (q, k, v))
        cum = each(lambda a: _dot_exact_lhs(tril_b, jnp.concatenate([a, jnp.where(strict, a[:, :C], 0.0)], axis=1)), g)
        tick()
        gcum = [a[:, :HEAD_DIM] for a in cum]
        decay = [jnp.where(incl, jnp.exp(a[:, HEAD_DIM:]), 0.0) for a in cum]
        eg = [jnp.exp(a) for a in gcum]
        kb = each(lambda a, b_: a * b_, k, beta)
        kq = each(lambda a, b_, c_: _dot_nt(jnp.concatenate([a, b_], axis=0).astype(BF16), c_.astype(BF16)), kb, q, k)
        tick()
        p = each(lambda a, d: jnp.where(strict, -(a[:C] * d), 0.0), kq, decay)
        att = each(lambda a, d: jnp.where(incl, a[C:] * d, 0.0), kq, decay)
        t = [eye + a for a in p]
        p = each(lambda a: _dot3(a, a), p)
        tick()
        for step in range(1, n_fac):
            tick()
            if step == n_fac - 1:
                t = each(lambda a, b_: a + _dot3(b_, a), t, p)
            else:
                y = each(lambda a, b_: _dot3(b_, jnp.concatenate([a, b_], axis=1)), t, p)
                t = each(lambda a, b_: a + b_[:, :C], t, y)
                p = [a[:, C:] for a in y]
        x = each(lambda t_, v_, b_, kb_, eg_: _dot3(t_, jnp.concatenate([v_ * b_, kb_ * eg_], axis=1)),
                 t, v, beta, kb, eg)
        while fill:
            tick()
        g_last = [a[C - 1:C, :] for a in gcum]
        kt = each(lambda k_, gl, gc: k_ * jnp.exp(gl - gc), k, g_last, gcum)
        ktx = each(lambda a, b_: _dot_tn(a.astype(BF16), b_.astype(BF16)), kt, x)
        for i, (bb, c) in enumerate(items):
            n = slot(bb, c)
            u_s[n] = x[i][:, :HEAD_DIM]
            att_s[n] = att[i].astype(BF16)
            wq_s[n] = jnp.concatenate([x[i][:, HEAD_DIM:], q[i] * eg[i]], axis=0).astype(BF16)
            bm_s[n] = ktx[i][:, :HEAD_DIM]
            m_s[n] = (-ktx[i][:, HEAD_DIM:]).astype(BF16)
            gl_s[n] = jnp.exp(g_last[i])

    def carry(items):
        s = [s_ref[bb, 0] for bb, _ in items]
        for (bb, c), s_i in zip(items, s):
            sall_s[slot(bb, c)] = s_i
        new = [gl_s[slot(bb, c)] * s_i + _dot(m_s[slot(bb, c)], s_i.astype(BF16)) + bm_s[slot(bb, c)]
               for (bb, c), s_i in zip(items, s)]
        for (bb, _), s_i in zip(items, new):
            s_ref[bb, 0] = s_i

    def emit(items):
        ns = [slot(bb, c) for bb, c in items]
        ws_qs = [_dot(wq_s[n], sall_s[n].astype(BF16)) for n in ns]
        v_new = [u_s[n] - a[:C] for n, a in zip(ns, ws_qs)]
        o = [a[C:] + _dot(att_s[n], vn.astype(BF16)) for n, a, vn in zip(ns, ws_qs, v_new)]
        o = [_rmsnorm(a, nw) for a in o]
        for (bb, c), a in zip(items, o):
            if is_first(c):
                rows = C - n_pad
                o_ref[bb, 0:rows, :] = (a[n_pad:] * _silu(z_ref[bb, 0:rows, :])).astype(BF16)
            else:
                r0 = c * C - n_pad if isinstance(c, int) else pl.multiple_of(c * C - n_pad, 2 * SUBLANES)
                o_ref[bb, pl.ds(r0, C), :] = (a * _silu(z_ref[bb, pl.ds(r0, C), :])).astype(BF16)

    lead = (n_chunks - 1) % group + 1
    if lead == 1 and n_chunks > 1:
        lead += group
    leading = [(bb, c) for bb in range(bb_count) for c in range(lead)]

    n_loop = (n_chunks - lead) // group
    later = lambda i: [(0, lead + i * group + j) for j in range(group)]
    carry_range = lambda lo, hi: lax.fori_loop(lo, hi, lambda c, _: (carry([(0, c)]), 0)[1], 0)

    for bb in range(bb_count):
        for x_s, x_ref, cs_ref in ((xq_s, xq_ref, cs_q_ref), (xk_s, xk_ref, cs_k_ref),
                                   (xv_s, xv_ref, cs_v_ref)):
            x_s[bb, 0:head, :] = jnp.zeros((head, HEAD_DIM), F32)
            x_s[bb, head - (GDN_CONV - 1):head, :] = cs_ref[bb]
            x_s[bb, head:head + seq, :] = x_ref[bb]
        if n_pad:
            g_s[bb, 0:n_pad, :] = jnp.zeros((n_pad, HEAD_DIM), F32)
            b_s[bb, 0:n_pad, :] = jnp.zeros((n_pad, HEAD_DIM), F32)
        pieces = []
        rest = gb_ref[bb]
        for _ in range(3):
            hi, rest = _split(rest)
            pieces.append(hi)
        both = _dot(pieces[0], pick_gb) + (_dot(pieces[1], pick_gb) + _dot(pieces[2], pick_gb))
        g_s[bb, n_pad:n_pad + seq, :] = both[:, :HEAD_DIM]
        b_s[bb, n_pad:n_pad + seq, :] = both[:, HEAD_DIM:]
        s_ref[bb, 0] = s0_ref[bb, 0]

    build(leading)
    carry([(bb, 0) for bb in range(bb_count)])

    def build_later(i, _):
        build(later(i), [functools.partial(carry, [(0, 1 + i * group + j)]) for j in range(group)])
        return 0

    if n_loop:
        lax.fori_loop(0, n_loop, build_later, 0)
    if n_chunks > 1:
        carry_range(1 + n_loop * group, n_chunks)
    emit(leading)
    if n_loop:
        lax.fori_loop(0, n_loop, lambda i, _: (emit(later(i)), 0)[1], 0)


def _gdn(qkv, gb, z, conv_state, s0, conv_w, norm_w, *, chunk, n_pad, bb_count):
    b, seq, _ = qkv.shape
    n_chunks = (seq + n_pad) // chunk
    assert n_chunks * chunk == seq + n_pad and b % bb_count == 0
    assert n_chunks == 1 or bb_count == 1
    col = lambda off: (lambda i, h: (i, 0, off + h))
    wcol = lambda off: (lambda i, h: (0, off + h))
    seq_blk = (bb_count, seq, HEAD_DIM)
    cs_blk = (bb_count, GDN_CONV - 1, HEAD_DIM)
    s_blk = (bb_count, 1, HEAD_DIM, HEAD_DIM)
    s_map = lambda i, h: (i, h, 0, 0)
    ext = chunk * n_chunks
    per_chunk = lambda r, c, dt=F32: pltpu.VMEM((bb_count * n_chunks, r, c), dt)
    kern = functools.partial(_gdn_kernel, chunk=chunk, n_pad=n_pad, seq=seq, bb_count=bb_count)
    return pl.pallas_call(
        kern,
        grid=(b // bb_count, GDN_HEADS),
        in_specs=[
            pl.BlockSpec(seq_blk, col(0)),
            pl.BlockSpec(seq_blk, col(GDN_HEADS)),
            pl.BlockSpec(seq_blk, col(2 * GDN_HEADS)),
            pl.BlockSpec((bb_count, seq, LANES), lambda i, h: (i, 0, 0)),
            pl.BlockSpec(seq_blk, col(0)),
            pl.BlockSpec(cs_blk, col(0)),
            pl.BlockSpec(cs_blk, col(GDN_HEADS)),
            pl.BlockSpec(cs_blk, col(2 * GDN_HEADS)),
            pl.BlockSpec(s_blk, s_map),
            pl.BlockSpec((GDN_CONV, HEAD_DIM), wcol(0)),
            pl.BlockSpec((GDN_CONV, HEAD_DIM), wcol(GDN_HEADS)),
            pl.BlockSpec((GDN_CONV, HEAD_DIM), wcol(2 * GDN_HEADS)),
            pl.BlockSpec((1, HEAD_DIM), lambda i, h: (0, 0)),
        ],
        out_specs=[pl.BlockSpec(seq_blk, col(0)), pl.BlockSpec(s_blk, s_map)],
        out_shape=[jax.ShapeDtypeStruct((b, seq, GDN_WIDTH), BF16),
                   jax.ShapeDtypeStruct((b, GDN_HEADS, HEAD_DIM, HEAD_DIM), F32)],
        scratch_shapes=[
            pltpu.VMEM((bb_count, ext + 8, HEAD_DIM), F32),
            pltpu.VMEM((bb_count, ext + 8, HEAD_DIM), F32),
            pltpu.VMEM((bb_count, ext + 8, HEAD_DIM), F32),
            pltpu.VMEM((bb_count, ext, HEAD_DIM), F32),
            pltpu.VMEM((bb_count, ext, HEAD_DIM), F32),
            per_chunk(chunk, HEAD_DIM),
            per_chunk(chunk, chunk, BF16),
            per_chunk(2 * chunk, HEAD_DIM, BF16),
            per_chunk(HEAD_DIM, HEAD_DIM, BF16),
            per_chunk(HEAD_DIM, HEAD_DIM),
            per_chunk(1, HEAD_DIM),
            per_chunk(HEAD_DIM, HEAD_DIM),
        ],
        compiler_params=_params("parallel", "parallel"),
        name="gdn",
    )(qkv, qkv, qkv, gb, z, conv_state, conv_state, conv_state, s0,
      conv_w, conv_w, conv_w, norm_w)


CONF_HALO = 32
CONF_ROWS = 16


def _conf_out_seq_kernel(glu_ref, o_ref, h_ref, cstate_ref, dw_ref, dwb_ref, lnw_ref, lnb_ref, wout_ref,
                         out_ref, ext_s, c_s, sh_s, *, tt):
    t = pl.program_id(1)
    lead = CONF_HALO - (CONF_CONV - 1)

    @pl.when(t == 0)
    def _():
        ext_s[0:lead, :] = jnp.zeros((lead, CONF_WIDTH), F32)
        ext_s[lead:CONF_HALO, :] = cstate_ref[0]

    @pl.when(t > 0)
    def _():
        ext_s[0:CONF_HALO, :] = ext_s[tt:tt + CONF_HALO, :]

    ext_s[CONF_HALO:CONF_HALO + tt, :] = glu_ref[0]
    dw = dw_ref[...]
    bias = dwb_ref[...]

    def rows(i, _):
        r0 = pl.multiple_of(i * CONF_ROWS, SUBLANES)
        blk = ext_s[pl.ds(r0, CONF_ROWS + CONF_HALO), :]
        span = CONF_ROWS + CONF_HALO - SUBLANES
        for s in range(1, SUBLANES):
            sh_s[s - 1] = blk[s:s + span]
        acc = bias
        for j in range(CONF_CONV):
            q, s = divmod(lead + j, SUBLANES)
            rows_j = slice(q * SUBLANES, q * SUBLANES + CONF_ROWS)
            acc = acc + (blk[rows_j] if s == 0 else sh_s[s - 1, rows_j, :]) * dw[j:j + 1]
        c_s[pl.ds(r0, CONF_ROWS), :] = acc
        return 0

    lax.fori_loop(0, tt // CONF_ROWS, rows, 0)
    c = c_s[...]
    mu = jnp.mean(c, axis=-1, keepdims=True)
    xc = c - mu
    var = jnp.mean(xc * xc, axis=-1, keepdims=True)
    c = _silu(xc * lax.rsqrt(var + EPS) * lnw_ref[...] + lnb_ref[...])
    mixed = jnp.concatenate([o_ref[0], c.astype(BF16)], axis=1)
    out_ref[...] = h_ref[...] + _dot(mixed, wout_ref[...])


def _conf_out_seq(glu, o, h, cstate, dw, dwb, lnw, lnb, wout):
    b, seq, _ = glu.shape
    tt = _pick_tile(seq, (688, 512, 256, 128, 64, 16))
    per_seq = seq // tt
    tile = lambda w: pl.BlockSpec((1, tt, w), lambda i, t: (i, t, 0))
    flat = pl.BlockSpec((tt, D_MODEL), lambda i, t: (i * per_seq + t, 0))
    fixed = lambda a: pl.BlockSpec(a.shape, lambda i, t: (0,) * a.ndim)
    return pl.pallas_call(
        functools.partial(_conf_out_seq_kernel, tt=tt),
        grid=(b, per_seq),
        in_specs=[tile(CONF_WIDTH), tile(GDN_WIDTH), flat,
                  pl.BlockSpec((1, CONF_CONV - 1, CONF_WIDTH), lambda i, t: (i, 0, 0)),
                  fixed(dw), fixed(dwb), fixed(lnw), fixed(lnb), fixed(wout)],
        out_specs=flat,
        out_shape=jax.ShapeDtypeStruct(h.shape, F32),
        input_output_aliases={2: 0},
        scratch_shapes=[pltpu.VMEM((tt + CONF_HALO, CONF_WIDTH), F32),
                        pltpu.VMEM((tt, CONF_WIDTH), F32),
                        pltpu.VMEM((SUBLANES - 1, CONF_ROWS + CONF_HALO - SUBLANES, CONF_WIDTH), F32)],
        compiler_params=_params("parallel", "arbitrary"),
        name="conf_out_seq",
    )(glu, o, h, cstate, dw, dwb, lnw, lnb, wout)


def _conf_out_step_kernel(cext_ref, o_ref, h_ref, dw_ref, dwb_ref, lnw_ref, lnb_ref, wout_ref, out_ref,
                          *, steps):
    t = pl.program_id(0)

    @pl.when(t < steps)
    def _():
        dw = dw_ref[...]
        c = dwb_ref[...] + cext_ref[t] * dw[0:1]
        for j in range(1, CONF_CONV):
            c = c + cext_ref[t + j] * dw[j:j + 1]
        mu = jnp.mean(c, axis=-1, keepdims=True)
        xc = c - mu
        var = jnp.mean(xc * xc, axis=-1, keepdims=True)
        c = _silu(xc * lax.rsqrt(var + EPS) * lnw_ref[...] + lnb_ref[...])
        mixed = jnp.concatenate([o_ref[...], c.astype(BF16)], axis=1)
        out_ref[...] = h_ref[...] + _dot(mixed, wout_ref[...])

    @pl.when(t >= steps)
    def _():
        out_ref[...] = jnp.zeros_like(out_ref)


def _conf_out_step(cext, o, h, row0, n_zero, dw, dwb, lnw, lnb, wout):
    steps = cext.shape[0] - (CONF_CONV - 1)
    b = cext.shape[1]
    assert row0 % b == 0 and n_zero % b == 0 and row0 + steps * b + n_zero <= h.shape[0]
    full = lambda a: pl.BlockSpec(a.shape, lambda t: (0,) * a.ndim)
    rows = pl.BlockSpec((b, D_MODEL), lambda t: (row0 // b + t, 0))
    return pl.pallas_call(
        functools.partial(_conf_out_step_kernel, steps=steps),
        grid=(steps + n_zero // b,),
        in_specs=[full(cext), pl.BlockSpec((b, GDN_WIDTH), lambda t: (jnp.minimum(t, steps - 1), 0)), rows,
                  full(dw), full(dwb), full(lnw), full(lnb), full(wout)],
        out_specs=rows,
        out_shape=jax.ShapeDtypeStruct(h.shape, F32),
        input_output_aliases={2: 0},
        compiler_params=_params("arbitrary"),
        name="conf_out_step",
    )(cext, o, h, dw, dwb, lnw, lnb, wout)


N_RANKS = PEER_TOPK + 1
RANK_ROWS = 24
NEG_INF = float("-inf")


def _top_rows(s, count):
    rows = []
    for r in range(count):
        m = jnp.max(s, axis=0, keepdims=True)
        rows.append(m)
        s = jnp.where(s == m, NEG_INF, s)
    return rows


TOP_WAYS = 4


def _top_rows_chained(s, count):
    n = s.shape[0] // TOP_WAYS
    lv = [s[i * n:(i + 1) * n] for i in range(TOP_WAYS)]
    for i, j in ((0, 1), (2, 3), (0, 2), (1, 3), (1, 2)):
        lv[i], lv[j] = jnp.maximum(lv[i], lv[j]), jnp.minimum(lv[i], lv[j])
    rows = []
    for r in range(count):
        m = jnp.max(lv[0], axis=0, keepdims=True)
        rows.append(m)
        hit = lv[0] == m
        for i in range(TOP_WAYS - 1):
            lv[i] = jnp.where(hit, lv[i + 1], lv[i])
        lv[-1] = jnp.where(hit, NEG_INF, lv[-1])
    return rows


def _prefix_count(hits):
    out = jnp.zeros(hits[0].shape, F32)
    for r, hit in enumerate(hits):
        out = jnp.where(hit, float(r + 1), out)
    return out


def _stack_rows(rows, height, tm):
    idx = lax.broadcasted_iota(jnp.int32, (height, tm), 0)
    out = jnp.full((height, tm), NEG_INF, F32)
    for r, row in enumerate(rows):
        out = jnp.where(idx == r, row, out)
    return out


def _peer_score_kernel(h_ref, nw_ref, wqt_ref, keys_ref, xnt_ref, cnt_ref, e1_ref, rank2_ref, e2_ref):
    tm = h_ref.shape[0]
    xn = _rmsnorm(h_ref[...], nw_ref[...])
    xnt = xn.T.astype(BF16)
    xnt_ref[...] = xnt
    qt = _dot(wqt_ref[...], xnt).astype(BF16)
    idx8 = lax.broadcasted_iota(jnp.int32, (SUBLANES, tm), 0)
    for hh in range(PEER_HEADS):
        s = []
        for p in range(2):
            r0 = (hh * 2 + p) * N_KEYS
            s.append(_dot(keys_ref[hh, p], qt[r0:r0 + N_KEYS, :]))
        a = _top_rows_chained(s[0], N_RANKS)
        b = _top_rows_chained(s[1], N_RANKS)
        a_st = _stack_rows(a, RANK_ROWS, tm)
        b_st = _stack_rows(b, RANK_ROWS, tm)
        cands = [a[0] + b_st,
                 jnp.where(lax.broadcasted_iota(jnp.int32, (RANK_ROWS, tm), 0) >= 1,
                           a_st + b[0], NEG_INF)]
        for r1 in range(1, N_RANKS):
            hi = N_RANKS // (r1 + 1) - 1
            if hi >= 1 and r1 <= 4:
                cands.append(jnp.where((idx8 >= 1) & (idx8 <= hi), a[r1] + b_st[0:SUBLANES], NEG_INF))
        cands.append(jnp.where((idx8 >= 5) & (idx8 <= N_RANKS // 2 - 1),
                               a_st[0:SUBLANES] + b[1], NEG_INF))
        cand = jnp.concatenate(cands, axis=0)
        top = _top_rows(cand, N_RANKS)
        z = jnp.zeros_like(top[0])
        for r in range(PEER_TOPK):
            z = z + jnp.exp(top[r] - top[0])
        thr = 0.5 * (top[PEER_TOPK - 1] + top[PEER_TOPK])
        cnt1 = _prefix_count([s[0] >= thr - b[r2] for r2 in range(PEER_TOPK)])
        rank2 = _prefix_count([s[1] < b[r] for r in range(PEER_TOPK)])
        cnt_ref[hh] = cnt1
        e1_ref[hh] = 0.5 * jnp.exp(s[0] - a[0]) / z
        rank2_ref[hh] = rank2.astype(BF16)
        e2_ref[hh] = jnp.exp(s[1] - b[0]).astype(BF16)


def _peer_score(h, nw, wqt, keys):
    n = h.shape[0]
    tm = _pick_tile(n, (PEER_TOKENS, 256, 128))
    heads = lambda: pl.BlockSpec((PEER_HEADS, N_KEYS, tm), lambda i: (0, 0, i))
    return pl.pallas_call(
        _peer_score_kernel,
        grid=(n // tm,),
        in_specs=[pl.BlockSpec((tm, D_MODEL), lambda i: (i, 0)),
                  pl.BlockSpec((1, D_MODEL), lambda i: (0, 0)),
                  pl.BlockSpec(wqt.shape, lambda i: (0, 0)),
                  pl.BlockSpec(keys.shape, lambda i: (0, 0, 0, 0))],
        out_specs=[pl.BlockSpec((D_MODEL, tm), lambda i: (0, i)), heads(), heads(), heads(), heads()],
        out_shape=[jax.ShapeDtypeStruct((D_MODEL, n), BF16)]
        + [jax.ShapeDtypeStruct((PEER_HEADS, N_KEYS, n), dt) for dt in (F32, F32, BF16, BF16)],
        compiler_params=_params("parallel"),
        name="peer_score",
    )(h, nw, wqt, keys)


EXPERT_CHUNK = 1024
PEER_TOKENS = 512
PEER_DENSE_TOKENS = 512
SQRT_HALF = math.sqrt(0.5)


def _peer_dense_kernel(xnt_ref, cnt_ref, e1_ref, rank2_ref, e2_ref, u_ref, vt_ref, h_ref, fw_ref,
                       out_ref, acc_s, w_s, gate_s, cnt_s, e1_s, *, final_norm, n_chunks):
    s = pl.program_id(0)
    c = lax.rem(s, n_chunks)
    cur = lax.rem(c, 2)
    tm = xnt_ref.shape[1]
    half = tm // 2

    @pl.when(s == 0)
    def _():
        acc_s[...] = jnp.zeros_like(acc_s)
        w_s[1] = jnp.zeros(w_s.shape[1:], BF16)

    prev = 1 - cur
    n_blocks = EXPERT_CHUNK // N_KEYS
    n_lt = tm // LANES
    per_lt = PEER_HEADS * n_blocks
    for l in range(n_lt):
        cnt_s[l * per_lt:(l + 1) * per_lt, :] = cnt_ref[:, :, l * LANES:(l + 1) * LANES].reshape(per_lt, LANES)
        e1_s[l * per_lt:(l + 1) * per_lt, :] = e1_ref[:, :, l * LANES:(l + 1) * LANES].reshape(per_lt, LANES)

    def rows(ref, hh, j):
        words = jnp.concatenate([jnp.broadcast_to(ref[pl.ds(l * per_lt + hh * n_blocks + j, 1), :], (SUBLANES, LANES))
                                 for l in range(n_lt)], axis=1)
        packed = jnp.concatenate([words, words], axis=0).astype(BF16)
        return jnp.concatenate([packed] * (N_KEYS // (2 * SUBLANES)), axis=0)

    xnt = xnt_ref[...]
    block = lambda j: slice(j * N_KEYS, (j + 1) * N_KEYS)

    def gate(j):
        total = None
        for hh in range(PEER_HEADS):
            e2 = e2_ref[hh]
            sel = jnp.where(rank2_ref[hh] < rows(cnt_s, hh, j), e2, jnp.zeros_like(e2))
            term = rows(e1_s, hh, j) * sel
            total = term if total is None else total + term
        gate_s[block(j), :] = total

    def project(j):
        return _dot(u_ref[block(j), :], xnt)

    def finish(j, pre):
        act = (pre * (1.0 + lax.erf(pre * SQRT_HALF))).astype(BF16)
        w_s[cur, block(j), :] = act * gate_s[block(j), :]

    ahead = 3
    pres = {j: project(j) for j in range(ahead)}
    gate(0)
    for j in range(n_blocks):
        if j + ahead < n_blocks:
            pres[j + ahead] = project(j + ahead)
        if j == 0:
            acc_s[:, :half] += _dot(vt_ref[0], w_s[prev, :, :half])
        if j == n_blocks // 2:
            acc_s[:, half:] += _dot(vt_ref[0], w_s[prev, :, half:])
        if j + 1 < n_blocks:
            gate(j + 1)
        finish(j, pres.pop(j))

    @pl.when(c == 0)
    def _():
        y = h_ref[...] + acc_s[...].T
        if final_norm:
            y = _rmsnorm(y, fw_ref[...])
        out_ref[...] = y
        acc_s[...] = jnp.zeros_like(acc_s)


def _peer_dense(xnt, cnt, e1, rank2, e2, u, vt, h, fw, *, layer, final_norm):
    n = h.shape[0]
    tm = _pick_tile(n, (PEER_DENSE_TOKENS, PEER_TOKENS, 256, 128))
    n_tiles = n // tm
    n_chunks = u.shape[1] // EXPERT_CHUNK
    assert n_chunks % 2 == 0
    rows_per_chunk = EXPERT_CHUNK // N_KEYS
    tile = lambda s: jnp.minimum(s // n_chunks, n_tiles - 1)
    chunk = lambda s: lax.rem(s, n_chunks)
    done = lambda s: jnp.maximum(s - 1, 0) // n_chunks
    by_chunk = lambda: pl.BlockSpec((PEER_HEADS, rows_per_chunk, tm), lambda s: (0, chunk(s), tile(s)))
    by_tile = lambda: pl.BlockSpec((PEER_HEADS, N_KEYS, tm), lambda s: (0, 0, tile(s)))
    return pl.pallas_call(
        functools.partial(_peer_dense_kernel, final_norm=final_norm, n_chunks=n_chunks),
        grid=(n_tiles * n_chunks + 1,),
        in_specs=[pl.BlockSpec((D_MODEL, tm), lambda s: (0, tile(s))),
                  by_chunk(), by_chunk(), by_tile(), by_tile(),
                  pl.BlockSpec((None, EXPERT_CHUNK, D_MODEL), lambda s: (layer, chunk(s), 0)),
                  pl.BlockSpec((None, 1, D_MODEL, EXPERT_CHUNK), lambda s: (layer, chunk(s + n_chunks - 1), 0, 0)),
                  pl.BlockSpec((tm, D_MODEL), lambda s: (done(s), 0)),
                  pl.BlockSpec((1, D_MODEL), lambda s: (0, 0))],
        out_specs=pl.BlockSpec((tm, D_MODEL), lambda s: (done(s), 0)),
        out_shape=jax.ShapeDtypeStruct((n, D_MODEL), F32),
        scratch_shapes=[pltpu.VMEM((D_MODEL, tm), F32), pltpu.VMEM((2, EXPERT_CHUNK, tm), BF16),
                        pltpu.VMEM((EXPERT_CHUNK, tm), BF16)]
        + [pltpu.VMEM((PEER_HEADS * rows_per_chunk * tm // LANES, LANES), F32)] * 2,
        compiler_params=_params("arbitrary"),
        name="peer_dense",
    )(xnt, cnt, e1, rank2, e2, u, vt, h, fw)


def _prep_layer(layer, norm_mix_w, w_in, conv_qkv_w, a_log, dt_bias, gdn_norm_w, conf_dw_w, conf_dw_b,
                conf_ln_w, conf_ln_b, w_out, norm_ffn_w, w_query, sub_keys):
    s0 = QKV_WIDTH
    s1 = s0 + GDN_WIDTH
    s2 = s1 + GDN_HEADS
    s3 = s2 + GDN_HEADS
    w = w_in[layer]
    row = lambda v: v.reshape(1, -1)
    return dict(
        norm_mix_w=row(norm_mix_w[layer]),
        wqkvz=w[:, :s1].astype(BF16),
        wab=jnp.pad(w[:, s1:s3], ((0, 0), (0, LANES - 2 * GDN_HEADS))).astype(BF16),
        wglu=w[:, s3:].astype(BF16),
        alog=row(jnp.pad(a_log[layer], (0, LANES - GDN_HEADS))),
        dtb=row(jnp.pad(dt_bias[layer], (0, LANES - GDN_HEADS))),
        conv_w=conv_qkv_w[layer],
        gdn_norm_w=row(gdn_norm_w[layer]),
        dw=conf_dw_w[layer], dwb=row(conf_dw_b[layer]),
        lnw=row(conf_ln_w[layer]), lnb=row(conf_ln_b[layer]),
        wout=w_out[layer].astype(BF16),
        norm_ffn_w=row(norm_ffn_w[layer]),
        wqt=w_query[layer].T.astype(BF16),
        keys=sub_keys[layer].astype(BF16),
    )


def _peer(h, lw, u, vt, fw, layer, final_norm):
    xnt, cnt, e1, rank2, e2 = _peer_score(h, lw["norm_ffn_w"], lw["wqt"], lw["keys"])
    return _peer_dense(xnt, cnt, e1, rank2, e2, u, vt, h, fw, layer=layer, final_norm=final_norm)


def _mix_prompt(h, lw, b, seq):
    n_pad = (-seq) % GDN_CHUNK
    qkv, z, gb, glu = _in_proj(h, lw["norm_mix_w"], lw["wqkvz"], lw["wab"], lw["wglu"], lw["alog"], lw["dtb"],
                               n_rows=b * seq)
    shp = lambda a: a.reshape(b, seq, a.shape[-1])
    qkv, z, gb, glu = shp(qkv), shp(z), shp(gb), shp(glu)
    o, s_new = _gdn(qkv, gb, z,
                    jnp.zeros((b, GDN_CONV - 1, QKV_WIDTH), F32),
                    jnp.zeros((b, GDN_HEADS, HEAD_DIM, HEAD_DIM), F32),
                    lw["conv_w"], lw["gdn_norm_w"], chunk=GDN_CHUNK, n_pad=n_pad, bb_count=1)
    h = _conf_out_seq(glu, o, h, jnp.zeros((b, CONF_CONV - 1, CONF_WIDTH), F32),
                      lw["dw"], lw["dwb"], lw["lnw"], lw["lnb"], lw["wout"])
    return h, s_new, qkv[:, seq - (GDN_CONV - 1):], glu[:, seq - (CONF_CONV - 1):]


def _mix_sample(h, row0, n_zero, lw, b, steps, s0, qkv_state, conf_state):
    n_pad = (-steps) % SUBLANES
    to_tb = lambda a: jnp.swapaxes(a, 0, 1)
    qkv, z, gb, glu = _in_proj(h[row0:row0 + steps * b], lw["norm_mix_w"], lw["wqkvz"], lw["wab"], lw["wglu"],
                               lw["alog"], lw["dtb"])
    bt = lambda a: to_tb(a.reshape(steps, b, a.shape[-1]))
    qkv_bt = bt(qkv)
    o, s_new = _gdn(qkv_bt, bt(gb), bt(z), qkv_state, s0,
                    lw["conv_w"], lw["gdn_norm_w"], chunk=steps + n_pad, n_pad=n_pad, bb_count=8)
    cext = jnp.concatenate([to_tb(conf_state), glu.reshape(steps, b, CONF_WIDTH)], axis=0)
    h = _conf_out_step(cext, to_tb(o).reshape(steps * b, GDN_WIDTH), h, row0, n_zero,
                       lw["dw"], lw["dwb"], lw["lnw"], lw["lnb"], lw["wout"])
    qkv_ext = jnp.concatenate([qkv_state, qkv_bt], axis=1)
    return (h, s_new, qkv_ext[:, qkv_ext.shape[1] - (GDN_CONV - 1):],
            to_tb(cext[cext.shape[0] - (CONF_CONV - 1):]))


def _assemble_kernel(meta_ref, x_ref, tail_ref, out_ref, *, n_seq, n_meta, n_tail):
    i = pl.program_id(0)

    @pl.when(i < n_seq)
    def _():
        out_ref[0:n_meta, :] = meta_ref[...]
        out_ref[n_meta:, :] = x_ref[0]

    @pl.when(i == n_seq)
    def _():
        out_ref[0:n_tail, :] = tail_ref[...]


def _assemble(meta, x, tail):
    b, seq0, d = x.shape
    n_meta, n_tail = meta.shape[0], tail.shape[0]
    seq = seq0 + n_meta
    assert 0 < n_tail <= seq and n_meta % SUBLANES == 0 and n_tail % SUBLANES == 0
    return pl.pallas_call(
        functools.partial(_assemble_kernel, n_seq=b, n_meta=n_meta, n_tail=n_tail),
        grid=(b + 1,),
        in_specs=[pl.BlockSpec((n_meta, d), lambda i: (0, 0)),
                  pl.BlockSpec((1, seq0, d), lambda i: (jnp.minimum(i, b - 1), 0, 0)),
                  pl.BlockSpec((n_tail, d), lambda i: (0, 0))],
        out_specs=pl.BlockSpec((seq, d), lambda i: (i, 0)),
        out_shape=jax.ShapeDtypeStruct((b * seq + n_tail, d), x.dtype),
        compiler_params=_params("arbitrary"),
        name="assemble",
    )(meta, x, tail)


DROP_ROWS = 512


def _drop_meta_kernel(h_ref, y_ref):
    y_ref[0] = h_ref[...]


def _drop_meta(h, b, seq, n_meta):
    real = seq - n_meta
    rows = _pick_tile(real, (DROP_ROWS, 256, 128, 64, 16))
    assert seq % SUBLANES == 0 and n_meta % SUBLANES == 0 and rows % SUBLANES == 0
    source = pl.BlockSpec((pl.Element(rows), pl.Element(D_MODEL)),
                          lambda i, j: (pl.multiple_of(i * seq + n_meta + j * rows, SUBLANES), 0))
    return pl.pallas_call(
        _drop_meta_kernel,
        grid=(b, real // rows),
        in_specs=[source],
        out_specs=pl.BlockSpec((1, rows, D_MODEL), lambda i, j: (i, j, 0)),
        out_shape=jax.ShapeDtypeStruct((b, real, D_MODEL), h.dtype),
        compiler_params=_params("parallel", "parallel"),
        name="drop_meta",
    )(h)


def kernel(x_prompt, x_sample, state_gdn, state_qkv_conv, state_conf_conv, meta_tokens, norm_mix_w, w_in,
           conv_qkv_w, a_log, dt_bias, gdn_norm_w, conf_dw_w, conf_dw_b, conf_ln_w, conf_ln_b, w_out,
           norm_ffn_w, w_query, sub_keys, expert_u, expert_v, final_norm_w):
    depth = w_in.shape[0]
    layers = [_prep_layer(l, norm_mix_w, w_in, conv_qkv_w, a_log, dt_bias, gdn_norm_w, conf_dw_w,
                          conf_dw_b, conf_ln_w, conf_ln_b, w_out, norm_ffn_w, w_query, sub_keys)
              for l in range(depth)]
    u_all = expert_u.astype(BF16)
    vt_all = jnp.swapaxes(expert_v.astype(BF16).reshape(depth, -1, EXPERT_CHUNK, D_MODEL), 2, 3)
    fw = final_norm_w.reshape(1, D_MODEL)
    b_p, seq0, _ = x_prompt.shape
    seq = seq0 + N_META
    b_s, steps, _ = x_sample.shape
    n_p, n_s = b_p * seq, b_s * steps
    n_fill = (-(n_p + n_s)) % PEER_DENSE_TOKENS
    h = _assemble(meta_tokens, x_prompt,
                  jnp.concatenate([jnp.swapaxes(x_sample, 0, 1).reshape(n_s, D_MODEL),
                                   jnp.zeros((n_fill, D_MODEL), F32)], axis=0))
    outs_p, outs_s = [], []
    for li, lw in enumerate(layers):
        h, *st_p = _mix_prompt(h, lw, b_p, seq)
        h, *st_s = _mix_sample(h, n_p, n_fill, lw, b_s, steps, state_gdn[li], state_qkv_conv[li],
                               state_conf_conv[li])
        outs_p.append(st_p)
        outs_s.append(st_s)
        h = _peer(h, lw, u_all, vt_all, fw, li, li == depth - 1)
    y_p = _drop_meta(h, b_p, seq, N_META)
    y_s = jnp.swapaxes(h[n_p:n_p + n_s].reshape(steps, b_s, D_MODEL), 0, 1)
    stack = lambda outs, i: jnp.stack([o[i] for o in outs])
    return (y_p, y_s, stack(outs_p, 0), stack(outs_p, 1), stack(outs_p, 2),
            stack(outs_s, 0), stack(outs_s, 1), stack(outs_s, 2))
```

```python
import functools
import math

import jax
import jax.numpy as jnp
from jax import lax
from jax.experimental import pallas as pl
from jax.experimental.pallas import tpu as pltpu

F32 = jnp.float32
BF16 = jnp.bfloat16

D_MODEL = 1024
N_META = 16
GDN_HEADS = 4
HEAD_DIM = 128
GDN_WIDTH = GDN_HEADS * HEAD_DIM
CONF_WIDTH = 512
QKV_WIDTH = 3 * GDN_WIDTH
GDN_CONV = 4
GDN_CHUNK = 64
GDN_GROUP = 8
CONF_CONV = 31
N_KEYS = 128
PEER_HEADS = 8
PEER_TOPK = 16
EPS = 1e-6

LANES = 128
SUBLANES = 8
VMEM_LIMIT_BYTES = 56 * 1024 * 1024


def _sigmoid(x):
    return 1.0 / (1.0 + jnp.exp(-x))


def _silu(x):
    return x * _sigmoid(x)


def _softplus(x):
    return jnp.maximum(x, 0.0) + jnp.log1p(jnp.exp(-jnp.abs(x)))


def _rmsnorm(x, w):
    return x * lax.rsqrt(jnp.mean(x * x, axis=-1, keepdims=True) + EPS) * w


def _dot(a, b, precision=None):
    return jnp.dot(a, b, preferred_element_type=F32, precision=precision)


def _dot_nt(a, b, precision=None):
    return lax.dot_general(a, b, (((1,), (1,)), ((), ())), preferred_element_type=F32,
                           precision=precision)


def _dot_tn(a, b, precision=None):
    return lax.dot_general(a, b, (((0,), (0,)), ((), ())), preferred_element_type=F32,
                           precision=precision)


def _split(a):
    hi = a.astype(BF16)
    return hi, a - hi.astype(F32)


def _dot3(a, b):
    a_hi, a_rest = _split(a)
    b_hi, b_rest = _split(b)
    return _dot(a_hi, b_hi) + (_dot(a_hi, b_rest.astype(BF16)) + _dot(a_rest.astype(BF16), b_hi))


def _dot_exact_lhs(a_bf16, b):
    b0, r = _split(b)
    b1, r = _split(r)
    return _dot(a_bf16, b0) + (_dot(a_bf16, b1) + _dot(a_bf16, r.astype(BF16)))


def _pick_tile(n, candidates):
    for c in candidates:
        if n % c == 0:
            return c
    return n


def _params(*semantics):
    return pltpu.CompilerParams(dimension_semantics=semantics, vmem_limit_bytes=VMEM_LIMIT_BYTES)


def _in_proj_kernel(h_ref, nw_ref, wqkvz_ref, wab_ref, wglu_ref, alog_ref, dtb_ref,
                    qkv_ref, z_ref, gb_ref, glu_ref):
    xb = _rmsnorm(h_ref[...], nw_ref[...]).astype(BF16)
    p = _dot(xb, wqkvz_ref[...])
    qkv_ref[...] = p[:, :QKV_WIDTH]
    z_ref[...] = p[:, QKV_WIDTH:]
    ab = _dot(xb, wab_ref[...])
    g = -jnp.exp(alog_ref[...]) * _softplus(ab + dtb_ref[...])
    lane = lax.broadcasted_iota(jnp.int32, ab.shape, 1)
    gb_ref[...] = jnp.where(lane < GDN_HEADS, g, _sigmoid(ab))
    pg = _dot(xb, wglu_ref[...])
    glu_ref[...] = pg[:, :CONF_WIDTH] * _sigmoid(pg[:, CONF_WIDTH:])


def _in_proj(h, nw, wqkvz, wab, wglu, alog, dtb, n_rows=None):
    n = h.shape[0] if n_rows is None else n_rows
    tm = _pick_tile(n, (384, 256, 128))
    row = lambda i: (i, 0)
    fixed = lambda i: (0, 0)
    widths = (QKV_WIDTH, GDN_WIDTH, LANES, CONF_WIDTH)
    return pl.pallas_call(
        _in_proj_kernel,
        grid=(n // tm,),
        in_specs=[
            pl.BlockSpec((tm, D_MODEL), row),
            pl.BlockSpec((1, D_MODEL), fixed),
            pl.BlockSpec(wqkvz.shape, fixed),
            pl.BlockSpec(wab.shape, fixed),
            pl.BlockSpec(wglu.shape, fixed),
            pl.BlockSpec((1, LANES), fixed),
            pl.BlockSpec((1, LANES), fixed),
        ],
        out_specs=[pl.BlockSpec((tm, w), row) for w in widths],
        out_shape=[jax.ShapeDtypeStruct((n, w), F32) for w in widths],
        compiler_params=_params("parallel"),
        name="in_proj",
    )(h, nw, wqkvz, wab, wglu, alog, dtb)


def _gdn_kernel(xq_ref, xk_ref, xv_ref, gb_ref, z_ref, cs_q_ref, cs_k_ref, cs_v_ref, s0_ref,
                cwq_ref, cwk_ref, cwv_ref, nw_ref,
                o_ref, s_ref,
                xq_s, xk_s, xv_s, g_s, b_s, u_s, att_s, wq_s, m_s, bm_s, gl_s, sall_s,
                *, chunk, n_pad, seq, bb_count):
    C = chunk
    n_chunks = (seq + n_pad) // C
    head = 8 + n_pad
    n_fac = int(math.log2(C))
    row_i = lax.broadcasted_iota(jnp.int32, (C, C), 0)
    col_i = lax.broadcasted_iota(jnp.int32, (C, C), 1)
    incl = row_i >= col_i
    strict = row_i > col_i
    tril_b = jnp.where(incl, 1.0, 0.0).astype(BF16)
    src_lane = lax.broadcasted_iota(jnp.int32, (LANES, HEAD_DIM), 0)
    pick_gb = jnp.concatenate([jnp.where(src_lane == pl.program_id(1) + off, 1.0, 0.0).astype(BF16)
                               for off in (0, GDN_HEADS)], axis=1)
    eye =jnp.where(row_i == col_i, 1.0, 0.0).astype(F32)
    group = GDN_GROUP
    live = lax.broadcasted_iota(jnp.int32, (C, HEAD_DIM), 0) >= n_pad
    cws = (cwq_ref[...], cwk_ref[...], cwv_ref[...])
    nw = nw_ref[...]

    def conv(blk, cw):
        acc = blk[5:5 + C] * cw[0:1]
        for j in range(1, GDN_CONV):
            acc = acc + blk[5 + j:5 + j + C] * cw[j:j + 1]
        return _silu(acc)

    slot = lambda bb, c: bb * n_chunks + c

    is_first = lambda c: isinstance(c, int) and c == 0

    def build(items, fill=()):
        fill = list(fill)
        tick = lambda: fill.pop(0)() if fill else None
        each = lambda f, *ls: [f(*a) for a in zip(*ls)]
        at = [(bb, c * C if isinstance(c, int) else pl.multiple_of(c * C, SUBLANES)) for bb, c in items]
        qc = [conv(xq_s[bb, pl.ds(t0, C + 8), :], cws[0]) for bb, t0 in at]
        kc = [conv(xk_s[bb, pl.ds(t0, C + 8), :], cws[1]) for bb, t0 in at]
        v = [conv(xv_s[bb, pl.ds(t0, C + 8), :], cws[2]) for bb, t0 in at]
        q = each(lambda a: a * lax.rsqrt(jnp.sum(a * a, axis=-1, keepdims=True) + EPS) * (HEAD_DIM ** -0.5), qc)
        k = each(lambda a: a * lax.rsqrt(jnp.sum(a * a, axis=-1, keepdims=True) + EPS), kc)
        g = [g_s[bb, pl.ds(t0, C), :] for bb, t0 in at]
        beta = [b_s[bb, pl.ds(t0, C), :] for bb, t0 in at]
        if n_pad:
            q, k, v = ([jnp.where(live, a, 0.0) if is_first(c) else a for a, (_, c) in zip(l, items)]
                       for l in (q, k, v))
        cum = each(lambda a: _dot_exact_lhs(tril_b, jnp.concatenate([a, jnp.where(strict, a[:, :C], 0.0)], axis=1)), g)
        tick()
        gcum = [a[:, :HEAD_DIM] for a in cum]
        decay = [jnp.where(incl, jnp.exp(a[:, HEAD_DIM:]), 0.0) for a in cum]
        eg = [jnp.exp(a) for a in gcum]
        kb = each(lambda a, b_: a * b_, k, beta)
        kq = each(lambda a, b_, c_: _dot_nt(jnp.concatenate([a, b_], axis=0).astype(BF16), c_.astype(BF16)), kb, q, k)
        tick()
        p = each(lambda a, d: jnp.where(strict, -(a[:C] * d), 0.0), kq, decay)
        att = each(lambda a, d: jnp.where(incl, a[C:] * d, 0.0), kq, decay)
        t = [eye + a for a in p]
        p = each(lambda a: _dot3(a, a), p)
        tick()
        for step in range(1, n_fac):
            tick()
            if step == n_fac - 1:
                t = each(lambda a, b_: a + _dot3(b_, a), t, p)
            else:
                y = each(lambda a, b_: _dot3(b_, jnp.concatenate([a, b_], axis=1)), t, p)
                t = each(lambda a, b_: a + b_[:, :C], t, y)
                p = [a[:, C:] for a in y]
        x = each(lambda t_, v_, b_, kb_, eg_: _dot3(t_, jnp.concatenate([v_ * b_, kb_ * eg_], axis=1)),
                 t, v, beta, kb, eg)
        while fill:
            tick()
        g_last = [a[C - 1:C, :] for a in gcum]
        kt = each(lambda k_, gl, gc: k_ * jnp.exp(gl - gc), k, g_last, gcum)
        ktx = each(lambda a, b_: _dot_tn(a.astype(BF16), b_.astype(BF16)), kt, x)
        for i, (bb, c) in enumerate(items):
            n = slot(bb, c)
            u_s[n] = x[i][:, :HEAD_DIM]
            att_s[n] = att[i].astype(BF16)
            wq_s[n] = jnp.concatenate([x[i][:, HEAD_DIM:], q[i] * eg[i]], axis=0).astype(BF16)
            bm_s[n] = ktx[i][:, :HEAD_DIM]
            m_s[n] = (-ktx[i][:, HEAD_DIM:]).astype(BF16)
            gl_s[n] = jnp.exp(g_last[i])

    def carry(items):
        s = [s_ref[bb, 0] for bb, _ in items]
        for (bb, c), s_i in zip(items, s):
            sall_s[slot(bb, c)] = s_i
        new = [gl_s[slot(bb, c)] * s_i + _dot(m_s[slot(bb, c)], s_i.astype(BF16)) + bm_s[slot(bb, c)]
               for (bb, c), s_i in zip(items, s)]
        for (bb, _), s_i in zip(items, new):
            s_ref[bb, 0] = s_i

    def emit(items):
        ns = [slot(bb, c) for bb, c in items]
        ws_qs = [_dot(wq_s[n], sall_s[n].astype(BF16)) for n in ns]
        v_new = [u_s[n] - a[:C] for n, a in zip(ns, ws_qs)]
        o = [a[C:] + _dot(att_s[n], vn.astype(BF16)) for n, a, vn in zip(ns, ws_qs, v_new)]
        o = [_rmsnorm(a, nw) for a in o]
        for (bb, c), a in zip(items, o):
            if is_first(c):
                rows = C - n_pad
                o_ref[bb, 0:rows, :] = (a[n_pad:] * _silu(z_ref[bb, 0:rows, :])).astype(BF16)
            else:
                r0 = c * C - n_pad if isinstance(c, int) else pl.multiple_of(c * C - n_pad, 2 * SUBLANES)
                o_ref[bb, pl.ds(r0, C), :] = (a * _silu(z_ref[bb, pl.ds(r0, C), :])).astype(BF16)

    lead = (n_chunks - 1) % group + 1
    if lead == 1 and n_chunks > 1:
        lead += group
    leading = [(bb, c) for bb in range(bb_count) for c in range(lead)]

    n_loop = (n_chunks - lead) // group
    later = lambda i: [(0, lead + i * group + j) for j in range(group)]
    carry_range = lambda lo, hi: lax.fori_loop(lo, hi, lambda c, _: (carry([(0, c)]), 0)[1], 0)

    for bb in range(bb_count):
        for x_s, x_ref, cs_ref in ((xq_s, xq_ref, cs_q_ref), (xk_s, xk_ref, cs_k_ref),
                                   (xv_s, xv_ref, cs_v_ref)):
            x_s[bb, 0:head, :] = jnp.zeros((head, HEAD_DIM), F32)
            x_s[bb, head - (GDN_CONV - 1):head, :] = cs_ref[bb]
            x_s[bb, head:head + seq, :] = x_ref[bb]
        if n_pad:
            g_s[bb, 0:n_pad, :] = jnp.zeros((n_pad, HEAD_DIM), F32)
            b_s[bb, 0:n_pad, :] = jnp.zeros((n_pad, HEAD_DIM), F32)
        pieces = []
        rest = gb_ref[bb]
        for _ in range(3):
            hi, rest = _split(rest)
            pieces.append(hi)
        both = _dot(pieces[0], pick_gb) + (_dot(pieces[1], pick_gb) + _dot(pieces[2], pick_gb))
        g_s[bb, n_pad:n_pad + seq, :] = both[:, :HEAD_DIM]
        b_s[bb, n_pad:n_pad + seq, :] = both[:, HEAD_DIM:]
        s_ref[bb, 0] = s0_ref[bb, 0]

    build(leading)
    carry([(bb, 0) for bb in range(bb_count)])

    def build_later(i, _):
        build(later(i), [functools.partial(carry, [(0, 1 + i * group + j)]) for j in range(group)])
        return 0

    if n_loop:
        lax.fori_loop(0, n_loop, build_later, 0)
    if n_chunks > 1:
        carry_range(1 + n_loop * group, n_chunks)
    emit(leading)
    if n_loop:
        lax.fori_loop(0, n_loop, lambda i, _: (emit(later(i)), 0)[1], 0)


def _gdn(qkv, gb, z, conv_state, s0, conv_w, norm_w, *, chunk, n_pad, bb_count, layer=None, s_prev=None):
    b, seq, _ = qkv.shape
    n_chunks = (seq + n_pad) // chunk
    assert n_chunks * chunk == seq + n_pad and b % bb_count == 0
    assert n_chunks == 1 or bb_count == 1
    col = lambda off: (lambda i, h: (i, 0, off + h))
    wcol = lambda off: (lambda i, h: (0, off + h))
    seq_blk = (bb_count, seq, HEAD_DIM)
    cs_blk = (bb_count, GDN_CONV - 1, HEAD_DIM)
    if layer is None:
        s_blk = (bb_count, 1, HEAD_DIM, HEAD_DIM)
        s_map = lambda i, h: (i, h, 0, 0)
    else:
        s_blk = (None, bb_count, 1, HEAD_DIM, HEAD_DIM)
        s_map = lambda i, h: (layer, i, h, 0, 0)
    ext = chunk * n_chunks
    per_chunk = lambda r, c, dt=F32: pltpu.VMEM((bb_count * n_chunks, r, c), dt)
    n_in = 13
    extra = [] if s_prev is None else [s_prev]

    def kern(*refs):
        _gdn_kernel(*refs[:n_in], *refs[n_in + len(extra):], chunk=chunk, n_pad=n_pad, seq=seq, bb_count=bb_count)

    return pl.pallas_call(
        kern,
        grid=(b // bb_count, GDN_HEADS),
        in_specs=[
            pl.BlockSpec(seq_blk, col(0)),
            pl.BlockSpec(seq_blk, col(GDN_HEADS)),
            pl.BlockSpec(seq_blk, col(2 * GDN_HEADS)),
            pl.BlockSpec((bb_count, seq, LANES), lambda i, h: (i, 0, 0)),
            pl.BlockSpec(seq_blk, col(0)),
            pl.BlockSpec(cs_blk, col(0)),
            pl.BlockSpec(cs_blk, col(GDN_HEADS)),
            pl.BlockSpec(cs_blk, col(2 * GDN_HEADS)),
            pl.BlockSpec(s_blk, s_map),
            pl.BlockSpec((GDN_CONV, HEAD_DIM), wcol(0)),
            pl.BlockSpec((GDN_CONV, HEAD_DIM), wcol(GDN_HEADS)),
            pl.BlockSpec((GDN_CONV, HEAD_DIM), wcol(2 * GDN_HEADS)),
            pl.BlockSpec((1, HEAD_DIM), lambda i, h: (0, 0)),
        ] + [pl.BlockSpec(memory_space=pl.ANY) for _ in extra],
        out_specs=[pl.BlockSpec(seq_blk, col(0)), pl.BlockSpec(s_blk, s_map)],
        out_shape=[jax.ShapeDtypeStruct((b, seq, GDN_WIDTH), BF16),
                   jax.ShapeDtypeStruct(s0.shape, F32)],
        input_output_aliases={n_in: 1} if extra else {},
        scratch_shapes=[
            pltpu.VMEM((bb_count, ext + 8, HEAD_DIM), F32),
            pltpu.VMEM((bb_count, ext + 8, HEAD_DIM), F32),
            pltpu.VMEM((bb_count, ext + 8, HEAD_DIM), F32),
            pltpu.VMEM((bb_count, ext, HEAD_DIM), F32),
            pltpu.VMEM((bb_count, ext, HEAD_DIM), F32),
            per_chunk(chunk, HEAD_DIM),
            per_chunk(chunk, chunk, BF16),
            per_chunk(2 * chunk, HEAD_DIM, BF16),
            per_chunk(HEAD_DIM, HEAD_DIM, BF16),
            per_chunk(HEAD_DIM, HEAD_DIM),
            per_chunk(1, HEAD_DIM),
            per_chunk(HEAD_DIM, HEAD_DIM),
        ],
        compiler_params=_params("parallel", "parallel"),
        name="gdn",
    )(qkv, qkv, qkv, gb, z, conv_state, conv_state, conv_state, s0,
      conv_w, conv_w, conv_w, norm_w, *extra)


CONF_HALO = 32
CONF_ROWS = 16


def _conf_out_seq_kernel(glu_ref, o_ref, h_ref, cstate_ref, dw_ref, dwb_ref, lnw_ref, lnb_ref, wout_ref,
                         out_ref, ext_s, c_s, sh_s, *, tt):
    t = pl.program_id(1)
    lead = CONF_HALO - (CONF_CONV - 1)

    @pl.when(t == 0)
    def _():
        ext_s[0:lead, :] = jnp.zeros((lead, CONF_WIDTH), F32)
        ext_s[lead:CONF_HALO, :] = cstate_ref[0]

    @pl.when(t > 0)
    def _():
        ext_s[0:CONF_HALO, :] = ext_s[tt:tt + CONF_HALO, :]

    ext_s[CONF_HALO:CONF_HALO + tt, :] = glu_ref[0]
    dw = dw_ref[...]
    bias = dwb_ref[...]

    def rows(i, _):
        r0 = pl.multiple_of(i * CONF_ROWS, SUBLANES)
        blk = ext_s[pl.ds(r0, CONF_ROWS + CONF_HALO), :]
        span = CONF_ROWS + CONF_HALO - SUBLANES
        for s in range(1, SUBLANES):
            sh_s[s - 1] = blk[s:s + span]
        acc = bias
        for j in range(CONF_CONV):
            q, s = divmod(lead + j, SUBLANES)
            rows_j = slice(q * SUBLANES, q * SUBLANES + CONF_ROWS)
            acc = acc + (blk[rows_j] if s == 0 else sh_s[s - 1, rows_j, :]) * dw[j:j + 1]
        c_s[pl.ds(r0, CONF_ROWS), :] = acc
        return 0

    lax.fori_loop(0, tt // CONF_ROWS, rows, 0)
    c = c_s[...]
    mu = jnp.mean(c, axis=-1, keepdims=True)
    xc = c - mu
    var = jnp.mean(xc * xc, axis=-1, keepdims=True)
    c = _silu(xc * lax.rsqrt(var + EPS) * lnw_ref[...] + lnb_ref[...])
    mixed = jnp.concatenate([o_ref[0], c.astype(BF16)], axis=1)
    out_ref[...] = h_ref[...] + _dot(mixed, wout_ref[...])


def _conf_out_seq(glu, o, h, cstate, dw, dwb, lnw, lnb, wout):
    b, seq, _ = glu.shape
    tt = _pick_tile(seq, (688, 512, 256, 128, 64, 16))
    per_seq = seq // tt
    tile = lambda w: pl.BlockSpec((1, tt, w), lambda i, t: (i, t, 0))
    flat = pl.BlockSpec((tt, D_MODEL), lambda i, t: (i * per_seq + t, 0))
    fixed = lambda a: pl.BlockSpec(a.shape, lambda i, t: (0,) * a.ndim)
    return pl.pallas_call(
        functools.partial(_conf_out_seq_kernel, tt=tt),
        grid=(b, per_seq),
        in_specs=[tile(CONF_WIDTH), tile(GDN_WIDTH), flat,
                  pl.BlockSpec((1, CONF_CONV - 1, CONF_WIDTH), lambda i, t: (i, 0, 0)),
                  fixed(dw), fixed(dwb), fixed(lnw), fixed(lnb), fixed(wout)],
        out_specs=flat,
        out_shape=jax.ShapeDtypeStruct(h.shape, F32),
        input_output_aliases={2: 0},
        scratch_shapes=[pltpu.VMEM((tt + CONF_HALO, CONF_WIDTH), F32),
                        pltpu.VMEM((tt, CONF_WIDTH), F32),
                        pltpu.VMEM((SUBLANES - 1, CONF_ROWS + CONF_HALO - SUBLANES, CONF_WIDTH), F32)],
        compiler_params=_params("parallel", "arbitrary"),
        name="conf_out_seq",
    )(glu, o, h, cstate, dw, dwb, lnw, lnb, wout)


def _conf_out_step_kernel(cext_ref, o_ref, h_ref, dw_ref, dwb_ref, lnw_ref, lnb_ref, wout_ref, out_ref,
                          *, steps):
    t = pl.program_id(0)

    @pl.when(t < steps)
    def _():
        dw = dw_ref[...]
        c = dwb_ref[...] + cext_ref[t] * dw[0:1]
        for j in range(1, CONF_CONV):
            c = c + cext_ref[t + j] * dw[j:j + 1]
        mu = jnp.mean(c, axis=-1, keepdims=True)
        xc = c - mu
        var = jnp.mean(xc * xc, axis=-1, keepdims=True)
        c = _silu(xc * lax.rsqrt(var + EPS) * lnw_ref[...] + lnb_ref[...])
        mixed = jnp.concatenate([o_ref[...], c.astype(BF16)], axis=1)
        out_ref[...] = h_ref[...] + _dot(mixed, wout_ref[...])

    @pl.when(t >= steps)
    def _():
        out_ref[...] = jnp.zeros_like(out_ref)


def _conf_out_step(cext, o, h, row0, n_zero, dw, dwb, lnw, lnb, wout):
    steps = cext.shape[0] - (CONF_CONV - 1)
    b = cext.shape[1]
    assert row0 % b == 0 and n_zero % b == 0 and row0 + steps * b + n_zero <= h.shape[0]
    full = lambda a: pl.BlockSpec(a.shape, lambda t: (0,) * a.ndim)
    rows = pl.BlockSpec((b, D_MODEL), lambda t: (row0 // b + t, 0))
    return pl.pallas_call(
        functools.partial(_conf_out_step_kernel, steps=steps),
        grid=(steps + n_zero // b,),
        in_specs=[full(cext), pl.BlockSpec((b, GDN_WIDTH), lambda t: (jnp.minimum(t, steps - 1), 0)), rows,
                  full(dw), full(dwb), full(lnw), full(lnb), full(wout)],
        out_specs=rows,
        out_shape=jax.ShapeDtypeStruct(h.shape, F32),
        input_output_aliases={2: 0},
        compiler_params=_params("arbitrary"),
        name="conf_out_step",
    )(cext, o, h, dw, dwb, lnw, lnb, wout)


N_RANKS = PEER_TOPK + 1
RANK_ROWS = 24
NEG_INF = float("-inf")


def _top_rows(s, count):
    rows = []
    for r in range(count):
        m = jnp.max(s, axis=0, keepdims=True)
        rows.append(m)
        s = jnp.where(s == m, NEG_INF, s)
    return rows


TOP_WAYS = 4


def _top_rows_chained(s, count):
    n = s.shape[0] // TOP_WAYS
    lv = [s[i * n:(i + 1) * n] for i in range(TOP_WAYS)]
    for i, j in ((0, 1), (2, 3), (0, 2), (1, 3), (1, 2)):
        lv[i], lv[j] = jnp.maximum(lv[i], lv[j]), jnp.minimum(lv[i], lv[j])
    rows = []
    for r in range(count):
        m = jnp.max(lv[0], axis=0, keepdims=True)
        rows.append(m)
        hit = lv[0] == m
        for i in range(TOP_WAYS - 1):
            lv[i] = jnp.where(hit, lv[i + 1], lv[i])
        lv[-1] = jnp.where(hit, NEG_INF, lv[-1])
    return rows


def _prefix_count(hits):
    out = jnp.zeros(hits[0].shape, F32)
    for r, hit in enumerate(hits):
        out = jnp.where(hit, float(r + 1), out)
    return out


def _stack_rows(rows, height, tm):
    idx = lax.broadcasted_iota(jnp.int32, (height, tm), 0)
    out = jnp.full((height, tm), NEG_INF, F32)
    for r, row in enumerate(rows):
        out = jnp.where(idx == r, row, out)
    return out


def _peer_score_kernel(h_ref, nw_ref, wqt_ref, keys_ref, xnt_ref, cnt_ref, e1_ref, rank2_ref, e2_ref):
    tm = h_ref.shape[0]
    xn = _rmsnorm(h_ref[...], nw_ref[...])
    xnt = xn.T.astype(BF16)
    xnt_ref[...] = xnt
    qt = _dot(wqt_ref[...], xnt).astype(BF16)
    idx8 = lax.broadcasted_iota(jnp.int32, (SUBLANES, tm), 0)
    for hh in range(PEER_HEADS):
        s = []
        for p in range(2):
            r0 = (hh * 2 + p) * N_KEYS
            s.append(_dot(keys_ref[hh, p], qt[r0:r0 + N_KEYS, :]))
        a = _top_rows_chained(s[0], N_RANKS)
        b = _top_rows_chained(s[1], N_RANKS)
        a_st = _stack_rows(a, RANK_ROWS, tm)
        b_st = _stack_rows(b, RANK_ROWS, tm)
        cands = [a[0] + b_st,
                 jnp.where(lax.broadcasted_iota(jnp.int32, (RANK_ROWS, tm), 0) >= 1,
                           a_st + b[0], NEG_INF)]
        for r1 in range(1, N_RANKS):
            hi = N_RANKS // (r1 + 1) - 1
            if hi >= 1 and r1 <= 4:
                cands.append(jnp.where((idx8 >= 1) & (idx8 <= hi), a[r1] + b_st[0:SUBLANES], NEG_INF))
        cands.append(jnp.where((idx8 >= 5) & (idx8 <= N_RANKS // 2 - 1),
                               a_st[0:SUBLANES] + b[1], NEG_INF))
        cand = jnp.concatenate(cands, axis=0)
        top = _top_rows(cand, N_RANKS)
        z = jnp.zeros_like(top[0])
        for r in range(PEER_TOPK):
            z = z + jnp.exp(top[r] - top[0])
        thr = 0.5 * (top[PEER_TOPK - 1] + top[PEER_TOPK])
        cnt1 = _prefix_count([s[0] >= thr - b[r2] for r2 in range(PEER_TOPK)])
        rank2 = _prefix_count([s[1] < b[r] for r in range(PEER_TOPK)])
        cnt_ref[hh] = cnt1
        e1_ref[hh] = 0.5 * jnp.exp(s[0] - a[0]) / z
        rank2_ref[hh] = rank2.astype(BF16)
        e2_ref[hh] = jnp.exp(s[1] - b[0]).astype(BF16)


def _peer_score(h, nw, wqt, keys):
    n = h.shape[0]
    tm = _pick_tile(n, (PEER_TOKENS, 256, 128))
    heads = lambda: pl.BlockSpec((PEER_HEADS, N_KEYS, tm), lambda i: (0, 0, i))
    return pl.pallas_call(
        _peer_score_kernel,
        grid=(n // tm,),
        in_specs=[pl.BlockSpec((tm, D_MODEL), lambda i: (i, 0)),
                  pl.BlockSpec((1, D_MODEL), lambda i: (0, 0)),
                  pl.BlockSpec(wqt.shape, lambda i: (0, 0)),
                  pl.BlockSpec(keys.shape, lambda i: (0, 0, 0, 0))],
        out_specs=[pl.BlockSpec((D_MODEL, tm), lambda i: (0, i)), heads(), heads(), heads(), heads()],
        out_shape=[jax.ShapeDtypeStruct((D_MODEL, n), BF16)]
        + [jax.ShapeDtypeStruct((PEER_HEADS, N_KEYS, n), dt) for dt in (F32, F32, BF16, BF16)],
        compiler_params=_params("parallel"),
        name="peer_score",
    )(h, nw, wqt, keys)


EXPERT_CHUNK = 1024
PEER_TOKENS = 512
PEER_DENSE_TOKENS = 512
SQRT_HALF = math.sqrt(0.5)


def _peer_dense_kernel(xnt_ref, cnt_ref, e1_ref, rank2_ref, e2_ref, u_ref, vt_ref, h_ref, fw_ref,
                       out_ref, acc_s, w_s, gate_s, cnt_s, e1_s, *, final_norm, n_chunks):
    s = pl.program_id(0)
    c = lax.rem(s, n_chunks)
    cur = lax.rem(c, 2)
    tm = xnt_ref.shape[1]
    half = tm // 2

    @pl.when(s == 0)
    def _():
        acc_s[...] = jnp.zeros_like(acc_s)
        w_s[1] = jnp.zeros(w_s.shape[1:], BF16)

    prev = 1 - cur
    n_blocks = EXPERT_CHUNK // N_KEYS
    n_lt = tm // LANES
    per_lt = PEER_HEADS * n_blocks
    for l in range(n_lt):
        cnt_s[l * per_lt:(l + 1) * per_lt, :] = cnt_ref[:, :, l * LANES:(l + 1) * LANES].reshape(per_lt, LANES)
        e1_s[l * per_lt:(l + 1) * per_lt, :] = e1_ref[:, :, l * LANES:(l + 1) * LANES].reshape(per_lt, LANES)

    def rows(ref, hh, j):
        words = jnp.concatenate([jnp.broadcast_to(ref[pl.ds(l * per_lt + hh * n_blocks + j, 1), :], (SUBLANES, LANES))
                                 for l in range(n_lt)], axis=1)
        packed = jnp.concatenate([words, words], axis=0).astype(BF16)
        return jnp.concatenate([packed] * (N_KEYS // (2 * SUBLANES)), axis=0)

    xnt = xnt_ref[...]
    block = lambda j: slice(j * N_KEYS, (j + 1) * N_KEYS)

    def gate(j):
        total = None
        for hh in range(PEER_HEADS):
            e2 = e2_ref[hh]
            sel = jnp.where(rank2_ref[hh] < rows(cnt_s, hh, j), e2, jnp.zeros_like(e2))
            term = rows(e1_s, hh, j) * sel
            total = term if total is None else total + term
        gate_s[block(j), :] = total

    def project(j):
        return _dot(u_ref[block(j), :], xnt)

    def finish(j, pre):
        act = (pre * (1.0 + lax.erf(pre * SQRT_HALF))).astype(BF16)
        w_s[cur, block(j), :] = act * gate_s[block(j), :]

    ahead = 3
    pres = {j: project(j) for j in range(ahead)}
    gate(0)
    for j in range(n_blocks):
        if j + ahead < n_blocks:
            pres[j + ahead] = project(j + ahead)
        if j == 0:
            acc_s[:, :half] += _dot(vt_ref[0], w_s[prev, :, :half])
        if j == n_blocks // 2:
            acc_s[:, half:] += _dot(vt_ref[0], w_s[prev, :, half:])
        if j + 1 < n_blocks:
            gate(j + 1)
        finish(j, pres.pop(j))

    @pl.when(c == 0)
    def _():
        y = h_ref[...] + acc_s[...].T
        if final_norm:
            y = _rmsnorm(y, fw_ref[...])
        out_ref[...] = y
        acc_s[...] = jnp.zeros_like(acc_s)


def _peer_dense(xnt, cnt, e1, rank2, e2, u, vt, h, fw, *, layer, final_norm):
    n = h.shape[0]
    tm = _pick_tile(n, (PEER_DENSE_TOKENS, PEER_TOKENS, 256, 128))
    n_tiles = n // tm
    n_chunks = u.shape[1] // EXPERT_CHUNK
    assert n_chunks % 2 == 0
    rows_per_chunk = EXPERT_CHUNK // N_KEYS
    tile = lambda s: jnp.minimum(s // n_chunks, n_tiles - 1)
    chunk = lambda s: lax.rem(s, n_chunks)
    done = lambda s: jnp.maximum(s - 1, 0) // n_chunks
    by_chunk = lambda: pl.BlockSpec((PEER_HEADS, rows_per_chunk, tm), lambda s: (0, chunk(s), tile(s)))
    by_tile = lambda: pl.BlockSpec((PEER_HEADS, N_KEYS, tm), lambda s: (0, 0, tile(s)))
    return pl.pallas_call(
        functools.partial(_peer_dense_kernel, final_norm=final_norm, n_chunks=n_chunks),
        grid=(n_tiles * n_chunks + 1,),
        in_specs=[pl.BlockSpec((D_MODEL, tm), lambda s: (0, tile(s))),
                  by_chunk(), by_chunk(), by_tile(), by_tile(),
                  pl.BlockSpec((None, EXPERT_CHUNK, D_MODEL), lambda s: (layer, chunk(s), 0)),
                  pl.BlockSpec((None, 1, D_MODEL, EXPERT_CHUNK), lambda s: (layer, chunk(s + n_chunks - 1), 0, 0)),
                  pl.BlockSpec((tm, D_MODEL), lambda s: (done(s), 0)),
                  pl.BlockSpec((1, D_MODEL), lambda s: (0, 0))],
        out_specs=pl.BlockSpec((tm, D_MODEL), lambda s: (done(s), 0)),
        out_shape=jax.ShapeDtypeStruct((n, D_MODEL), F32),
        scratch_shapes=[pltpu.VMEM((D_MODEL, tm), F32), pltpu.VMEM((2, EXPERT_CHUNK, tm), BF16),
                        pltpu.VMEM((EXPERT_CHUNK, tm), BF16)]
        + [pltpu.VMEM((PEER_HEADS * rows_per_chunk * tm // LANES, LANES), F32)] * 2,
        compiler_params=_params("arbitrary"),
        name="peer_dense",
    )(xnt, cnt, e1, rank2, e2, u, vt, h, fw)


def _prep_layer(layer, norm_mix_w, w_in, conv_qkv_w, a_log, dt_bias, gdn_norm_w, conf_dw_w, conf_dw_b,
                conf_ln_w, conf_ln_b, w_out, norm_ffn_w, w_query, sub_keys):
    s0 = QKV_WIDTH
    s1 = s0 + GDN_WIDTH
    s2 = s1 + GDN_HEADS
    s3 = s2 + GDN_HEADS
    w = w_in[layer]
    row = lambda v: v.reshape(1, -1)
    return dict(
        norm_mix_w=row(norm_mix_w[layer]),
        wqkvz=w[:, :s1].astype(BF16),
        wab=jnp.pad(w[:, s1:s3], ((0, 0), (0, LANES - 2 * GDN_HEADS))).astype(BF16),
        wglu=w[:, s3:].astype(BF16),
        alog=row(jnp.pad(a_log[layer], (0, LANES - GDN_HEADS))),
        dtb=row(jnp.pad(dt_bias[layer], (0, LANES - GDN_HEADS))),
        conv_w=conv_qkv_w[layer],
        gdn_norm_w=row(gdn_norm_w[layer]),
        dw=conf_dw_w[layer], dwb=row(conf_dw_b[layer]),
        lnw=row(conf_ln_w[layer]), lnb=row(conf_ln_b[layer]),
        wout=w_out[layer].astype(BF16),
        norm_ffn_w=row(norm_ffn_w[layer]),
        wqt=w_query[layer].T.astype(BF16),
        keys=sub_keys[layer].astype(BF16),
    )


def _peer(h, lw, u, vt, fw, layer, final_norm):
    xnt, cnt, e1, rank2, e2 = _peer_score(h, lw["norm_ffn_w"], lw["wqt"], lw["keys"])
    return _peer_dense(xnt, cnt, e1, rank2, e2, u, vt, h, fw, layer=layer, final_norm=final_norm)


def _mix_prompt(h, lw, b, seq):
    n_pad = (-seq) % GDN_CHUNK
    qkv, z, gb, glu = _in_proj(h, lw["norm_mix_w"], lw["wqkvz"], lw["wab"], lw["wglu"], lw["alog"], lw["dtb"],
                               n_rows=b * seq)
    shp = lambda a: a.reshape(b, seq, a.shape[-1])
    qkv, z, gb, glu = shp(qkv), shp(z), shp(gb), shp(glu)
    o, s_new = _gdn(qkv, gb, z,
                    jnp.zeros((b, GDN_CONV - 1, QKV_WIDTH), F32),
                    jnp.zeros((b, GDN_HEADS, HEAD_DIM, HEAD_DIM), F32),
                    lw["conv_w"], lw["gdn_norm_w"], chunk=GDN_CHUNK, n_pad=n_pad, bb_count=1)
    h = _conf_out_seq(glu, o, h, jnp.zeros((b, CONF_CONV - 1, CONF_WIDTH), F32),
                      lw["dw"], lw["dwb"], lw["lnw"], lw["lnb"], lw["wout"])
    return h, s_new, qkv[:, seq - (GDN_CONV - 1):], glu[:, seq - (CONF_CONV - 1):]


def _mix_sample(h, row0, n_zero, lw, b, steps, layer, s0_all, s_prev, qkv_state, conf_state):
    n_pad = (-steps) % SUBLANES
    to_tb = lambda a: jnp.swapaxes(a, 0, 1)
    qkv, z, gb, glu = _in_proj(h[row0:row0 + steps * b], lw["norm_mix_w"], lw["wqkvz"], lw["wab"], lw["wglu"],
                               lw["alog"], lw["dtb"])
    bt = lambda a: to_tb(a.reshape(steps, b, a.shape[-1]))
    qkv_bt = bt(qkv)
    o, s_new = _gdn(qkv_bt, bt(gb), bt(z), qkv_state, s0_all, lw["conv_w"], lw["gdn_norm_w"],
                    chunk=steps + n_pad, n_pad=n_pad, bb_count=8, layer=layer, s_prev=s_prev)
    cext = jnp.concatenate([to_tb(conf_state), glu.reshape(steps, b, CONF_WIDTH)], axis=0)
    h = _conf_out_step(cext, to_tb(o).reshape(steps * b, GDN_WIDTH), h, row0, n_zero,
                       lw["dw"], lw["dwb"], lw["lnw"], lw["lnb"], lw["wout"])
    qkv_ext = jnp.concatenate([qkv_state, qkv_bt], axis=1)
    return (h, s_new, qkv_ext[:, qkv_ext.shape[1] - (GDN_CONV - 1):],
            to_tb(cext[cext.shape[0] - (CONF_CONV - 1):]))


def _assemble_kernel(meta_ref, x_ref, tail_ref, out_ref, *, n_seq, n_meta, n_tail):
    i = pl.program_id(0)

    @pl.when(i < n_seq)
    def _():
        out_ref[0:n_meta, :] = meta_ref[...]
        out_ref[n_meta:, :] = x_ref[0]

    @pl.when(i == n_seq)
    def _():
        out_ref[0:n_tail, :] = tail_ref[...]


def _assemble(meta, x, tail):
    b, seq0, d = x.shape
    n_meta, n_tail = meta.shape[0], tail.shape[0]
    seq = seq0 + n_meta
    assert 0 < n_tail <= seq and n_meta % SUBLANES == 0 and n_tail % SUBLANES == 0
    return pl.pallas_call(
        functools.partial(_assemble_kernel, n_seq=b, n_meta=n_meta, n_tail=n_tail),
        grid=(b + 1,),
        in_specs=[pl.BlockSpec((n_meta, d), lambda i: (0, 0)),
                  pl.BlockSpec((1, seq0, d), lambda i: (jnp.minimum(i, b - 1), 0, 0)),
                  pl.BlockSpec((n_tail, d), lambda i: (0, 0))],
        out_specs=pl.BlockSpec((seq, d), lambda i: (i, 0)),
        out_shape=jax.ShapeDtypeStruct((b * seq + n_tail, d), x.dtype),
        compiler_params=_params("arbitrary"),
        name="assemble",
    )(meta, x, tail)


DROP_ROWS = 512


def _drop_meta_kernel(h_ref, y_ref):
    y_ref[0] = h_ref[...]


def _drop_meta(h, b, seq, n_meta):
    real = seq - n_meta
    rows = _pick_tile(real, (DROP_ROWS, 256, 128, 64, 16))
    assert seq % SUBLANES == 0 and n_meta % SUBLANES == 0 and rows % SUBLANES == 0
    source = pl.BlockSpec((pl.Element(rows), pl.Element(D_MODEL)),
                          lambda i, j: (pl.multiple_of(i * seq + n_meta + j * rows, SUBLANES), 0))
    return pl.pallas_call(
        _drop_meta_kernel,
        grid=(b, real // rows),
        in_specs=[source],
        out_specs=pl.BlockSpec((1, rows, D_MODEL), lambda i, j: (i, j, 0)),
        out_shape=jax.ShapeDtypeStruct((b, real, D_MODEL), h.dtype),
        compiler_params=_params("parallel", "parallel"),
        name="drop_meta",
    )(h)


def kernel(x_prompt, x_sample, state_gdn, state_qkv_conv, state_conf_conv, meta_tokens, norm_mix_w, w_in,
           conv_qkv_w, a_log, dt_bias, gdn_norm_w, conf_dw_w, conf_dw_b, conf_ln_w, conf_ln_b, w_out,
           norm_ffn_w, w_query, sub_keys, expert_u, expert_v, final_norm_w):
    depth = w_in.shape[0]
    layers = [_prep_layer(l, norm_mix_w, w_in, conv_qkv_w, a_log, dt_bias, gdn_norm_w, conf_dw_w,
                          conf_dw_b, conf_ln_w, conf_ln_b, w_out, norm_ffn_w, w_query, sub_keys)
              for l in range(depth)]
    u_all = expert_u.astype(BF16)
    vt_all = jnp.swapaxes(expert_v.astype(BF16).reshape(depth, -1, EXPERT_CHUNK, D_MODEL), 2, 3)
    fw = final_norm_w.reshape(1, D_MODEL)
    b_p, seq0, _ = x_prompt.shape
    seq = seq0 + N_META
    b_s, steps, _ = x_sample.shape
    n_p, n_s = b_p * seq, b_s * steps
    n_fill = (-(n_p + n_s)) % PEER_DENSE_TOKENS
    h = _assemble(meta_tokens, x_prompt,
                  jnp.concatenate([jnp.swapaxes(x_sample, 0, 1).reshape(n_s, D_MODEL),
                                   jnp.zeros((n_fill, D_MODEL), F32)], axis=0))
    outs_p, outs_s = [], []
    s_sample = None
    for li, lw in enumerate(layers):
        h, *st_p = _mix_prompt(h, lw, b_p, seq)
        h, s_sample, *st_s = _mix_sample(h, n_p, n_fill, lw, b_s, steps, li, state_gdn, s_sample,
                                         state_qkv_conv[li], state_conf_conv[li])
        outs_p.append(st_p)
        outs_s.append(st_s)
        h = _peer(h, lw, u_all, vt_all, fw, li, li == depth - 1)
    y_p = _drop_meta(h, b_p, seq, N_META)
    y_s = jnp.swapaxes(h[n_p:n_p + n_s].reshape(steps, b_s, D_MODEL), 0, 1)
    stack = lambda outs, i: jnp.stack([o[i] for o in outs])
    return (y_p, y_s, stack(outs_p, 0), stack(outs_p, 1), stack(outs_p, 2),
            s_sample, stack(outs_s, 0), stack(outs_s, 1))
```

```python
import functools
import math

import jax
import jax.numpy as jnp
from jax import lax
from jax.experimental import pallas as pl
from jax.experimental.pallas import tpu as pltpu

F32 = jnp.float32
BF16 = jnp.bfloat16

D_MODEL = 1024
N_META = 16
GDN_HEADS = 4
HEAD_DIM = 128
GDN_WIDTH = GDN_HEADS * HEAD_DIM
CONF_WIDTH = 512
QKV_WIDTH = 3 * GDN_WIDTH
GDN_CONV = 4
GDN_CHUNK = 64
GDN_GROUP = 8
CONF_CONV = 31
N_KEYS = 128
PEER_HEADS = 8
PEER_TOPK = 16
EPS = 1e-6

LANES = 128
SUBLANES = 8
VMEM_LIMIT_BYTES = 56 * 1024 * 1024


def _sigmoid(x):
    return 1.0 / (1.0 + jnp.exp(-x))


def _silu(x):
    return x * _sigmoid(x)


def _softplus(x):
    return jnp.maximum(x, 0.0) + jnp.log1p(jnp.exp(-jnp.abs(x)))


def _rmsnorm(x, w):
    return x * lax.rsqrt(jnp.mean(x * x, axis=-1, keepdims=True) + EPS) * w


def _dot(a, b, precision=None):
    return jnp.dot(a, b, preferred_element_type=F32, precision=precision)


def _dot_nt(a, b, precision=None):
    return lax.dot_general(a, b, (((1,), (1,)), ((), ())), preferred_element_type=F32,
                           precision=precision)


def _dot_tn(a, b, precision=None):
    return lax.dot_general(a, b, (((0,), (0,)), ((), ())), preferred_element_type=F32,
                           precision=precision)


def _split(a):
    hi = a.astype(BF16)
    return hi, a - hi.astype(F32)


def _dot3(a, b):
    a_hi, a_rest = _split(a)
    b_hi, b_rest = _split(b)
    return _dot(a_hi, b_hi) + (_dot(a_hi, b_rest.astype(BF16)) + _dot(a_rest.astype(BF16), b_hi))


def _dot_exact_lhs(a_bf16, b):
    b0, r = _split(b)
    b1, r = _split(r)
    return _dot(a_bf16, b0) + (_dot(a_bf16, b1) + _dot(a_bf16, r.astype(BF16)))


def _pick_tile(n, candidates):
    for c in candidates:
        if n % c == 0:
            return c
    return n


def _params(*semantics):
    return pltpu.CompilerParams(dimension_semantics=semantics, vmem_limit_bytes=VMEM_LIMIT_BYTES)


def _in_proj_kernel(h_ref, nw_ref, wqkvz_ref, wab_ref, wglu_ref, alog_ref, dtb_ref,
                    qkv_ref, z_ref, gb_ref, glu_ref):
    xb = _rmsnorm(h_ref[...], nw_ref[...]).astype(BF16)
    p = _dot(xb, wqkvz_ref[...])
    qkv_ref[...] = p[:, :QKV_WIDTH]
    z_ref[...] = p[:, QKV_WIDTH:]
    ab = _dot(xb, wab_ref[...])
    g = -jnp.exp(alog_ref[...]) * _softplus(ab + dtb_ref[...])
    lane = lax.broadcasted_iota(jnp.int32, ab.shape, 1)
    gb_ref[...] = jnp.where(lane < GDN_HEADS, g, _sigmoid(ab))
    pg = _dot(xb, wglu_ref[...])
    glu_ref[...] = pg[:, :CONF_WIDTH] * _sigmoid(pg[:, CONF_WIDTH:])


def _in_proj(h, nw, wqkvz, wab, wglu, alog, dtb, n_rows=None):
    n = h.shape[0] if n_rows is None else n_rows
    tm = _pick_tile(n, (384, 256, 128))
    row = lambda i: (i, 0)
    fixed = lambda i: (0, 0)
    widths = (QKV_WIDTH, GDN_WIDTH, LANES, CONF_WIDTH)
    return pl.pallas_call(
        _in_proj_kernel,
        grid=(n // tm,),
        in_specs=[
            pl.BlockSpec((tm, D_MODEL), row),
            pl.BlockSpec((1, D_MODEL), fixed),
            pl.BlockSpec(wqkvz.shape, fixed),
            pl.BlockSpec(wab.shape, fixed),
            pl.BlockSpec(wglu.shape, fixed),
            pl.BlockSpec((1, LANES), fixed),
            pl.BlockSpec((1, LANES), fixed),
        ],
        out_specs=[pl.BlockSpec((tm, w), row) for w in widths],
        out_shape=[jax.ShapeDtypeStruct((n, w), F32) for w in widths],
        compiler_params=_params("parallel"),
        name="in_proj",
    )(h, nw, wqkvz, wab, wglu, alog, dtb)


def _gdn_kernel(xq_ref, xk_ref, xv_ref, gb_ref, z_ref, cs_q_ref, cs_k_ref, cs_v_ref, s0_ref,
                cwq_ref, cwk_ref, cwv_ref, nw_ref,
                o_ref, s_ref,
                xq_s, xk_s, xv_s, g_s, b_s, u_s, att_s, wq_s, m_s, bm_s, gl_s, sall_s,
                *, chunk, n_pad, seq, bb_count):
    C = chunk
    n_chunks = (seq + n_pad) // C
    head = 8 + n_pad
    n_fac = int(math.log2(C))
    row_i = lax.broadcasted_iota(jnp.int32, (C, C), 0)
    col_i = lax.broadcasted_iota(jnp.int32, (C, C), 1)
    incl = row_i >= col_i
    strict = row_i > col_i
    tril_b = jnp.where(incl, 1.0, 0.0).astype(BF16)
    src_lane = lax.broadcasted_iota(jnp.int32, (LANES, HEAD_DIM), 0)
    pick_gb = jnp.concatenate([jnp.where(src_lane == pl.program_id(1) + off, 1.0, 0.0).astype(BF16)
                               for off in (0, GDN_HEADS)], axis=1)
    eye =jnp.where(row_i == col_i, 1.0, 0.0).astype(F32)
    group = GDN_GROUP
    live = lax.broadcasted_iota(jnp.int32, (C, HEAD_DIM), 0) >= n_pad
    cws = (cwq_ref[...], cwk_ref[...], cwv_ref[...])
    nw = nw_ref[...]

    def conv(blk, cw):
        acc = blk[5:5 + C] * cw[0:1]
        for j in range(1, GDN_CONV):
            acc = acc + blk[5 + j:5 + j + C] * cw[j:j + 1]
        return _silu(acc)

    slot = lambda bb, c: bb * n_chunks + c

    is_first = lambda c: isinstance(c, int) and c == 0

    def build(items, fill=()):
        fill = list(fill)
        tick = lambda: fill.pop(0)() if fill else None
        each = lambda f, *ls: [f(*a) for a in zip(*ls)]
        at = [(bb, c * C if isinstance(c, int) else pl.multiple_of(c * C, SUBLANES)) for bb, c in items]
        qc = [conv(xq_s[bb, pl.ds(t0, C + 8), :], cws[0]) for bb, t0 in at]
        kc = [conv(xk_s[bb, pl.ds(t0, C + 8), :], cws[1]) for bb, t0 in at]
        v = [conv(xv_s[bb, pl.ds(t0, C + 8), :], cws[2]) for bb, t0 in at]
        q = each(lambda a: a * lax.rsqrt(jnp.sum(a * a, axis=-1, keepdims=True) + EPS) * (HEAD_DIM ** -0.5), qc)
        k = each(lambda a: a * lax.rsqrt(jnp.sum(a * a, axis=-1, keepdims=True) + EPS), kc)
        g = [g_s[bb, pl.ds(t0, C), :] for bb, t0 in at]
        beta = [b_s[bb, pl.ds(t0, C), :] for bb, t0 in at]
        if n_pad:
            q, k, v = ([jnp.where(live, a, 0.0) if is_first(c) else a for a, (_, c) in zip(l, items)]
                       for l in (q, k, v))
        cum = each(lambda a: _dot_exact_lhs(tril_b, jnp.concatenate([a, jnp.where(strict, a[:, :C], 0.0)], axis=1)), g)
        tick()
        gcum = [a[:, :HEAD_DIM] for a in cum]
        decay = [jnp.where(incl, jnp.exp(a[:, HEAD_DIM:]), 0.0) for a in cum]
        eg = [jnp.exp(a) for a in gcum]
        kb = each(lambda a, b_: a * b_, k, beta)
        kq = each(lambda a, b_, c_: _dot_nt(jnp.concatenate([a, b_], axis=0).astype(BF16), c_.astype(BF16)), kb, q, k)
        tick()
        p = each(lambda a, d: jnp.where(strict, -(a[:C] * d), 0.0), kq, decay)
        att = each(lambda a, d: jnp.where(incl, a[C:] * d, 0.0), kq, decay)
        t = [eye + a for a in p]
        p = each(lambda a: _dot3(a, a), p)
        tick()
        for step in range(1, n_fac):
            tick()
            if step == n_fac - 1:
                t = each(lambda a, b_: a + _dot3(b_, a), t, p)
            else:
                y = each(lambda a, b_: _dot3(b_, jnp.concatenate([a, b_], axis=1)), t, p)
                t = each(lambda a, b_: a + b_[:, :C], t, y)
                p = [a[:, C:] for a in y]
        x = each(lambda t_, v_, b_, kb_, eg_: _dot3(t_, jnp.concatenate([v_ * b_, kb_ * eg_], axis=1)),
                 t, v, beta, kb, eg)
        while fill:
            tick()
        g_last = [a[C - 1:C, :] for a in gcum]
        kt = each(lambda k_, gl, gc: k_ * jnp.exp(gl - gc), k, g_last, gcum)
        ktx = each(lambda a, b_: _dot_tn(a.astype(BF16), b_.astype(BF16)), kt, x)
        for i, (bb, c) in enumerate(items):
            n = slot(bb, c)
            u_s[n] = x[i][:, :HEAD_DIM]
            att_s[n] = att[i].astype(BF16)
            wq_s[n] = jnp.concatenate([x[i][:, HEAD_DIM:], q[i] * eg[i]], axis=0).astype(BF16)
            bm_s[n] = ktx[i][:, :HEAD_DIM]
            m_s[n] = (-ktx[i][:, HEAD_DIM:]).astype(BF16)
            gl_s[n] = jnp.exp(g_last[i])

    def carry(items):
        s = [s_ref[bb, 0] for bb, _ in items]
        for (bb, c), s_i in zip(items, s):
            sall_s[slot(bb, c)] = s_i
        new = [gl_s[slot(bb, c)] * s_i + _dot(m_s[slot(bb, c)], s_i.astype(BF16)) + bm_s[slot(bb, c)]
               for (bb, c), s_i in zip(items, s)]
        for (bb, _), s_i in zip(items, new):
            s_ref[bb, 0] = s_i

    def emit(items):
        ns = [slot(bb, c) for bb, c in items]
        ws_qs = [_dot(wq_s[n], sall_s[n].astype(BF16)) for n in ns]
        v_new = [u_s[n] - a[:C] for n, a in zip(ns, ws_qs)]
        o = [a[C:] + _dot(att_s[n], vn.astype(BF16)) for n, a, vn in zip(ns, ws_qs, v_new)]
        o = [_rmsnorm(a, nw) for a in o]
        for (bb, c), a in zip(items, o):
            if is_first(c):
                rows = C - n_pad
                o_ref[bb, 0:rows, :] = (a[n_pad:] * _silu(z_ref[bb, 0:rows, :])).astype(BF16)
            else:
                r0 = c * C - n_pad if isinstance(c, int) else pl.multiple_of(c * C - n_pad, 2 * SUBLANES)
                o_ref[bb, pl.ds(r0, C), :] = (a * _silu(z_ref[bb, pl.ds(r0, C), :])).astype(BF16)

    lead = (n_chunks - 1) % group + 1
    if lead == 1 and n_chunks > 1:
        lead += group
    leading = [(bb, c) for bb in range(bb_count) for c in range(lead)]

    n_loop = (n_chunks - lead) // group
    later = lambda i: [(0, lead + i * group + j) for j in range(group)]
    carry_range = lambda lo, hi: lax.fori_loop(lo, hi, lambda c, _: (carry([(0, c)]), 0)[1], 0)

    for bb in range(bb_count):
        for x_s, x_ref, cs_ref in ((xq_s, xq_ref, cs_q_ref), (xk_s, xk_ref, cs_k_ref),
                                   (xv_s, xv_ref, cs_v_ref)):
            x_s[bb, 0:head, :] = jnp.zeros((head, HEAD_DIM), F32)
            x_s[bb, head - (GDN_CONV - 1):head, :] = cs_ref[bb]
            x_s[bb, head:head + seq, :] = x_ref[bb]
        if n_pad:
            g_s[bb, 0:n_pad, :] = jnp.zeros((n_pad, HEAD_DIM), F32)
            b_s[bb, 0:n_pad, :] = jnp.zeros((n_pad, HEAD_DIM), F32)
        pieces = []
        rest = gb_ref[bb]
        for _ in range(3):
            hi, rest = _split(rest)
            pieces.append(hi)
        both = _dot(pieces[0], pick_gb) + (_dot(pieces[1], pick_gb) + _dot(pieces[2], pick_gb))
        g_s[bb, n_pad:n_pad + seq, :] = both[:, :HEAD_DIM]
        b_s[bb, n_pad:n_pad + seq, :] = both[:, HEAD_DIM:]
        s_ref[bb, 0] = s0_ref[bb, 0]

    build(leading)
    carry([(bb, 0) for bb in range(bb_count)])

    def build_later(i, _):
        build(later(i), [functools.partial(carry, [(0, 1 + i * group + j)]) for j in range(group)])
        return 0

    if n_loop:
        lax.fori_loop(0, n_loop, build_later, 0)
    if n_chunks > 1:
        carry_range(1 + n_loop * group, n_chunks)
    emit(leading)
    if n_loop:
        lax.fori_loop(0, n_loop, lambda i, _: (emit(later(i)), 0)[1], 0)


def _gdn(qkv, gb, z, conv_state, s0, conv_w, norm_w, *, chunk, n_pad, bb_count, layer=None, s_prev=None):
    b, seq, _ = qkv.shape
    n_chunks = (seq + n_pad) // chunk
    assert n_chunks * chunk == seq + n_pad and b % bb_count == 0
    assert n_chunks == 1 or bb_count == 1
    col = lambda off: (lambda i, h: (i, 0, off + h))
    wcol = lambda off: (lambda i, h: (0, off + h))
    seq_blk = (bb_count, seq, HEAD_DIM)
    cs_blk = (bb_count, GDN_CONV - 1, HEAD_DIM)
    if layer is None:
        s_blk = (bb_count, 1, HEAD_DIM, HEAD_DIM)
        s_map = lambda i, h: (i, h, 0, 0)
    else:
        s_blk = (None, bb_count, 1, HEAD_DIM, HEAD_DIM)
        s_map = lambda i, h: (layer, i, h, 0, 0)
    ext = chunk * n_chunks
    per_chunk = lambda r, c, dt=F32: pltpu.VMEM((bb_count * n_chunks, r, c), dt)
    n_in = 13
    extra = [] if s_prev is None else [s_prev]

    def kern(*refs):
        _gdn_kernel(*refs[:n_in], *refs[n_in + len(extra):], chunk=chunk, n_pad=n_pad, seq=seq, bb_count=bb_count)

    return pl.pallas_call(
        kern,
        grid=(b // bb_count, GDN_HEADS),
        in_specs=[
            pl.BlockSpec(seq_blk, col(0)),
            pl.BlockSpec(seq_blk, col(GDN_HEADS)),
            pl.BlockSpec(seq_blk, col(2 * GDN_HEADS)),
            pl.BlockSpec((bb_count, seq, LANES), lambda i, h: (i, 0, 0)),
            pl.BlockSpec(seq_blk, col(0)),
            pl.BlockSpec(cs_blk, col(0)),
            pl.BlockSpec(cs_blk, col(GDN_HEADS)),
            pl.BlockSpec(cs_blk, col(2 * GDN_HEADS)),
            pl.BlockSpec(s_blk, s_map),
            pl.BlockSpec((GDN_CONV, HEAD_DIM), wcol(0)),
            pl.BlockSpec((GDN_CONV, HEAD_DIM), wcol(GDN_HEADS)),
            pl.BlockSpec((GDN_CONV, HEAD_DIM), wcol(2 * GDN_HEADS)),
            pl.BlockSpec((1, HEAD_DIM), lambda i, h: (0, 0)),
        ] + [pl.BlockSpec(memory_space=pl.ANY) for _ in extra],
        out_specs=[pl.BlockSpec(seq_blk, col(0)), pl.BlockSpec(s_blk, s_map)],
        out_shape=[jax.ShapeDtypeStruct((b, seq, GDN_WIDTH), BF16),
                   jax.ShapeDtypeStruct(s0.shape, F32)],
        input_output_aliases={n_in: 1} if extra else {},
        scratch_shapes=[
            pltpu.VMEM((bb_count, ext + 8, HEAD_DIM), F32),
            pltpu.VMEM((bb_count, ext + 8, HEAD_DIM), F32),
            pltpu.VMEM((bb_count, ext + 8, HEAD_DIM), F32),
            pltpu.VMEM((bb_count, ext, HEAD_DIM), F32),
            pltpu.VMEM((bb_count, ext, HEAD_DIM), F32),
            per_chunk(chunk, HEAD_DIM),
            per_chunk(chunk, chunk, BF16),
            per_chunk(2 * chunk, HEAD_DIM, BF16),
            per_chunk(HEAD_DIM, HEAD_DIM, BF16),
            per_chunk(HEAD_DIM, HEAD_DIM),
            per_chunk(1, HEAD_DIM),
            per_chunk(HEAD_DIM, HEAD_DIM),
        ],
        compiler_params=_params("parallel", "parallel"),
        name="gdn",
    )(qkv, qkv, qkv, gb, z, conv_state, conv_state, conv_state, s0,
      conv_w, conv_w, conv_w, norm_w, *extra)


CONF_HALO = 32
CONF_ROWS = 16


def _conf_out_seq_kernel(glu_ref, o_ref, h_ref, cstate_ref, dw_ref, dwb_ref, lnw_ref, lnb_ref, wout_ref,
                         out_ref, ext_s, c_s, sh_s, *, tt):
    t = pl.program_id(1)
    lead = CONF_HALO - (CONF_CONV - 1)

    @pl.when(t == 0)
    def _():
        ext_s[0:lead, :] = jnp.zeros((lead, CONF_WIDTH), F32)
        ext_s[lead:CONF_HALO, :] = cstate_ref[0]

    @pl.when(t > 0)
    def _():
        ext_s[0:CONF_HALO, :] = ext_s[tt:tt + CONF_HALO, :]

    ext_s[CONF_HALO:CONF_HALO + tt, :] = glu_ref[0]
    dw = dw_ref[...]
    bias = dwb_ref[...]

    def rows(i, _):
        r0 = pl.multiple_of(i * CONF_ROWS, SUBLANES)
        blk = ext_s[pl.ds(r0, CONF_ROWS + CONF_HALO), :]
        span = CONF_ROWS + CONF_HALO - SUBLANES
        for s in range(1, SUBLANES):
            sh_s[s - 1] = blk[s:s + span]
        acc = bias
        for j in range(CONF_CONV):
            q, s = divmod(lead + j, SUBLANES)
            rows_j = slice(q * SUBLANES, q * SUBLANES + CONF_ROWS)
            acc = acc + (blk[rows_j] if s == 0 else sh_s[s - 1, rows_j, :]) * dw[j:j + 1]
        c_s[pl.ds(r0, CONF_ROWS), :] = acc
        return 0

    lax.fori_loop(0, tt // CONF_ROWS, rows, 0)
    c = c_s[...]
    mu = jnp.mean(c, axis=-1, keepdims=True)
    xc = c - mu
    var = jnp.mean(xc * xc, axis=-1, keepdims=True)
    c = _silu(xc * lax.rsqrt(var + EPS) * lnw_ref[...] + lnb_ref[...])
    mixed = jnp.concatenate([o_ref[0], c.astype(BF16)], axis=1)
    out_ref[...] = h_ref[...] + _dot(mixed, wout_ref[...])


def _conf_out_seq(glu, o, h, cstate, dw, dwb, lnw, lnb, wout):
    b, seq, _ = glu.shape
    tt = _pick_tile(seq, (688, 512, 256, 128, 64, 16))
    per_seq = seq // tt
    tile = lambda w: pl.BlockSpec((1, tt, w), lambda i, t: (i, t, 0))
    flat = pl.BlockSpec((tt, D_MODEL), lambda i, t: (i * per_seq + t, 0))
    fixed = lambda a: pl.BlockSpec(a.shape, lambda i, t: (0,) * a.ndim)
    return pl.pallas_call(
        functools.partial(_conf_out_seq_kernel, tt=tt),
        grid=(b, per_seq),
        in_specs=[tile(CONF_WIDTH), tile(GDN_WIDTH), flat,
                  pl.BlockSpec((1, CONF_CONV - 1, CONF_WIDTH), lambda i, t: (i, 0, 0)),
                  fixed(dw), fixed(dwb), fixed(lnw), fixed(lnb), fixed(wout)],
        out_specs=flat,
        out_shape=jax.ShapeDtypeStruct(h.shape, F32),
        input_output_aliases={2: 0},
        scratch_shapes=[pltpu.VMEM((tt + CONF_HALO, CONF_WIDTH), F32),
                        pltpu.VMEM((tt, CONF_WIDTH), F32),
                        pltpu.VMEM((SUBLANES - 1, CONF_ROWS + CONF_HALO - SUBLANES, CONF_WIDTH), F32)],
        compiler_params=_params("parallel", "arbitrary"),
        name="conf_out_seq",
    )(glu, o, h, cstate, dw, dwb, lnw, lnb, wout)


def _conf_out_step_kernel(cext_ref, o_ref, h_ref, dw_ref, dwb_ref, lnw_ref, lnb_ref, wout_ref, out_ref,
                          *, steps):
    t = pl.program_id(0)

    @pl.when(t < steps)
    def _():
        dw = dw_ref[...]
        c = dwb_ref[...] + cext_ref[t] * dw[0:1]
        for j in range(1, CONF_CONV):
            c = c + cext_ref[t + j] * dw[j:j + 1]
        mu = jnp.mean(c, axis=-1, keepdims=True)
        xc = c - mu
        var = jnp.mean(xc * xc, axis=-1, keepdims=True)
        c = _silu(xc * lax.rsqrt(var + EPS) * lnw_ref[...] + lnb_ref[...])
        mixed = jnp.concatenate([o_ref[...], c.astype(BF16)], axis=1)
        out_ref[...] = h_ref[...] + _dot(mixed, wout_ref[...])

    @pl.when(t >= steps)
    def _():
        out_ref[...] = jnp.zeros_like(out_ref)


def _conf_out_step(cext, o, h, row0, n_zero, dw, dwb, lnw, lnb, wout):
    steps = cext.shape[0] - (CONF_CONV - 1)
    b = cext.shape[1]
    assert row0 % b == 0 and n_zero % b == 0 and row0 + steps * b + n_zero <= h.shape[0]
    full = lambda a: pl.BlockSpec(a.shape, lambda t: (0,) * a.ndim)
    rows = pl.BlockSpec((b, D_MODEL), lambda t: (row0 // b + t, 0))
    return pl.pallas_call(
        functools.partial(_conf_out_step_kernel, steps=steps),
        grid=(steps + n_zero // b,),
        in_specs=[full(cext), pl.BlockSpec((b, GDN_WIDTH), lambda t: (jnp.minimum(t, steps - 1), 0)), rows,
                  full(dw), full(dwb), full(lnw), full(lnb), full(wout)],
        out_specs=rows,
        out_shape=jax.ShapeDtypeStruct(h.shape, F32),
        input_output_aliases={2: 0},
        compiler_params=_params("arbitrary"),
        name="conf_out_step",
    )(cext, o, h, dw, dwb, lnw, lnb, wout)


N_RANKS = PEER_TOPK + 1
RANK_ROWS = 24
NEG_INF = float("-inf")


def _top_rows(s, count):
    rows = []
    for r in range(count):
        m = jnp.max(s, axis=0, keepdims=True)
        rows.append(m)
        s = jnp.where(s == m, NEG_INF, s)
    return rows


TOP_WAYS = 4


def _top_rows_chained(s, count):
    n = s.shape[0] // TOP_WAYS
    lv = [s[i * n:(i + 1) * n] for i in range(TOP_WAYS)]
    for i, j in ((0, 1), (2, 3), (0, 2), (1, 3), (1, 2)):
        lv[i], lv[j] = jnp.maximum(lv[i], lv[j]), jnp.minimum(lv[i], lv[j])
    rows = []
    for r in range(count):
        m = jnp.max(lv[0], axis=0, keepdims=True)
        rows.append(m)
        hit = lv[0] == m
        for i in range(TOP_WAYS - 1):
            lv[i] = jnp.where(hit, lv[i + 1], lv[i])
        lv[-1] = jnp.where(hit, NEG_INF, lv[-1])
    return rows


def _prefix_count(hits):
    out = jnp.zeros(hits[0].shape, F32)
    for r, hit in enumerate(hits):
        out = jnp.where(hit, float(r + 1), out)
    return out


def _stack_rows(rows, height, tm):
    idx = lax.broadcasted_iota(jnp.int32, (height, tm), 0)
    out = jnp.full((height, tm), NEG_INF, F32)
    for r, row in enumerate(rows):
        out = jnp.where(idx == r, row, out)
    return out


def _peer_score_kernel(h_ref, nw_ref, wqt_ref, keys_ref, xnt_ref, cnt_ref, e1_ref, rank2_ref, e2_ref):
    tm = h_ref.shape[0]
    xn = _rmsnorm(h_ref[...], nw_ref[...])
    xnt = xn.T.astype(BF16)
    xnt_ref[...] = xnt
    qt = _dot(wqt_ref[...], xnt).astype(BF16)
    idx8 = lax.broadcasted_iota(jnp.int32, (SUBLANES, tm), 0)
    for hh in range(PEER_HEADS):
        s = []
        for p in range(2):
            r0 = (hh * 2 + p) * N_KEYS
            s.append(_dot(keys_ref[hh, p], qt[r0:r0 + N_KEYS, :]))
        a = _top_rows_chained(s[0], N_RANKS)
        b = _top_rows_chained(s[1], N_RANKS)
        a_st = _stack_rows(a, RANK_ROWS, tm)
        b_st = _stack_rows(b, RANK_ROWS, tm)
        cands = [a[0] + b_st,
                 jnp.where(lax.broadcasted_iota(jnp.int32, (RANK_ROWS, tm), 0) >= 1,
                           a_st + b[0], NEG_INF)]
        for r1 in range(1, N_RANKS):
            hi = N_RANKS // (r1 + 1) - 1
            if hi >= 1 and r1 <= 4:
                cands.append(jnp.where((idx8 >= 1) & (idx8 <= hi), a[r1] + b_st[0:SUBLANES], NEG_INF))
        cands.append(jnp.where((idx8 >= 5) & (idx8 <= N_RANKS // 2 - 1),
                               a_st[0:SUBLANES] + b[1], NEG_INF))
        cand = jnp.concatenate(cands, axis=0)
        top = _top_rows(cand, N_RANKS)
        z = jnp.zeros_like(top[0])
        for r in range(PEER_TOPK):
            z = z + jnp.exp(top[r] - top[0])
        thr = 0.5 * (top[PEER_TOPK - 1] + top[PEER_TOPK])
        cnt1 = _prefix_count([s[0] >= thr - b[r2] for r2 in range(PEER_TOPK)])
        rank2 = _prefix_count([s[1] < b[r] for r in range(PEER_TOPK)])
        cnt_ref[hh] = cnt1
        e1_ref[hh] = 0.5 * jnp.exp(s[0] - a[0]) / z
        rank2_ref[hh] = rank2.astype(BF16)
        e2_ref[hh] = jnp.exp(s[1] - b[0]).astype(BF16)


def _peer_score(h, nw, wqt, keys):
    n = h.shape[0]
    tm = _pick_tile(n, (PEER_TOKENS, 256, 128))
    heads = lambda: pl.BlockSpec((PEER_HEADS, N_KEYS, tm), lambda i: (0, 0, i))
    return pl.pallas_call(
        _peer_score_kernel,
        grid=(n // tm,),
        in_specs=[pl.BlockSpec((tm, D_MODEL), lambda i: (i, 0)),
                  pl.BlockSpec((1, D_MODEL), lambda i: (0, 0)),
                  pl.BlockSpec(wqt.shape, lambda i: (0, 0)),
                  pl.BlockSpec(keys.shape, lambda i: (0, 0, 0, 0))],
        out_specs=[pl.BlockSpec((D_MODEL, tm), lambda i: (0, i)), heads(), heads(), heads(), heads()],
        out_shape=[jax.ShapeDtypeStruct((D_MODEL, n), BF16)]
        + [jax.ShapeDtypeStruct((PEER_HEADS, N_KEYS, n), dt) for dt in (F32, F32, BF16, BF16)],
        compiler_params=_params("parallel"),
        name="peer_score",
    )(h, nw, wqt, keys)


EXPERT_CHUNK = 1024
PEER_TOKENS = 512
PEER_DENSE_TOKENS = 512
SQRT_HALF = math.sqrt(0.5)


def _peer_dense_kernel(xnt_ref, cnt_ref, e1_ref, rank2_ref, e2_ref, u_ref, vt_ref, h_ref, fw_ref,
                       out_ref, acc_s, w_s, gate_s, cnt_s, e1_s, *, final_norm, n_chunks):
    s = pl.program_id(0)
    c = lax.rem(s, n_chunks)
    cur = lax.rem(c, 2)
    tm = xnt_ref.shape[1]
    half = tm // 2

    @pl.when(s == 0)
    def _():
        acc_s[...] = jnp.zeros_like(acc_s)
        w_s[1] = jnp.zeros(w_s.shape[1:], BF16)

    prev = 1 - cur
    n_blocks = EXPERT_CHUNK // N_KEYS
    n_lt = tm // LANES
    per_lt = PEER_HEADS * n_blocks
    for l in range(n_lt):
        cnt_s[l * per_lt:(l + 1) * per_lt, :] = cnt_ref[:, :, l * LANES:(l + 1) * LANES].reshape(per_lt, LANES)
        e1_s[l * per_lt:(l + 1) * per_lt, :] = e1_ref[:, :, l * LANES:(l + 1) * LANES].reshape(per_lt, LANES)

    def rows(ref, hh, j):
        words = jnp.concatenate([jnp.broadcast_to(ref[pl.ds(l * per_lt + hh * n_blocks + j, 1), :], (SUBLANES, LANES))
                                 for l in range(n_lt)], axis=1)
        packed = jnp.concatenate([words, words], axis=0).astype(BF16)
        return jnp.concatenate([packed] * (N_KEYS // (2 * SUBLANES)), axis=0)

    xnt = xnt_ref[...]
    block = lambda j: slice(j * N_KEYS, (j + 1) * N_KEYS)

    def gate(j):
        total = None
        for hh in range(PEER_HEADS):
            e2 = e2_ref[hh]
            sel = jnp.where(rank2_ref[hh] < rows(cnt_s, hh, j), e2, jnp.zeros_like(e2))
            term = rows(e1_s, hh, j) * sel
            total = term if total is None else total + term
        gate_s[block(j), :] = total

    def project(j):
        return _dot(u_ref[block(j), :], xnt)

    def finish(j, pre):
        act = (pre * (1.0 + lax.erf(pre * SQRT_HALF))).astype(BF16)
        w_s[cur, block(j), :] = act * gate_s[block(j), :]

    ahead = 3
    pres = {j: project(j) for j in range(ahead)}
    gate(0)
    for j in range(n_blocks):
        if j + ahead < n_blocks:
            pres[j + ahead] = project(j + ahead)
        if j == 0:
            acc_s[:, :half] += _dot(vt_ref[0], w_s[prev, :, :half])
        if j == n_blocks // 2:
            acc_s[:, half:] += _dot(vt_ref[0], w_s[prev, :, half:])
        if j + 1 < n_blocks:
            gate(j + 1)
        finish(j, pres.pop(j))

    @pl.when(c == 0)
    def _():
        y = h_ref[...] + acc_s[...].T
        if final_norm:
            y = _rmsnorm(y, fw_ref[...])
        out_ref[...] = y
        acc_s[...] = jnp.zeros_like(acc_s)


def _peer_dense(xnt, cnt, e1, rank2, e2, u, vt, h, fw, *, layer, final_norm):
    n = h.shape[0]
    tm = _pick_tile(n, (PEER_DENSE_TOKENS, PEER_TOKENS, 256, 128))
    n_tiles = n // tm
    n_chunks = u.shape[1] // EXPERT_CHUNK
    assert n_chunks % 2 == 0
    rows_per_chunk = EXPERT_CHUNK // N_KEYS
    tile = lambda s: jnp.minimum(s // n_chunks, n_tiles - 1)
    chunk = lambda s: lax.rem(s, n_chunks)
    done = lambda s: jnp.maximum(s - 1, 0) // n_chunks
    by_chunk = lambda: pl.BlockSpec((PEER_HEADS, rows_per_chunk, tm), lambda s: (0, chunk(s), tile(s)))
    by_tile = lambda: pl.BlockSpec((PEER_HEADS, N_KEYS, tm), lambda s: (0, 0, tile(s)))
    return pl.pallas_call(
        functools.partial(_peer_dense_kernel, final_norm=final_norm, n_chunks=n_chunks),
        grid=(n_tiles * n_chunks + 1,),
        in_specs=[pl.BlockSpec((D_MODEL, tm), lambda s: (0, tile(s))),
                  by_chunk(), by_chunk(), by_tile(), by_tile(),
                  pl.BlockSpec((None, EXPERT_CHUNK, D_MODEL), lambda s: (layer, chunk(s), 0)),
                  pl.BlockSpec((None, 1, D_MODEL, EXPERT_CHUNK), lambda s: (layer, chunk(s + n_chunks - 1), 0, 0)),
                  pl.BlockSpec((tm, D_MODEL), lambda s: (done(s), 0)),
                  pl.BlockSpec((1, D_MODEL), lambda s: (0, 0))],
        out_specs=pl.BlockSpec((tm, D_MODEL), lambda s: (done(s), 0)),
        out_shape=jax.ShapeDtypeStruct((n, D_MODEL), F32),
        scratch_shapes=[pltpu.VMEM((D_MODEL, tm), F32), pltpu.VMEM((2, EXPERT_CHUNK, tm), BF16),
                        pltpu.VMEM((EXPERT_CHUNK, tm), BF16)]
        + [pltpu.VMEM((PEER_HEADS * rows_per_chunk * tm // LANES, LANES), F32)] * 2,
        compiler_params=_params("arbitrary"),
        name="peer_dense",
    )(xnt, cnt, e1, rank2, e2, u, vt, h, fw)


def _prep_layer(layer, norm_mix_w, w_in, conv_qkv_w, a_log, dt_bias, gdn_norm_w, conf_dw_w, conf_dw_b,
                conf_ln_w, conf_ln_b, w_out, norm_ffn_w, w_query, sub_keys):
    s0 = QKV_WIDTH
    s1 = s0 + GDN_WIDTH
    s2 = s1 + GDN_HEADS
    s3 = s2 + GDN_HEADS
    w = w_in[layer]
    row = lambda v: v.reshape(1, -1)
    return dict(
        norm_mix_w=row(norm_mix_w[layer]),
        wqkvz=w[:, :s1].astype(BF16),
        wab=jnp.pad(w[:, s1:s3], ((0, 0), (0, LANES - 2 * GDN_HEADS))).astype(BF16),
        wglu=w[:, s3:].astype(BF16),
        alog=row(jnp.pad(a_log[layer], (0, LANES - GDN_HEADS))),
        dtb=row(jnp.pad(dt_bias[layer], (0, LANES - GDN_HEADS))),
        conv_w=conv_qkv_w[layer],
        gdn_norm_w=row(gdn_norm_w[layer]),
        dw=conf_dw_w[layer], dwb=row(conf_dw_b[layer]),
        lnw=row(conf_ln_w[layer]), lnb=row(conf_ln_b[layer]),
        wout=w_out[layer].astype(BF16),
        norm_ffn_w=row(norm_ffn_w[layer]),
        wqt=w_query[layer].T.astype(BF16),
        keys=sub_keys[layer].astype(BF16),
    )


def _peer(h, lw, u, vt, fw, layer, final_norm):
    xnt, cnt, e1, rank2, e2 = _peer_score(h, lw["norm_ffn_w"], lw["wqt"], lw["keys"])
    return _peer_dense(xnt, cnt, e1, rank2, e2, u, vt, h, fw, layer=layer, final_norm=final_norm)


def _mix_prompt(h, lw, b, seq):
    n_pad = (-seq) % GDN_CHUNK
    qkv, z, gb, glu = _in_proj(h, lw["norm_mix_w"], lw["wqkvz"], lw["wab"], lw["wglu"], lw["alog"], lw["dtb"],
                               n_rows=b * seq)
    shp = lambda a: a.reshape(b, seq, a.shape[-1])
    qkv, z, gb, glu = shp(qkv), shp(z), shp(gb), shp(glu)
    o, s_new = _gdn(qkv, gb, z,
                    jnp.zeros((b, GDN_CONV - 1, QKV_WIDTH), F32),
                    jnp.zeros((b, GDN_HEADS, HEAD_DIM, HEAD_DIM), F32),
                    lw["conv_w"], lw["gdn_norm_w"], chunk=GDN_CHUNK, n_pad=n_pad, bb_count=1)
    h = _conf_out_seq(glu, o, h, jnp.zeros((b, CONF_CONV - 1, CONF_WIDTH), F32),
                      lw["dw"], lw["dwb"], lw["lnw"], lw["lnb"], lw["wout"])
    return h, s_new, qkv[:, seq - (GDN_CONV - 1):], glu[:, seq - (CONF_CONV - 1):]


def _mix_sample(h, row0, n_zero, lw, b, steps, layer, s0_all, s_prev, qkv_state, conf_state):
    n_pad = (-steps) % SUBLANES
    to_tb = lambda a: jnp.swapaxes(a, 0, 1)
    qkv, z, gb, glu = _in_proj(h[row0:row0 + steps * b], lw["norm_mix_w"], lw["wqkvz"], lw["wab"], lw["wglu"],
                               lw["alog"], lw["dtb"])
    bt = lambda a: to_tb(a.reshape(steps, b, a.shape[-1]))
    qkv_bt = bt(qkv)
    o, s_new = _gdn(qkv_bt, bt(gb), bt(z), qkv_state, s0_all, lw["conv_w"], lw["gdn_norm_w"],
                    chunk=steps + n_pad, n_pad=n_pad, bb_count=16, layer=layer, s_prev=s_prev)
    cext = jnp.concatenate([to_tb(conf_state), glu.reshape(steps, b, CONF_WIDTH)], axis=0)
    h = _conf_out_step(cext, to_tb(o).reshape(steps * b, GDN_WIDTH), h, row0, n_zero,
                       lw["dw"], lw["dwb"], lw["lnw"], lw["lnb"], lw["wout"])
    qkv_ext = jnp.concatenate([qkv_state, qkv_bt], axis=1)
    return (h, s_new, qkv_ext[:, qkv_ext.shape[1] - (GDN_CONV - 1):],
            to_tb(cext[cext.shape[0] - (CONF_CONV - 1):]))


def _assemble_kernel(meta_ref, x_ref, tail_ref, out_ref, *, n_seq, n_meta, n_tail):
    i = pl.program_id(0)

    @pl.when(i < n_seq)
    def _():
        out_ref[0:n_meta, :] = meta_ref[...]
        out_ref[n_meta:, :] = x_ref[0]

    @pl.when(i == n_seq)
    def _():
        out_ref[0:n_tail, :] = tail_ref[...]


def _assemble(meta, x, tail):
    b, seq0, d = x.shape
    n_meta, n_tail = meta.shape[0], tail.shape[0]
    seq = seq0 + n_meta
    assert 0 < n_tail <= seq and n_meta % SUBLANES == 0 and n_tail % SUBLANES == 0
    return pl.pallas_call(
        functools.partial(_assemble_kernel, n_seq=b, n_meta=n_meta, n_tail=n_tail),
        grid=(b + 1,),
        in_specs=[pl.BlockSpec((n_meta, d), lambda i: (0, 0)),
                  pl.BlockSpec((1, seq0, d), lambda i: (jnp.minimum(i, b - 1), 0, 0)),
                  pl.BlockSpec((n_tail, d), lambda i: (0, 0))],
        out_specs=pl.BlockSpec((seq, d), lambda i: (i, 0)),
        out_shape=jax.ShapeDtypeStruct((b * seq + n_tail, d), x.dtype),
        compiler_params=_params("arbitrary"),
        name="assemble",
    )(meta, x, tail)


DROP_ROWS = 512


def _drop_meta_kernel(h_ref, y_ref):
    y_ref[0] = h_ref[...]


def _drop_meta(h, b, seq, n_meta):
    real = seq - n_meta
    rows = _pick_tile(real, (DROP_ROWS, 256, 128, 64, 16))
    assert seq % SUBLANES == 0 and n_meta % SUBLANES == 0 and rows % SUBLANES == 0
    source = pl.BlockSpec((pl.Element(rows), pl.Element(D_MODEL)),
                          lambda i, j: (pl.multiple_of(i * seq + n_meta + j * rows, SUBLANES), 0))
    return pl.pallas_call(
        _drop_meta_kernel,
        grid=(b, real // rows),
        in_specs=[source],
        out_specs=pl.BlockSpec((1, rows, D_MODEL), lambda i, j: (i, j, 0)),
        out_shape=jax.ShapeDtypeStruct((b, real, D_MODEL), h.dtype),
        compiler_params=_params("parallel", "parallel"),
        name="drop_meta",
    )(h)


def kernel(x_prompt, x_sample, state_gdn, state_qkv_conv, state_conf_conv, meta_tokens, norm_mix_w, w_in,
           conv_qkv_w, a_log, dt_bias, gdn_norm_w, conf_dw_w, conf_dw_b, conf_ln_w, conf_ln_b, w_out,
           norm_ffn_w, w_query, sub_keys, expert_u, expert_v, final_norm_w):
    depth = w_in.shape[0]
    layers = [_prep_layer(l, norm_mix_w, w_in, conv_qkv_w, a_log, dt_bias, gdn_norm_w, conf_dw_w,
                          conf_dw_b, conf_ln_w, conf_ln_b, w_out, norm_ffn_w, w_query, sub_keys)
              for l in range(depth)]
    u_all = expert_u.astype(BF16)
    vt_all = jnp.swapaxes(expert_v.astype(BF16).reshape(depth, -1, EXPERT_CHUNK, D_MODEL), 2, 3)
    fw = final_norm_w.reshape(1, D_MODEL)
    b_p, seq0, _ = x_prompt.shape
    seq = seq0 + N_META
    b_s, steps, _ = x_sample.shape
    n_p, n_s = b_p * seq, b_s * steps
    n_fill = (-(n_p + n_s)) % PEER_DENSE_TOKENS
    h = _assemble(meta_tokens, x_prompt,
                  jnp.concatenate([jnp.swapaxes(x_sample, 0, 1).reshape(n_s, D_MODEL),
                                   jnp.zeros((n_fill, D_MODEL), F32)], axis=0))
    outs_p, outs_s = [], []
    s_sample = None
    for li, lw in enumerate(layers):
        h, *st_p = _mix_prompt(h, lw, b_p, seq)
        h, s_sample, *st_s = _mix_sample(h, n_p, n_fill, lw, b_s, steps, li, state_gdn, s_sample,
                                         state_qkv_conv[li], state_conf_conv[li])
        outs_p.append(st_p)
        outs_s.append(st_s)
        h = _peer(h, lw, u_all, vt_all, fw, li, li == depth - 1)
    y_p = _drop_meta(h, b_p, seq, N_META)
    y_s = jnp.swapaxes(h[n_p:n_p + n_s].reshape(steps, b_s, D_MODEL), 0, 1)
    stack = lambda outs, i: jnp.stack([o[i] for o in outs])
    return (y_p, y_s, stack(outs_p, 0), stack(outs_p, 1), stack(outs_p, 2),
            s_sample, stack(outs_s, 0), stack(outs_s, 1))
```
